```python
import math
import jax, jax.numpy as jnp
from jax import lax
import numpy as np

D_MODEL = 4096
BATCH = 4
SEQ = 2048
DEPTH = 2

GRID_W = 64
CTX_LEN = 256
HEAD_DIM = 128
MIX_WIDTH = D_MODEL
ROPE_THETA = 10000.0
EPS = 1e-6
N_MOD = 9
FFN_DIM = 2 * D_MODEL

WIN_Q_HEADS = 12
WIN_KV_HEADS = 4
WIN_GROUP = WIN_Q_HEADS // WIN_KV_HEADS
WIN_WIDTH = WIN_Q_HEADS * HEAD_DIM
WINDOW = 128
BLOCK = 128

POOL_WINDOWS = (2, 4, 8, 16)
POOL_WIDTH = 1024
POOL_GROUP = POOL_WIDTH // len(POOL_WINDOWS)

DIFF_HEADS = 6
DIFF_V_DIM = 2 * HEAD_DIM
DIFF_WIDTH = DIFF_HEADS * DIFF_V_DIM

IN_SPLITS = (WIN_WIDTH,
             WIN_KV_HEADS * HEAD_DIM,
             WIN_KV_HEADS * HEAD_DIM,
             POOL_WIDTH,
             2 * DIFF_HEADS * HEAD_DIM,
             2 * DIFF_HEADS * HEAD_DIM,
             DIFF_WIDTH)
IN_WIDTH = 8192

kernel_name = "hybrid_parallel_window_pool_diffattn_dit_block"


def rms_norm(x, g):
    xf = x.astype(jnp.float32)
    y = xf * lax.rsqrt(jnp.mean(xf * xf, axis=-1, keepdims=True) + EPS)
    return (y * g.astype(jnp.float32)).astype(x.dtype)


def modulate(h, shift, scale):
    return h * (1 + scale) + shift


def swiglu(h, wg, wu, wd):
    return (jax.nn.silu(h @ wg) * (h @ wu)) @ wd


def half_ffn(s, shift, scale, gate, g_pre, g_post, wg, wu, wd):
    h = modulate(rms_norm(s, g_pre), shift, scale)
    return s + 0.5 * gate * rms_norm(swiglu(h, wg, wu, wd), g_post)


def axial_rope(L):
    rows = L // GRID_W
    row = jnp.repeat(jnp.arange(rows, dtype=jnp.float32), GRID_W)
    col = jnp.tile(jnp.arange(GRID_W, dtype=jnp.float32), rows)
    half = HEAD_DIM // 2
    inv = ROPE_THETA ** (-jnp.arange(0, half, 2, dtype=jnp.float32) / half)
    ar = row[:, None] * inv
    ac = col[:, None] * inv
    ang = jnp.concatenate([ar, ar, ac, ac], axis=-1)
    return jnp.cos(ang), jnp.sin(ang)


def _rotate_half(u):
    a, b = jnp.split(u, 2, axis=-1)
    return jnp.concatenate([-b, a], axis=-1)


def apply_rope(x, cos, sin):
    shape = (1, x.shape[1]) + (1,) * (x.ndim - 3) + (HEAD_DIM,)
    xf = x.astype(jnp.float32)
    xr, xc = jnp.split(xf, 2, axis=-1)
    rot = jnp.concatenate([_rotate_half(xr), _rotate_half(xc)], axis=-1)
    return (xf * cos.reshape(shape) + rot * sin.reshape(shape)).astype(x.dtype)


def split_proj(p):
    B, N = p.shape[:2]
    parts = []
    start = 0
    for size in IN_SPLITS:
        parts.append(p[..., start:start + size])
        start += size
    qa, ka, va, u, qd, kd, vd = parts
    qa = qa.reshape(B, N, WIN_KV_HEADS, WIN_GROUP, HEAD_DIM)
    ka = ka.reshape(B, N, WIN_KV_HEADS, HEAD_DIM)
    va = va.reshape(B, N, WIN_KV_HEADS, HEAD_DIM)
    qd = qd.reshape(B, N, DIFF_HEADS, 2, HEAD_DIM)
    kd = kd.reshape(B, N, DIFF_HEADS, 2, HEAD_DIM)
    vd = vd.reshape(B, N, DIFF_HEADS, DIFF_V_DIM)
    return qa, ka, va, u, qd, kd, vd


def softmax_with_sink(logits, sink):
    sink = jnp.broadcast_to(sink, logits.shape[:-1] + (1,))
    return jax.nn.softmax(jnp.concatenate([logits, sink], axis=-1), axis=-1)[..., :-1]


def window_attn_latent(q, k, v, kc, vc, sink):
    B, L = q.shape[:2]
    nb = L // BLOCK
    scale = HEAD_DIM ** -0.5
    qb = q.reshape(B, nb, BLOCK, WIN_KV_HEADS, WIN_GROUP, HEAD_DIM)

    def band(t):
        tp = jnp.pad(t, ((0, 0), (BLOCK, BLOCK), (0, 0), (0, 0)))
        tp = tp.reshape(B, nb + 2, BLOCK, WIN_KV_HEADS, HEAD_DIM)
        return jnp.concatenate([tp[:, :-2], tp[:, 1:-1], tp[:, 2:]], axis=2)

    kb, vb = band(k), band(v)
    s_loc = jnp.einsum('bnqhgd,bnkhd->bnhgqk', qb, kb).astype(jnp.float32) * scale
    s_ctx = jnp.einsum('bnqhgd,bchd->bnhgqc', qb, kc).astype(jnp.float32) * scale
    blk = jnp.arange(nb)[:, None, None]
    qpos = blk * BLOCK + jnp.arange(BLOCK)[None, :, None]
    kpos = (blk - 1) * BLOCK + jnp.arange(3 * BLOCK)[None, None, :]
    valid = (jnp.abs(kpos - qpos) <= WINDOW) & (kpos >= 0) & (kpos < L)
    s_loc = jnp.where(valid[None, :, None, None], s_loc, -jnp.inf)
    sink_b = sink.astype(jnp.float32).reshape(1, 1, WIN_KV_HEADS, WIN_GROUP, 1, 1)
    p = softmax_with_sink(jnp.concatenate([s_loc, s_ctx], axis=-1), sink_b)
    p_loc = p[..., :3 * BLOCK].astype(v.dtype)
    p_ctx = p[..., 3 * BLOCK:].astype(v.dtype)
    o = (jnp.einsum('bnhgqk,bnkhd->bnqhgd', p_loc, vb)
         + jnp.einsum('bnhgqc,bchd->bnqhgd', p_ctx, vc))
    return o.reshape(B, L, WIN_WIDTH)


def window_attn_ctx(qc, kc, vc, sink):
    B, C = qc.shape[:2]
    s = jnp.einsum('bqhgd,bkhd->bhgqk', qc, kc).astype(jnp.float32) * HEAD_DIM ** -0.5
    p = softmax_with_sink(s, sink.astype(jnp.float32).reshape(1, WIN_KV_HEADS, WIN_GROUP, 1, 1))
    o = jnp.einsum('bhgqk,bkhd->bqhgd', p.astype(vc.dtype), vc)
    return o.reshape(B, C, WIN_WIDTH)


def pool_mix(u, pool_w, pool_scale):
    N = u.shape[1]
    uf = u.astype(jnp.float32)
    cs = jnp.pad(jnp.cumsum(uf, axis=1), ((0, 0), (1, 0), (0, 0)))
    t = jnp.arange(N)
    outs = []
    for g, w in enumerate(POOL_WINDOWS):
        sl = slice(g * POOL_GROUP, (g + 1) * POOL_GROUP)
        lo = jnp.clip(t - w // 2, 0, N)
        hi = jnp.clip(t + w // 2, 0, N)
        seg = cs[..., sl]
        mean = (seg[:, hi] - seg[:, lo]) / (hi - lo).astype(jnp.float32)[None, :, None]
        outs.append((mean - uf[..., sl]).astype(u.dtype) @ pool_w[g])
    return jnp.concatenate(outs, axis=-1) * pool_scale


def diff_attend(q1, q2, k1, k2, v, lam):
    scale = HEAD_DIM ** -0.5
    s1 = jnp.einsum('bqhd,bkhd->bhqk', q1, k1).astype(jnp.float32) * scale
    s2 = jnp.einsum('bqhd,bkhd->bhqk', q2, k2).astype(jnp.float32) * scale
    a = jax.nn.softmax(s1, axis=-1) - lam * jax.nn.softmax(s2, axis=-1)
    return jnp.einsum('bhqk,bkhe->bqhe', a.astype(v.dtype), v)


def diff_attn_latent(q, k, v, kc, vc, lam):
    B, L = q.shape[:2]
    nb = L // BLOCK
    kk = jnp.concatenate([k, kc], axis=1)
    vv = jnp.concatenate([v, vc], axis=1)
    k1, k2 = kk[..., 0, :], kk[..., 1, :]
    qb = q.reshape(B, nb, BLOCK, DIFF_HEADS, 2, HEAD_DIM).transpose(1, 0, 2, 3, 4, 5)

    def block(qblk):
        return diff_attend(qblk[..., 0, :], qblk[..., 1, :], k1, k2, vv, lam)

    o = lax.map(block, qb)
    return o.transpose(1, 0, 2, 3, 4).reshape(B, L, DIFF_HEADS, DIFF_V_DIM)


def diff_out(o, subln, lam_init):
    B, N = o.shape[:2]
    return (rms_norm(o, subln) * (1.0 - lam_init)).reshape(B, N, DIFF_WIDTH)


def setup_inputs(seed: int = 0) -> dict:
    key = jax.random.key(seed)
    ks = jax.random.split(key, 20)
    f32 = jnp.float32

    def nrm(k, shape, scale):
        return jax.random.normal(k, shape, f32) * scale

    return {
        "x": nrm(ks[0], (BATCH, SEQ, D_MODEL), 1.0),
        "c": nrm(ks[1], (BATCH, D_MODEL), 1.0),
        "ctx": nrm(ks[2], (BATCH, CTX_LEN, D_MODEL), 1.0),
        "c_ctx": nrm(ks[3], (D_MODEL,), 1.0),
        "w_ada": nrm(ks[4], (DEPTH, D_MODEL, N_MOD * D_MODEL), 0.5 * D_MODEL ** -0.5),
        "b_ada": nrm(ks[5], (DEPTH, N_MOD * D_MODEL), 0.01),
        "norm_pre": 1.0 + nrm(ks[6], (DEPTH, 3, D_MODEL), 0.05),
        "norm_post": 1.0 + nrm(ks[7], (DEPTH, 3, D_MODEL), 0.05),
        "ffn_w_gate": nrm(ks[8], (DEPTH, 2, D_MODEL, FFN_DIM), D_MODEL ** -0.5),
        "ffn_w_up": nrm(ks[9], (DEPTH, 2, D_MODEL, FFN_DIM), D_MODEL ** -0.5),
        "ffn_w_down": nrm(ks[10], (DEPTH, 2, FFN_DIM, D_MODEL), FFN_DIM ** -0.5),
        "w_in": nrm(ks[11], (DEPTH, D_MODEL, IN_WIDTH), D_MODEL ** -0.5),
        "w_out": nrm(ks[12], (DEPTH, MIX_WIDTH, D_MODEL), MIX_WIDTH ** -0.5),
        "attn_sink": nrm(ks[13], (DEPTH, WIN_Q_HEADS), 0.5),
        "pool_w": nrm(ks[14], (DEPTH, len(POOL_WINDOWS), POOL_GROUP, POOL_GROUP), POOL_GROUP ** -0.5),
        "pool_scale": 1.0 + nrm(ks[15], (DEPTH, POOL_WIDTH), 0.1),
        "diff_lambda": nrm(ks[16], (DEPTH, 4, HEAD_DIM), 0.1),
        "diff_subln": 1.0 + nrm(ks[17], (DEPTH, DIFF_V_DIM), 0.05),
    }


def reference(x, c, ctx, c_ctx, w_ada, b_ada, norm_pre, norm_post, ffn_w_gate, ffn_w_up,
              ffn_w_down, w_in, w_out, attn_sink, pool_w, pool_scale, diff_lambda, diff_subln):
    L = x.shape[1]
    cos, sin = axial_rope(L)
    xc = ctx
    sc = jax.nn.silu(c)
    scc = jax.nn.silu(c_ctx)
    for i in range(DEPTH):
        update_ctx = i < DEPTH - 1
        m = jnp.split((sc @ w_ada[i] + b_ada[i])[:, None, :], N_MOD, axis=-1)
        mc = jnp.split((scc @ w_ada[i] + b_ada[i])[None, None, :], N_MOD, axis=-1)

        x = half_ffn(x, m[0], m[1], m[2], norm_pre[i, 0], norm_post[i, 0],
                     ffn_w_gate[i, 0], ffn_w_up[i, 0], ffn_w_down[i, 0])
        xc = half_ffn(xc, mc[0], mc[1], mc[2], norm_pre[i, 0], norm_post[i, 0],
                      ffn_w_gate[i, 0], ffn_w_up[i, 0], ffn_w_down[i, 0])

        h = modulate(rms_norm(x, norm_pre[i, 1]), m[3], m[4])
        hc = modulate(rms_norm(xc, norm_pre[i, 1]), mc[3], mc[4])
        qa, ka, va, u, qd, kd, vd = split_proj(h @ w_in[i])
        qac, kac, vac, uc, qdc, kdc, vdc = split_proj(hc @ w_in[i])
        qa, ka = apply_rope(qa, cos, sin), apply_rope(ka, cos, sin)
        qd, kd = apply_rope(qd, cos, sin), apply_rope(kd, cos, sin)

        lam_init = 0.8 - 0.6 * math.exp(-0.3 * i)
        dl = diff_lambda[i].astype(jnp.float32)
        lam = (jnp.exp(jnp.sum(dl[0] * dl[1])) - jnp.exp(jnp.sum(dl[2] * dl[3])) + lam_init)

        o_a = window_attn_latent(qa, ka, va, kac, vac, attn_sink[i])
        o_b = pool_mix(u, pool_w[i], pool_scale[i])
        o_c = diff_out(diff_attn_latent(qd, kd, vd, kdc, vdc, lam), diff_subln[i], lam_init)
        y = jnp.concatenate([o_a, o_b, o_c], axis=-1) @ w_out[i]
        x = x + m[5] * rms_norm(y, norm_post[i, 1])

        if update_ctx:
            o_ac = window_attn_ctx(qac, kac, vac, attn_sink[i])
            o_bc = pool_mix(uc, pool_w[i], pool_scale[i])
            o_cc = diff_out(diff_attend(qdc[..., 0, :], qdc[..., 1, :], kdc[..., 0, :],
                                        kdc[..., 1, :], vdc, lam), diff_subln[i], lam_init)
            yc = jnp.concatenate([o_ac, o_bc, o_cc], axis=-1) @ w_out[i]
            xc = xc + mc[5] * rms_norm(yc, norm_post[i, 1])

        x = half_ffn(x, m[6], m[7], m[8], norm_pre[i, 2], norm_post[i, 2],
                     ffn_w_gate[i, 1], ffn_w_up[i, 1], ffn_w_down[i, 1])
        if update_ctx:
            xc = half_ffn(xc, mc[6], mc[7], mc[8], norm_pre[i, 2], norm_post[i, 2],
                          ffn_w_gate[i, 1], ffn_w_up[i, 1], ffn_w_down[i, 1])
    return x
```

```python
import functools
import math

import jax
import jax.numpy as jnp
from jax import lax
from jax.experimental import pallas as pl
from jax.experimental.pallas import tpu as pltpu

D_MODEL = 4096
BATCH = 4
SEQ = 2048
DEPTH = 2
GRID_W = 64
CTX_LEN = 256
HEAD_DIM = 128
ROPE_THETA = 10000.0
EPS = 1e-6
N_MOD = 9
FFN_DIM = 2 * D_MODEL

WIN_Q_HEADS = 12
WIN_KV_HEADS = 4
WIN_GROUP = WIN_Q_HEADS // WIN_KV_HEADS
WIN_WIDTH = WIN_Q_HEADS * HEAD_DIM
WINDOW = 128
BLOCK = 128

POOL_WINDOWS = (2, 4, 8, 16)
POOL_WIDTH = 1024
POOL_GROUP = POOL_WIDTH // len(POOL_WINDOWS)
POOL_HALO = 8

DIFF_HEADS = 6
DIFF_V_DIM = 2 * HEAD_DIM
DIFF_WIDTH = DIFF_HEADS * DIFF_V_DIM
IN_WIDTH = 8192

COL_QA = 0
COL_KA = COL_QA + WIN_WIDTH
COL_VA = COL_KA + WIN_KV_HEADS * HEAD_DIM
COL_U = COL_VA + WIN_KV_HEADS * HEAD_DIM
COL_QD = COL_U + POOL_WIDTH
COL_KD = COL_QD + 2 * DIFF_HEADS * HEAD_DIM
COL_VD = COL_KD + 2 * DIFF_HEADS * HEAD_DIM

F32 = jnp.float32
BF16 = jnp.bfloat16

N_LAT = BATCH * SEQ
N_CTX = BATCH * CTX_LEN
N_ALL = N_LAT + N_CTX

TM = 512
TF = 512
TN_IN = 512
TK_OUT = 512
TN_ADA = 512
TQ_DIFF = 256
ROW_CHUNK = 32
MOD_ROWS = 16
VMEM_LIMIT = 60 * 1024 * 1024

LAT_TILES = N_LAT // TM
ALL_TILES = N_ALL // TM
TILES_PER_BATCH = SEQ // TM


def _mod_row(t):
    return jnp.where(t < LAT_TILES, t // TILES_PER_BATCH, BATCH)


def _dot(a, b):
    return jnp.dot(a, b, preferred_element_type=F32)


def _dot_nt(a, b):
    return lax.dot_general(a, b, (((1,), (1,)), ((), ())), preferred_element_type=F32)


def _for_rows(n_rows, fn):
    def body(i, carry):
        fn(pl.ds(pl.multiple_of(i * ROW_CHUNK, ROW_CHUNK), ROW_CHUNK))
        return carry
    lax.fori_loop(0, n_rows // ROW_CHUNK, body, 0)


def _rms(v, g):
    return v * lax.rsqrt(jnp.mean(v * v, axis=-1, keepdims=True) + EPS) * g


def _params(*sem):
    return pltpu.CompilerParams(dimension_semantics=sem, vmem_limit_bytes=VMEM_LIMIT)


def _ada_kernel(c_ref, w_ref, b_ref, o_ref):
    c = c_ref[...]
    s = (c / (1.0 + jnp.exp(-c))).astype(BF16)
    o_ref[...] = _dot(s, w_ref[...].astype(BF16)) + b_ref[...]


def _ada_call(cc, w_ada, b_ada):
    n_out = N_MOD * D_MODEL
    return pl.pallas_call(
        _ada_kernel,
        grid=(DEPTH, n_out // TN_ADA),
        in_specs=[
            pl.BlockSpec((MOD_ROWS, D_MODEL), lambda i, j: (0, 0)),
            pl.BlockSpec((None, D_MODEL, TN_ADA), lambda i, j: (i, 0, j)),
            pl.BlockSpec((None, 1, TN_ADA), lambda i, j: (i, 0, j)),
        ],
        out_specs=pl.BlockSpec((None, MOD_ROWS, TN_ADA), lambda i, j: (i, 0, j)),
        out_shape=jax.ShapeDtypeStruct((DEPTH, MOD_ROWS, n_out), F32),
        compiler_params=_params("arbitrary", "arbitrary"),
        name="adaln",
    )(cc, w_ada, b_ada.reshape(DEPTH, 1, n_out))


def _mod_spec(layer, group):
    return pl.BlockSpec((None, None, None, 3, 1, D_MODEL),
                        lambda t, j: (layer, _mod_row(t), group, 0, 0, 0))


def _norm_spec(layer, k):
    return pl.BlockSpec((None, None, 1, D_MODEL), lambda t, j: (layer, k, 0, 0))


def _ffn_kernel(s_ref, m_ref, gpre_ref, gpost_ref, wg_ref, wu_ref, wd_ref, o_ref, h_ref):
    f = pl.program_id(1)

    @pl.when(f == 0)
    def _prologue():
        shift, scale1 = m_ref[0], 1.0 + m_ref[1]
        gpre = gpre_ref[...]

        def rows(r):
            h = _rms(s_ref[r, :], gpre) * scale1 + shift
            h_ref[r, :] = h.astype(BF16)
            o_ref[r, :] = jnp.zeros((ROW_CHUNK, D_MODEL), F32)
        _for_rows(TM, rows)

    h = h_ref[...]
    g = _dot(h, wg_ref[...])
    u = _dot(h, wu_ref[...])
    a = (g / (1.0 + jnp.exp(-g)) * u).astype(BF16)
    for c in range(D_MODEL // 1024):
        cs = slice(c * 1024, (c + 1) * 1024)
        o_ref[:, cs] += _dot(a, wd_ref[:, cs])

    @pl.when(f == pl.num_programs(1) - 1)
    def _epilogue():
        half_gate = 0.5 * m_ref[2]
        gpost = gpost_ref[...]

        def rows(r):
            o_ref[r, :] = s_ref[r, :] + half_gate * _rms(o_ref[r, :], gpost)
        _for_rows(TM, rows)


def _ffn_call(s, mods, norm_pre, norm_post, wg, wu, wd, layer, half, n_tiles):
    k = 2 * half
    return pl.pallas_call(
        _ffn_kernel,
        grid=(n_tiles, FFN_DIM // TF),
        in_specs=[
            pl.BlockSpec((TM, D_MODEL), lambda t, f: (t, 0), pipeline_mode=pl.Buffered(1)),
            _mod_spec(layer, k),
            _norm_spec(layer, k),
            _norm_spec(layer, k),
            pl.BlockSpec((None, None, D_MODEL, TF), lambda t, f: (layer, half, 0, f)),
            pl.BlockSpec((None, None, D_MODEL, TF), lambda t, f: (layer, half, 0, f)),
            pl.BlockSpec((None, None, TF, D_MODEL), lambda t, f: (layer, half, f, 0)),
        ],
        out_specs=pl.BlockSpec((TM, D_MODEL), lambda t, f: (t, 0)),
        out_shape=jax.ShapeDtypeStruct((n_tiles * TM, D_MODEL), F32),
        scratch_shapes=[pltpu.VMEM((TM, D_MODEL), BF16)],
        compiler_params=_params("arbitrary", "arbitrary"),
        name=f"ffn_l{layer}h{half}",
    )(s, mods, norm_pre, norm_post, wg, wu, wd)


def _is_rope_tile(j):
    lo = lambda col: col // TN_IN
    return ((j < lo(COL_VA)) | ((j >= lo(COL_QD)) & (j < lo(COL_VD))))


def _inproj_kernel(s_ref, m_ref, gpre_ref, w_ref, cos_ref, sina_ref, sinb_ref, p_ref, u_ref, h_ref):
    t = pl.program_id(0)
    j = pl.program_id(1)

    @pl.when(j == 0)
    def _prologue():
        shift, scale1 = m_ref[0], 1.0 + m_ref[1]
        gpre = gpre_ref[...]

        def rows(r):
            h_ref[r, :] = (_rms(s_ref[r, :], gpre) * scale1 + shift).astype(BF16)
        _for_rows(TM, rows)

    y = _dot(h_ref[...], w_ref[...])
    rope = _is_rope_tile(j) & (t < LAT_TILES)

    @pl.when(rope)
    def _rotated():
        cos, sina, sinb = cos_ref[...], sina_ref[...], sinb_ref[...]
        for hd in range(TN_IN // HEAD_DIM):
            cs = slice(hd * HEAD_DIM, (hd + 1) * HEAD_DIM)
            v = y[:, cs]
            fwd = pltpu.roll(v, HEAD_DIM // 4, 1)
            bwd = pltpu.roll(v, HEAD_DIM - HEAD_DIM // 4, 1)
            p_ref[:, cs] = (v * cos + bwd * sina + fwd * sinb).astype(BF16)

    @pl.when(jnp.logical_not(rope))
    def _plain():
        p_ref[...] = y.astype(BF16)

    @pl.when((j >= COL_U // TN_IN) & (j < COL_QD // TN_IN))
    def _pool_channels():
        u_ref[...] = y


def _inproj_call(s, mods, norm_pre, w_in, cos, sina, sinb, layer):
    u_blk0 = COL_U // TN_IN
    u_blks = POOL_WIDTH // TN_IN
    rope_spec = pl.BlockSpec((TM, HEAD_DIM), lambda t, j: (t % TILES_PER_BATCH, 0))
    return pl.pallas_call(
        _inproj_kernel,
        grid=(ALL_TILES, IN_WIDTH // TN_IN),
        in_specs=[
            pl.BlockSpec((TM, D_MODEL), lambda t, j: (t, 0)),
            _mod_spec(layer, 1),
            _norm_spec(layer, 1),
            pl.BlockSpec((None, D_MODEL, TN_IN), lambda t, j: (layer, 0, j)),
            rope_spec, rope_spec, rope_spec,
        ],
        out_specs=[
            pl.BlockSpec((TM, TN_IN), lambda t, j: (t, j)),
            pl.BlockSpec((TM, TN_IN), lambda t, j: (t, jnp.clip(j - u_blk0, 0, u_blks - 1))),
        ],
        out_shape=[
            jax.ShapeDtypeStruct((N_ALL, IN_WIDTH), BF16),
            jax.ShapeDtypeStruct((N_ALL, POOL_WIDTH), F32),
        ],
        scratch_shapes=[pltpu.VMEM((TM, D_MODEL), BF16)],
        compiler_params=_params("arbitrary", "arbitrary"),
        name=f"inproj_l{layer}",
    )(s, mods, norm_pre, w_in, cos, sina, sinb)


LAT_QBLOCKS = SEQ // BLOCK
CTX_QBLOCKS = CTX_LEN // BLOCK
BAND = 3 * BLOCK


def _win_kernel(sink_ref, q_ref, k_ref, v_ref, kc_ref, vc_ref, o_ref):
    n = pl.program_id(1)
    scale = HEAD_DIM ** -0.5
    row = lax.broadcasted_iota(jnp.int32, (WIN_GROUP * BLOCK, 1), 0)

    def head_slice(i):
        return slice(i * HEAD_DIM, (i + 1) * HEAD_DIM)

    def group_q(h):
        return jnp.concatenate([q_ref[:, head_slice(WIN_GROUP * h + g)] for g in range(WIN_GROUP)], axis=0)

    def group_sink(h):
        s0 = sink_ref[0, WIN_GROUP * h]
        s1 = sink_ref[0, WIN_GROUP * h + 1]
        s2 = sink_ref[0, WIN_GROUP * h + 2]
        return jnp.where(row < BLOCK, s0, jnp.where(row < 2 * BLOCK, s1, s2))

    def store(h, o):
        for g in range(WIN_GROUP):
            o_ref[:, head_slice(WIN_GROUP * h + g)] = o[g * BLOCK:(g + 1) * BLOCK].astype(BF16)

    @pl.when(n < LAT_QBLOCKS)
    def _latent_queries():
        start = pl.multiple_of(jnp.clip((n - 1) * BLOCK, 0, SEQ - BAND), BLOCK)
        qpos = n * BLOCK + (row & (BLOCK - 1))
        kpos = start + lax.broadcasted_iota(jnp.int32, (1, BAND), 1)
        valid = jnp.abs(kpos - qpos) <= WINDOW
        for h in range(WIN_KV_HEADS):
            hs = head_slice(h)
            q = group_q(h)
            s_loc = jnp.where(valid, _dot_nt(q, k_ref[pl.ds(start, BAND), hs]) * scale, -jnp.inf)
            s_ctx = _dot_nt(q, kc_ref[:, hs]) * scale
            sink = group_sink(h)
            m = jnp.maximum(jnp.maximum(jnp.max(s_loc, axis=-1, keepdims=True),
                                        jnp.max(s_ctx, axis=-1, keepdims=True)), sink)
            e_loc = jnp.exp(s_loc - m)
            e_ctx = jnp.exp(s_ctx - m)
            den = (jnp.sum(e_loc, axis=-1, keepdims=True) + jnp.sum(e_ctx, axis=-1, keepdims=True)
                   + jnp.exp(sink - m))
            o = _dot(e_loc.astype(BF16), v_ref[pl.ds(start, BAND), hs]) + _dot(e_ctx.astype(BF16), vc_ref[:, hs])
            store(h, o * (1.0 / den))

    @pl.when(n >= LAT_QBLOCKS)
    def _context_queries():
        for h in range(WIN_KV_HEADS):
            hs = head_slice(h)
            s_ctx = _dot_nt(group_q(h), kc_ref[:, hs]) * scale
            sink = group_sink(h)
            m = jnp.maximum(jnp.max(s_ctx, axis=-1, keepdims=True), sink)
            e_ctx = jnp.exp(s_ctx - m)
            den = jnp.sum(e_ctx, axis=-1, keepdims=True) + jnp.exp(sink - m)
            store(h, _dot(e_ctx.astype(BF16), vc_ref[:, hs]) * (1.0 / den))


def _win_call(p, sink, layer, with_ctx_queries):
    nq = LAT_QBLOCKS + (CTX_QBLOCKS if with_ctx_queries else 0)
    kv_cols = WIN_KV_HEADS * HEAD_DIM

    def q_row(b, n):
        return jnp.where(n < LAT_QBLOCKS, b * LAT_QBLOCKS + n,
                         N_LAT // BLOCK + b * CTX_QBLOCKS + n - LAT_QBLOCKS)

    ctx_row = lambda b: N_LAT // CTX_LEN + b
    return pl.pallas_call(
        _win_kernel,
        grid=(BATCH, nq),
        in_specs=[
            pl.BlockSpec(memory_space=pltpu.SMEM),
            pl.BlockSpec((BLOCK, WIN_WIDTH), lambda b, n: (q_row(b, n), 0)),
            pl.BlockSpec((SEQ, kv_cols), lambda b, n: (b, COL_KA // kv_cols)),
            pl.BlockSpec((SEQ, kv_cols), lambda b, n: (b, COL_VA // kv_cols)),
            pl.BlockSpec((CTX_LEN, kv_cols), lambda b, n: (ctx_row(b), COL_KA // kv_cols)),
            pl.BlockSpec((CTX_LEN, kv_cols), lambda b, n: (ctx_row(b), COL_VA // kv_cols)),
        ],
        out_specs=pl.BlockSpec((BLOCK, WIN_WIDTH), lambda b, n: (q_row(b, n), 0)),
        out_shape=jax.ShapeDtypeStruct((N_ALL if with_ctx_queries else N_LAT, WIN_WIDTH), BF16),
        compiler_params=_params("arbitrary", "arbitrary"),
        name=f"win_attn_l{layer}",
    )(sink[layer].reshape(1, WIN_Q_HEADS), p, p, p, p, p)


LAT_QTILES = SEQ // TQ_DIFF


def _diff_kernel(lam_ref, subln_ref, q_ref, k_ref, v_ref, kc_ref, vc_ref, o_ref, *, lam_init):
    i = pl.program_id(2)
    scale = HEAD_DIM ** -0.5
    dl = lam_ref[...]
    lam = (jnp.exp(jnp.sum(dl[0:1] * dl[1:2], axis=-1, keepdims=True))
           - jnp.exp(jnp.sum(dl[2:3] * dl[3:4], axis=-1, keepdims=True)) + lam_init)

    def attend(with_latent):
        maps = []
        for j in range(2):
            cs = slice(j * HEAD_DIM, (j + 1) * HEAD_DIM)
            q = q_ref[:, cs]
            s_ctx = _dot_nt(q, kc_ref[:, cs]) * scale
            m = jnp.max(s_ctx, axis=-1, keepdims=True)
            e_lat = None
            if with_latent:
                s_lat = _dot_nt(q, k_ref[:, cs]) * scale
                m = jnp.maximum(m, jnp.max(s_lat, axis=-1, keepdims=True))
                e_lat = jnp.exp(s_lat - m)
            e_ctx = jnp.exp(s_ctx - m)
            den = jnp.sum(e_ctx, axis=-1, keepdims=True)
            if with_latent:
                den = den + jnp.sum(e_lat, axis=-1, keepdims=True)
            maps.append((e_lat, e_ctx, 1.0 / den))
        (e1_lat, e1_ctx, inv1), (e2_lat, e2_ctx, inv2) = maps
        w2 = lam * inv2
        o = _dot((e1_ctx * inv1 - e2_ctx * w2).astype(BF16), vc_ref[...])
        if with_latent:
            o = o + _dot((e1_lat * inv1 - e2_lat * w2).astype(BF16), v_ref[...])
        o_ref[...] = (_rms(o, subln_ref[...]) * (1.0 - lam_init)).astype(BF16)

    @pl.when(i < LAT_QTILES)
    def _latent_queries():
        attend(True)

    @pl.when(i >= LAT_QTILES)
    def _context_queries():
        attend(False)


def _diff_call(p, diff_lambda, diff_subln, layer, with_ctx_queries):
    nq = LAT_QTILES + (CTX_LEN // TQ_DIFF if with_ctx_queries else 0)
    lam_init = 0.8 - 0.6 * math.exp(-0.3 * layer)
    w = DIFF_V_DIM

    def q_row(b, i):
        return jnp.where(i < LAT_QTILES, b * LAT_QTILES + i, N_LAT // TQ_DIFF + b)

    ctx_row = lambda b: N_LAT // CTX_LEN + b
    return pl.pallas_call(
        functools.partial(_diff_kernel, lam_init=lam_init),
        grid=(BATCH, DIFF_HEADS, nq),
        in_specs=[
            pl.BlockSpec((None, 4, HEAD_DIM), lambda b, h, i: (layer, 0, 0)),
            pl.BlockSpec((None, 1, DIFF_V_DIM), lambda b, h, i: (layer, 0, 0)),
            pl.BlockSpec((TQ_DIFF, w), lambda b, h, i: (q_row(b, i), COL_QD // w + h)),
            pl.BlockSpec((SEQ, w), lambda b, h, i: (b, COL_KD // w + h)),
            pl.BlockSpec((SEQ, w), lambda b, h, i: (b, COL_VD // w + h)),
            pl.BlockSpec((CTX_LEN, w), lambda b, h, i: (ctx_row(b), COL_KD // w + h)),
            pl.BlockSpec((CTX_LEN, w), lambda b, h, i: (ctx_row(b), COL_VD // w + h)),
        ],
        out_specs=pl.BlockSpec((TQ_DIFF, w), lambda b, h, i: (q_row(b, i), h)),
        out_shape=jax.ShapeDtypeStruct((N_ALL if with_ctx_queries else N_LAT, DIFF_WIDTH), BF16),
        compiler_params=_params("arbitrary", "arbitrary", "arbitrary"),
        name=f"diff_attn_l{layer}",
    )(diff_lambda, diff_subln.reshape(DEPTH, 1, DIFF_V_DIM), p, p, p, p, p)


def _pool_kernel(u_ref, w_ref, scale_ref, o_ref, pad_ref, *, n_tok):
    t = lax.broadcasted_iota(jnp.int32, (n_tok, 1), 0)
    zeros = jnp.zeros((POOL_HALO, POOL_GROUP), F32)
    pad_ref[0:POOL_HALO, :] = zeros
    pad_ref[POOL_HALO + n_tok:, :] = zeros
    for g, w in enumerate(POOL_WINDOWS):
        cs = slice(g * POOL_GROUP, (g + 1) * POOL_GROUP)
        u = u_ref[:, cs]
        pad_ref[POOL_HALO:POOL_HALO + n_tok, :] = u
        total = pad_ref[POOL_HALO - w // 2:POOL_HALO - w // 2 + n_tok, :]
        for d in range(1 - w // 2, w // 2):
            total = total + pad_ref[POOL_HALO + d:POOL_HALO + d + n_tok, :]
        count = (jnp.clip(t + w // 2, 0, n_tok) - jnp.clip(t - w // 2, 0, n_tok)).astype(F32)
        centred = (total / count - u).astype(BF16)
        o_ref[:, cs] = (_dot(centred, w_ref[g]) * scale_ref[:, cs]).astype(BF16)


def _pool_call(u, pool_w, pool_scale, layer, n_tok, row_block0):
    return pl.pallas_call(
        functools.partial(_pool_kernel, n_tok=n_tok),
        grid=(BATCH,),
        in_specs=[
            pl.BlockSpec((n_tok, POOL_WIDTH), lambda b: (row_block0 + b, 0)),
            pl.BlockSpec((None, len(POOL_WINDOWS), POOL_GROUP, POOL_GROUP), lambda b: (layer, 0, 0, 0)),
            pl.BlockSpec((None, 1, POOL_WIDTH), lambda b: (layer, 0, 0)),
        ],
        out_specs=pl.BlockSpec((n_tok, POOL_WIDTH), lambda b: (b, 0)),
        out_shape=jax.ShapeDtypeStruct((BATCH * n_tok, POOL_WIDTH), BF16),
        scratch_shapes=[pltpu.VMEM((n_tok + 2 * POOL_HALO, POOL_GROUP), F32)],
        compiler_params=_params("arbitrary"),
        name=f"pool_l{layer}_n{n_tok}",
    )(u, pool_w, pool_scale.reshape(DEPTH, 1, POOL_WIDTH))


A_BLKS = WIN_WIDTH // TK_OUT
B_BLKS = POOL_WIDTH // TK_OUT
C_BLKS = DIFF_WIDTH // TK_OUT


def _outproj_kernel(s_ref, m_ref, gpost_ref, oa_ref, ob_ref, oc_ref, w_ref, o_ref):
    c = pl.program_id(1)

    @pl.when(c == 0)
    def _init():
        def rows(r):
            o_ref[r, :] = jnp.zeros((ROW_CHUNK, D_MODEL), F32)
        _for_rows(TM, rows)

    def accumulate(src_ref):
        for n in range(D_MODEL // 1024):
            cs = slice(n * 1024, (n + 1) * 1024)
            o_ref[:, cs] += _dot(src_ref[...], w_ref[:, cs])

    pl.when(c < A_BLKS)(lambda: accumulate(oa_ref))
    pl.when((c >= A_BLKS) & (c < A_BLKS + B_BLKS))(lambda: accumulate(ob_ref))
    pl.when(c >= A_BLKS + B_BLKS)(lambda: accumulate(oc_ref))

    @pl.when(c == pl.num_programs(1) - 1)
    def _epilogue():
        gate = m_ref[2]
        gpost = gpost_ref[...]

        def rows(r):
            o_ref[r, :] = s_ref[r, :] + gate * _rms(o_ref[r, :], gpost)
        _for_rows(TM, rows)


def _outproj_call(s, mods, norm_post, o_a, o_b, o_c, w_out, layer, n_tiles):
    return pl.pallas_call(
        _outproj_kernel,
        grid=(n_tiles, D_MODEL // TK_OUT),
        in_specs=[
            pl.BlockSpec((TM, D_MODEL), lambda t, c: (t, 0)),
            _mod_spec(layer, 1),
            _norm_spec(layer, 1),
            pl.BlockSpec((TM, TK_OUT), lambda t, c: (t, jnp.clip(c, 0, A_BLKS - 1))),
            pl.BlockSpec((TM, TK_OUT), lambda t, c: (t, jnp.clip(c - A_BLKS, 0, B_BLKS - 1))),
            pl.BlockSpec((TM, TK_OUT), lambda t, c: (t, jnp.clip(c - A_BLKS - B_BLKS, 0, C_BLKS - 1))),
            pl.BlockSpec((None, TK_OUT, D_MODEL), lambda t, c: (layer, c, 0)),
        ],
        out_specs=pl.BlockSpec((TM, D_MODEL), lambda t, c: (t, 0)),
        out_shape=jax.ShapeDtypeStruct((n_tiles * TM, D_MODEL), F32),
        compiler_params=_params("arbitrary", "arbitrary"),
        name=f"outproj_l{layer}",
    )(s, mods, norm_post, o_a, o_b, o_c, w_out)


def _rope_tables():
    row = jnp.repeat(jnp.arange(SEQ // GRID_W, dtype=F32), GRID_W)
    col = jnp.tile(jnp.arange(GRID_W, dtype=F32), SEQ // GRID_W)
    half = HEAD_DIM // 2
    inv = ROPE_THETA ** (-jnp.arange(0, half, 2, dtype=F32) / half)
    ar = row[:, None] * inv
    ac = col[:, None] * inv
    ang = jnp.concatenate([ar, ar, ac, ac], axis=-1)
    cos, sin = jnp.cos(ang), jnp.sin(ang)
    low = (jnp.arange(HEAD_DIM) % half) < half // 2
    return cos, jnp.where(low, -sin, 0.0), jnp.where(low, 0.0, sin)


def kernel(x, c, ctx, c_ctx, w_ada, b_ada, norm_pre, norm_post, ffn_w_gate, ffn_w_up, ffn_w_down,
           w_in, w_out, attn_sink, pool_w, pool_scale, diff_lambda, diff_subln):
    s = jnp.concatenate([x.reshape(N_LAT, D_MODEL), ctx.reshape(N_CTX, D_MODEL)], axis=0)
    cc = jnp.zeros((MOD_ROWS, D_MODEL), F32).at[:BATCH].set(c).at[BATCH].set(c_ctx)
    mods = _ada_call(cc, w_ada, b_ada).reshape(DEPTH, MOD_ROWS, 3, 3, 1, D_MODEL)
    npre = norm_pre.reshape(DEPTH, 3, 1, D_MODEL)
    npost = norm_post.reshape(DEPTH, 3, 1, D_MODEL)
    wg, wu, wd = ffn_w_gate.astype(BF16), ffn_w_up.astype(BF16), ffn_w_down.astype(BF16)
    w_in16, w_out16, pool_w16 = w_in.astype(BF16), w_out.astype(BF16), pool_w.astype(BF16)
    cos, sina, sinb = _rope_tables()

    for layer in range(DEPTH):
        update_ctx = layer < DEPTH - 1
        n_tiles = ALL_TILES if update_ctx else LAT_TILES
        s = _ffn_call(s, mods, npre, npost, wg, wu, wd, layer, 0, ALL_TILES)
        p, u = _inproj_call(s, mods, npre, w_in16, cos, sina, sinb, layer)
        o_a = _win_call(p, attn_sink, layer, update_ctx)
        o_c = _diff_call(p, diff_lambda, diff_subln, layer, update_ctx)
        o_b = _pool_call(u, pool_w16, pool_scale, layer, SEQ, 0)
        if update_ctx:
            o_bc = _pool_call(u, pool_w16, pool_scale, layer, CTX_LEN, N_LAT // CTX_LEN)
            o_b = jnp.concatenate([o_b, o_bc], axis=0)
        s = _outproj_call(s, mods, npost, o_a, o_b, o_c, w_out16, layer, n_tiles)
        s = _ffn_call(s, mods, npre, npost, wg, wu, wd, layer, 1, n_tiles)
    return s.reshape(BATCH, SEQ, D_MODEL)
```

```python
import functools
import math

import jax
import jax.numpy as jnp
from jax import lax
from jax.experimental import pallas as pl
from jax.experimental.pallas import tpu as pltpu

D_MODEL = 4096
BATCH = 4
SEQ = 2048
DEPTH = 2
GRID_W = 64
CTX_LEN = 256
HEAD_DIM = 128
ROPE_THETA = 10000.0
EPS = 1e-6
N_MOD = 9
FFN_DIM = 2 * D_MODEL

WIN_Q_HEADS = 12
WIN_KV_HEADS = 4
WIN_GROUP = WIN_Q_HEADS // WIN_KV_HEADS
WIN_WIDTH = WIN_Q_HEADS * HEAD_DIM
WINDOW = 128
BLOCK = 128

POOL_WINDOWS = (2, 4, 8, 16)
POOL_WIDTH = 1024
POOL_GROUP = POOL_WIDTH // len(POOL_WINDOWS)
POOL_HALO = 8

DIFF_HEADS = 6
DIFF_V_DIM = 2 * HEAD_DIM
DIFF_WIDTH = DIFF_HEADS * DIFF_V_DIM
IN_WIDTH = 8192

COL_QA = 0
COL_KA = COL_QA + WIN_WIDTH
COL_VA = COL_KA + WIN_KV_HEADS * HEAD_DIM
COL_U = COL_VA + WIN_KV_HEADS * HEAD_DIM
COL_QD = COL_U + POOL_WIDTH
COL_KD = COL_QD + 2 * DIFF_HEADS * HEAD_DIM
COL_VD = COL_KD + 2 * DIFF_HEADS * HEAD_DIM

F32 = jnp.float32
BF16 = jnp.bfloat16
LANES = 128
SUBLANES = 8
LOG2E = math.log2(math.e)

N_LAT = BATCH * SEQ
N_CTX = BATCH * CTX_LEN
N_ALL = N_LAT + N_CTX

TM = 512
TM_OUT = 256
TF = 512
TN_IN = 1024
RH = 512
MXU_COLS = 256
TN_ADA = 512
TQ_DIFF = 512
TQ_SUB = 256
ROW_CHUNK = 16
MOD_ROWS = 16
VMEM_LIMIT = 60 * 1024 * 1024

LAT_TILES = N_LAT // TM
ALL_TILES = N_ALL // TM


def _mod_row(t, tm):
    return jnp.where(t < N_LAT // tm, t // (SEQ // tm), BATCH)


def _dot(a, b):
    return jnp.dot(a, b, preferred_element_type=F32)


def _dot_nt(a, b):
    return lax.dot_general(a, b, (((1,), (1,)), ((), ())), preferred_element_type=F32)


def _for_rows(n_rows, chunk, fn, unroll=1):
    def body(i, carry):
        fn(pl.ds(pl.multiple_of(i * chunk, chunk), chunk))
        return carry
    lax.fori_loop(0, n_rows // chunk, body, 0, unroll=unroll)


def _rstd_pass(load_rows, stat_ref, n_rows):
    def rows(r):
        v = load_rows(r)
        sq = v * v
        parts = [sq[:, k * LANES:(k + 1) * LANES] for k in range(D_MODEL // LANES)]
        while len(parts) > 1:
            parts = [parts[k] + parts[k + 1] for k in range(0, len(parts), 2)]
        total = jnp.sum(parts[0], axis=-1, keepdims=True)
        rstd = lax.rsqrt(total * (1.0 / D_MODEL) + EPS)
        stat_ref[r, :] = jnp.broadcast_to(rstd, (SUBLANES, LANES))
    _for_rows(n_rows, SUBLANES, rows, unroll=16)


def _rms(v, g):
    return v * lax.rsqrt(jnp.mean(v * v, axis=-1, keepdims=True) + EPS) * g


def _params(*sem):
    return pltpu.CompilerParams(dimension_semantics=sem, vmem_limit_bytes=VMEM_LIMIT)


def _ada_kernel(c_ref, w_ref, b_ref, o_ref):
    c = c_ref[...]
    s = (c / (1.0 + jnp.exp(-c))).astype(BF16)
    o_ref[...] = _dot(s, w_ref[...].astype(BF16)) + b_ref[...]


def _ada_call(cc, w_ada, b_ada):
    n_out = N_MOD * D_MODEL
    return pl.pallas_call(
        _ada_kernel,
        grid=(DEPTH, n_out // TN_ADA),
        in_specs=[
            pl.BlockSpec((MOD_ROWS, D_MODEL), lambda i, j: (0, 0)),
            pl.BlockSpec((None, D_MODEL, TN_ADA), lambda i, j: (i, 0, j)),
            pl.BlockSpec((None, 1, TN_ADA), lambda i, j: (i, 0, j)),
        ],
        out_specs=pl.BlockSpec((None, MOD_ROWS, TN_ADA), lambda i, j: (i, 0, j)),
        out_shape=jax.ShapeDtypeStruct((DEPTH, MOD_ROWS, n_out), F32),
        compiler_params=_params("arbitrary", "arbitrary"),
        name="adaln",
    )(cc, w_ada, b_ada.reshape(DEPTH, 1, n_out))


def _mod_spec(layer, group, tm):
    return pl.BlockSpec((None, None, None, 3, 1, D_MODEL),
                        lambda t, *_: (layer, _mod_row(t, tm), group, 0, 0, 0))


def _norm_spec(layer, k):
    return pl.BlockSpec((None, None, 1, D_MODEL), lambda t, *_: (layer, k, 0, 0))


def _modulated_norm_to(h_ref, s_ref, stat_ref, vec_ref, m_ref, gpre_ref, n_rows, also=None):
    _rstd_pass(lambda r: s_ref[r, :], stat_ref, n_rows)
    full = (ROW_CHUNK, D_MODEL)
    vec_ref[0:ROW_CHUNK, :] = jnp.broadcast_to(gpre_ref[...] * (1.0 + m_ref[1]), full)
    vec_ref[ROW_CHUNK:, :] = jnp.broadcast_to(m_ref[0], full)

    def rows(r):
        rstd = stat_ref[r, :]
        for k in range(D_MODEL // LANES):
            cs = slice(k * LANES, (k + 1) * LANES)
            h = s_ref[r, cs] * rstd * vec_ref[0:ROW_CHUNK, cs] + vec_ref[ROW_CHUNK:, cs]
            h_ref[r, cs] = h.astype(BF16)
        if also is not None:
            also(r)
    _for_rows(n_rows, ROW_CHUNK, rows, unroll=2)


def _gated_residual_to(o_ref, s_ref, stat_ref, vec_ref, gate_gain, n_rows):
    _rstd_pass(lambda r: o_ref[r, :], stat_ref, n_rows)
    vec_ref[0:ROW_CHUNK, :] = jnp.broadcast_to(gate_gain, (ROW_CHUNK, D_MODEL))

    def rows(r):
        rstd = stat_ref[r, :]
        for k in range(D_MODEL // LANES):
            cs = slice(k * LANES, (k + 1) * LANES)
            o_ref[r, cs] = s_ref[r, cs] + o_ref[r, cs] * rstd * vec_ref[0:ROW_CHUNK, cs]
    _for_rows(n_rows, ROW_CHUNK, rows, unroll=2)


def _norm_scratch(n_rows):
    return [pltpu.VMEM((n_rows, LANES), F32), pltpu.VMEM((2 * ROW_CHUNK, D_MODEL), F32)]


def _ffn_kernel(s_ref, m_ref, gpre_ref, gpost_ref, wg_ref, wu_ref, wd_ref, o_ref,
                h_ref, stat_ref, vec_ref):
    f = pl.program_id(1)

    @pl.when(f == 0)
    def _prologue():
        def zero_acc(r):
            o_ref[r, :] = jnp.zeros((ROW_CHUNK, D_MODEL), F32)
        _modulated_norm_to(h_ref, s_ref, stat_ref, vec_ref, m_ref, gpre_ref, TM, also=zero_acc)

    h = h_ref[...]
    g = _dot(h, wg_ref[...])
    u = _dot(h, wu_ref[...])
    a = (g / (1.0 + jnp.exp(-g)) * u).astype(BF16)
    for c in range(D_MODEL // 1024):
        cs = slice(c * 1024, (c + 1) * 1024)
        o_ref[:, cs] += _dot(a, wd_ref[:, cs])

    @pl.when(f == pl.num_programs(1) - 1)
    def _epilogue():
        _gated_residual_to(o_ref, s_ref, stat_ref, vec_ref, 0.5 * m_ref[2] * gpost_ref[...], TM)


def _ffn_call(s, mods, norm_pre, norm_post, wg, wu, wd, layer, half, n_tiles):
    k = 2 * half
    return pl.pallas_call(
        _ffn_kernel,
        grid=(n_tiles, FFN_DIM // TF),
        in_specs=[
            pl.BlockSpec((TM, D_MODEL), lambda t, f: (t, 0), pipeline_mode=pl.Buffered(1)),
            _mod_spec(layer, k, TM),
            _norm_spec(layer, k),
            _norm_spec(layer, k),
            pl.BlockSpec((None, None, D_MODEL, TF), lambda t, f: (layer, half, 0, f)),
            pl.BlockSpec((None, None, D_MODEL, TF), lambda t, f: (layer, half, 0, f)),
            pl.BlockSpec((None, None, TF, D_MODEL), lambda t, f: (layer, half, f, 0)),
        ],
        out_specs=pl.BlockSpec((TM, D_MODEL), lambda t, f: (t, 0)),
        out_shape=jax.ShapeDtypeStruct((n_tiles * TM, D_MODEL), F32),
        scratch_shapes=[pltpu.VMEM((TM, D_MODEL), BF16)] + _norm_scratch(TM),
        compiler_params=_params("arbitrary", "arbitrary"),
        name=f"ffn_l{layer}h{half}",
    )(s, mods, norm_pre, norm_post, wg, wu, wd)


def _is_rope_block(jj):
    blk = lambda col: col // RH
    return (jj < blk(COL_VA)) | ((jj >= blk(COL_QD)) & (jj < blk(COL_VD)))


def _inproj_kernel(s_ref, m_ref, gpre_ref, w_ref, cos_ref, sina_ref, sinb_ref, p_ref, u_ref,
                   h_ref, stat_ref, vec_ref):
    t = pl.program_id(0)
    j = pl.program_id(1)

    @pl.when(j == 0)
    def _prologue():
        _modulated_norm_to(h_ref, s_ref, stat_ref, vec_ref, m_ref, gpre_ref, TM)

    h = h_ref[...]
    chunks = []
    for c in range(TN_IN // MXU_COLS):
        if c % (RH // MXU_COLS) == 0:
            rope = _is_rope_block(j * (TN_IN // RH) + c // (RH // MXU_COLS)) & (t < LAT_TILES)
            cos = jnp.where(rope, cos_ref[...], 1.0)
            sina = jnp.where(rope, sina_ref[...], 0.0)
            sinb = jnp.where(rope, sinb_ref[...], 0.0)
        y = _dot(h, w_ref[:, c * MXU_COLS:(c + 1) * MXU_COLS])
        chunks.append(y)
        for hd in range(MXU_COLS // HEAD_DIM):
            v = y[:, hd * HEAD_DIM:(hd + 1) * HEAD_DIM]
            fwd = pltpu.roll(v, HEAD_DIM // 4, 1)
            bwd = pltpu.roll(v, HEAD_DIM - HEAD_DIM // 4, 1)
            col = c * MXU_COLS + hd * HEAD_DIM
            p_ref[:, col:col + HEAD_DIM] = (v * cos + bwd * sina + fwd * sinb).astype(BF16)

    for step in range(IN_WIDTH // TN_IN):
        for c in range(TN_IN // MXU_COLS):
            col = step * TN_IN + c * MXU_COLS
            if COL_U <= col < COL_QD:
                @pl.when(j == step)
                def _pool_channels(c=c, col=col):
                    u_ref[:, col - COL_U:col - COL_U + MXU_COLS] = chunks[c]


def _inproj_call(s, mods, norm_pre, w_in, cos, sina, sinb, layer):
    rope_spec = pl.BlockSpec((TM, HEAD_DIM), lambda t, j: (t % (SEQ // TM), 0))
    return pl.pallas_call(
        _inproj_kernel,
        grid=(ALL_TILES, IN_WIDTH // TN_IN),
        in_specs=[
            pl.BlockSpec((TM, D_MODEL), lambda t, j: (t, 0), pipeline_mode=pl.Buffered(1)),
            _mod_spec(layer, 1, TM),
            _norm_spec(layer, 1),
            pl.BlockSpec((None, D_MODEL, TN_IN), lambda t, j: (layer, 0, j)),
            rope_spec, rope_spec, rope_spec,
        ],
        out_specs=[
            pl.BlockSpec((TM, TN_IN), lambda t, j: (t, j)),
            pl.BlockSpec((TM, POOL_WIDTH), lambda t, j: (t, 0)),
        ],
        out_shape=[
            jax.ShapeDtypeStruct((N_ALL, IN_WIDTH), BF16),
            jax.ShapeDtypeStruct((N_ALL, POOL_WIDTH), F32),
        ],
        scratch_shapes=[pltpu.VMEM((TM, D_MODEL), BF16)] + _norm_scratch(TM),
        compiler_params=_params("arbitrary", "arbitrary"),
        name=f"inproj_l{layer}",
    )(s, mods, norm_pre, w_in, cos, sina, sinb)


LAT_QBLOCKS = SEQ // BLOCK
CTX_QBLOCKS = CTX_LEN // BLOCK
BAND = 3 * BLOCK


def _win_kernel(sink_ref, q_ref, k_ref, v_ref, kc_ref, vc_ref, o_ref):
    n = pl.program_id(1)
    to_log2 = HEAD_DIM ** -0.5 * LOG2E
    row = lax.broadcasted_iota(jnp.int32, (WIN_GROUP * BLOCK, 1), 0)

    def head_slice(i):
        return slice(i * HEAD_DIM, (i + 1) * HEAD_DIM)

    def group_q(h):
        return jnp.concatenate([q_ref[:, head_slice(WIN_GROUP * h + g)] for g in range(WIN_GROUP)], axis=0)

    def group_sink(h):
        s0 = sink_ref[0, WIN_GROUP * h]
        s1 = sink_ref[0, WIN_GROUP * h + 1]
        s2 = sink_ref[0, WIN_GROUP * h + 2]
        return jnp.where(row < BLOCK, s0, jnp.where(row < 2 * BLOCK, s1, s2)) * LOG2E

    def store(h, o):
        for g in range(WIN_GROUP):
            o_ref[:, head_slice(WIN_GROUP * h + g)] = o[g * BLOCK:(g + 1) * BLOCK].astype(BF16)

    @pl.when(n < LAT_QBLOCKS)
    def _latent_queries():
        start = pl.multiple_of(jnp.clip((n - 1) * BLOCK, 0, SEQ - BAND), BLOCK)
        qpos = n * BLOCK + (row & (BLOCK - 1))
        kpos = start + lax.broadcasted_iota(jnp.int32, (1, BAND), 1)
        valid = jnp.abs(kpos - qpos) <= WINDOW
        for h in range(WIN_KV_HEADS):
            hs = head_slice(h)
            q = group_q(h)
            s_loc = jnp.where(valid, _dot_nt(q, k_ref[pl.ds(start, BAND), hs]) * to_log2, -jnp.inf)
            s_ctx = _dot_nt(q, kc_ref[:, hs]) * to_log2
            sink = group_sink(h)
            m = jnp.maximum(jnp.maximum(jnp.max(s_loc, axis=-1, keepdims=True),
                                        jnp.max(s_ctx, axis=-1, keepdims=True)), sink)
            e_loc = jnp.exp2(s_loc - m)
            e_ctx = jnp.exp2(s_ctx - m)
            den = (jnp.sum(e_loc, axis=-1, keepdims=True) + jnp.sum(e_ctx, axis=-1, keepdims=True)
                   + jnp.exp2(sink - m))
            o = _dot(e_loc.astype(BF16), v_ref[pl.ds(start, BAND), hs]) + _dot(e_ctx.astype(BF16), vc_ref[:, hs])
            store(h, o * (1.0 / den))

    @pl.when(n >= LAT_QBLOCKS)
    def _context_queries():
        for h in range(WIN_KV_HEADS):
            hs = head_slice(h)
            s_ctx = _dot_nt(group_q(h), kc_ref[:, hs]) * to_log2
            sink = group_sink(h)
            m = jnp.maximum(jnp.max(s_ctx, axis=-1, keepdims=True), sink)
            e_ctx = jnp.exp2(s_ctx - m)
            den = jnp.sum(e_ctx, axis=-1, keepdims=True) + jnp.exp2(sink - m)
            store(h, _dot(e_ctx.astype(BF16), vc_ref[:, hs]) * (1.0 / den))


def _win_call(p, sink, layer, with_ctx_queries):
    nq = LAT_QBLOCKS + (CTX_QBLOCKS if with_ctx_queries else 0)
    kv_cols = WIN_KV_HEADS * HEAD_DIM

    def q_row(b, n):
        return jnp.where(n < LAT_QBLOCKS, b * LAT_QBLOCKS + n,
                         N_LAT // BLOCK + b * CTX_QBLOCKS + n - LAT_QBLOCKS)

    ctx_row = lambda b: N_LAT // CTX_LEN + b
    return pl.pallas_call(
        _win_kernel,
        grid=(BATCH, nq),
        in_specs=[
            pl.BlockSpec(memory_space=pltpu.SMEM),
            pl.BlockSpec((BLOCK, WIN_WIDTH), lambda b, n: (q_row(b, n), 0)),
            pl.BlockSpec((SEQ, kv_cols), lambda b, n: (b, COL_KA // kv_cols)),
            pl.BlockSpec((SEQ, kv_cols), lambda b, n: (b, COL_VA // kv_cols)),
            pl.BlockSpec((CTX_LEN, kv_cols), lambda b, n: (ctx_row(b), COL_KA // kv_cols)),
            pl.BlockSpec((CTX_LEN, kv_cols), lambda b, n: (ctx_row(b), COL_VA // kv_cols)),
        ],
        out_specs=pl.BlockSpec((BLOCK, WIN_WIDTH), lambda b, n: (q_row(b, n), 0)),
        out_shape=jax.ShapeDtypeStruct((N_ALL if with_ctx_queries else N_LAT, WIN_WIDTH), BF16),
        compiler_params=_params("arbitrary", "arbitrary"),
        name=f"win_attn_l{layer}",
    )(sink[layer].reshape(1, WIN_Q_HEADS), p, p, p, p, p)


LAT_QTILES = SEQ // TQ_DIFF


def _diff_kernel(lam_ref, subln_ref, q_ref, k_ref, v_ref, kc_ref, vc_ref, *rest, lam_init):
    o_ref = rest[-2] if len(rest) == 3 else rest[0]
    i = pl.program_id(2)
    to_log2 = HEAD_DIM ** -0.5 * LOG2E
    dl = lam_ref[...]
    lam = (jnp.exp(jnp.sum(dl[0:1] * dl[1:2], axis=-1, keepdims=True))
           - jnp.exp(jnp.sum(dl[2:3] * dl[3:4], axis=-1, keepdims=True)) + lam_init)
    out_gain = subln_ref[...] * (1.0 - lam_init)

    def attend(q, with_latent):
        maps = []
        for j in range(2):
            cs = slice(j * HEAD_DIM, (j + 1) * HEAD_DIM)
            s_ctx = _dot_nt(q[:, cs], kc_ref[:, cs])
            m = jnp.max(s_ctx, axis=-1, keepdims=True)
            e_lat = None
            if with_latent:
                s_lat = _dot_nt(q[:, cs], k_ref[:, cs])
                m = jnp.maximum(m, jnp.max(s_lat, axis=-1, keepdims=True))
                e_lat = jnp.exp2((s_lat - m) * to_log2)
            e_ctx = jnp.exp2((s_ctx - m) * to_log2)
            den = jnp.sum(e_ctx, axis=-1, keepdims=True)
            if with_latent:
                den = den + jnp.sum(e_lat, axis=-1, keepdims=True)
            maps.append((e_lat, e_ctx, 1.0 / den))
        (e1_lat, e1_ctx, inv1), (e2_lat, e2_ctx, inv2) = maps
        w2 = lam * inv2
        o = _dot((e1_ctx * inv1 - e2_ctx * w2).astype(BF16), vc_ref[...])
        if with_latent:
            o = o + _dot((e1_lat * inv1 - e2_lat * w2).astype(BF16), v_ref[...])
        return (_rms(o, out_gain)).astype(BF16)

    for sub in range(TQ_DIFF // TQ_SUB):
        rs = slice(sub * TQ_SUB, (sub + 1) * TQ_SUB)
        o_ref[rs, :] = attend(q_ref[rs, :], True)

    if len(rest) == 3:
        qc_ref, _, oc_ref = rest

        @pl.when(i == LAT_QTILES - 1)
        def _context_queries():
            oc_ref[...] = attend(qc_ref[...], False)


def _diff_call(p, diff_lambda, diff_subln, layer, with_ctx_queries):
    lam_init = 0.8 - 0.6 * math.exp(-0.3 * layer)
    w = DIFF_V_DIM
    ctx_row = lambda b: N_LAT // CTX_LEN + b
    in_specs = [
        pl.BlockSpec((None, 4, HEAD_DIM), lambda b, h, i: (layer, 0, 0)),
        pl.BlockSpec((None, 1, DIFF_V_DIM), lambda b, h, i: (layer, 0, 0)),
        pl.BlockSpec((TQ_DIFF, w), lambda b, h, i: (b * LAT_QTILES + i, COL_QD // w + h)),
        pl.BlockSpec((SEQ, w), lambda b, h, i: (b, COL_KD // w + h)),
        pl.BlockSpec((SEQ, w), lambda b, h, i: (b, COL_VD // w + h)),
        pl.BlockSpec((CTX_LEN, w), lambda b, h, i: (ctx_row(b), COL_KD // w + h)),
        pl.BlockSpec((CTX_LEN, w), lambda b, h, i: (ctx_row(b), COL_VD // w + h)),
    ]
    out_specs = [pl.BlockSpec((TQ_DIFF, w), lambda b, h, i: (b * LAT_QTILES + i, h))]
    out_shape = [jax.ShapeDtypeStruct((N_LAT, DIFF_WIDTH), BF16)]
    operands = [diff_lambda, diff_subln.reshape(DEPTH, 1, DIFF_V_DIM), p, p, p, p, p]
    if with_ctx_queries:
        in_specs.append(pl.BlockSpec((CTX_LEN, w), lambda b, h, i: (ctx_row(b), COL_QD // w + h)))
        out_specs.append(pl.BlockSpec((CTX_LEN, w), lambda b, h, i: (b, h)))
        out_shape.append(jax.ShapeDtypeStruct((N_CTX, DIFF_WIDTH), BF16))
        operands.append(p)
    return pl.pallas_call(
        functools.partial(_diff_kernel, lam_init=lam_init),
        grid=(BATCH, DIFF_HEADS, LAT_QTILES),
        in_specs=in_specs,
        out_specs=out_specs,
        out_shape=out_shape,
        compiler_params=_params("arbitrary", "arbitrary", "arbitrary"),
        name=f"diff_attn_l{layer}",
    )(*operands)


def _pool_kernel(u_ref, w_ref, scale_ref, o_ref, pad_ref, *, n_tok):
    t = lax.broadcasted_iota(jnp.int32, (n_tok, 1), 0)
    zeros = jnp.zeros((POOL_HALO, POOL_GROUP), F32)
    pad_ref[0:POOL_HALO, :] = zeros
    pad_ref[POOL_HALO + n_tok:, :] = zeros
    for g, w in enumerate(POOL_WINDOWS):
        cs = slice(g * POOL_GROUP, (g + 1) * POOL_GROUP)
        u = u_ref[:, cs]
        pad_ref[POOL_HALO:POOL_HALO + n_tok, :] = u
        total = pad_ref[POOL_HALO - w // 2:POOL_HALO - w // 2 + n_tok, :]
        for d in range(1 - w // 2, w // 2):
            total = total + pad_ref[POOL_HALO + d:POOL_HALO + d + n_tok, :]
        count = (jnp.clip(t + w // 2, 0, n_tok) - jnp.clip(t - w // 2, 0, n_tok)).astype(F32)
        centred = (total / count - u).astype(BF16)
        o_ref[:, cs] = (_dot(centred, w_ref[g]) * scale_ref[:, cs]).astype(BF16)


def _pool_call(u, pool_w, pool_scale, layer, n_tok, row_block0):
    return pl.pallas_call(
        functools.partial(_pool_kernel, n_tok=n_tok),
        grid=(BATCH,),
        in_specs=[
            pl.BlockSpec((n_tok, POOL_WIDTH), lambda b: (row_block0 + b, 0)),
            pl.BlockSpec((None, len(POOL_WINDOWS), POOL_GROUP, POOL_GROUP), lambda b: (layer, 0, 0, 0)),
            pl.BlockSpec((None, 1, POOL_WIDTH), lambda b: (layer, 0, 0)),
        ],
        out_specs=pl.BlockSpec((n_tok, POOL_WIDTH), lambda b: (b, 0)),
        out_shape=jax.ShapeDtypeStruct((BATCH * n_tok, POOL_WIDTH), BF16),
        scratch_shapes=[pltpu.VMEM((n_tok + 2 * POOL_HALO, POOL_GROUP), F32)],
        compiler_params=_params("arbitrary"),
        name=f"pool_l{layer}_n{n_tok}",
    )(u, pool_w, pool_scale.reshape(DEPTH, 1, POOL_WIDTH))


def _outproj_kernel(s_ref, m_ref, gpost_ref, oa_ref, ob_ref, oc_ref, w_ref, o_ref, stat_ref, vec_ref):
    oa, ob, oc = oa_ref[...], ob_ref[...], oc_ref[...]
    b0, c0 = WIN_WIDTH, WIN_WIDTH + POOL_WIDTH
    for n in range(D_MODEL // 1024):
        cs = slice(n * 1024, (n + 1) * 1024)
        o_ref[:, cs] = _dot(oa, w_ref[0:b0, cs]) + _dot(ob, w_ref[b0:c0, cs]) + _dot(oc, w_ref[c0:, cs])
    _gated_residual_to(o_ref, s_ref, stat_ref, vec_ref, m_ref[2] * gpost_ref[...], TM_OUT)


def _outproj_call(s, mods, norm_post, o_a, o_b, o_c, w_out, layer, n_rows):
    row_spec = lambda width: pl.BlockSpec((TM_OUT, width), lambda t: (t, 0))
    return pl.pallas_call(
        _outproj_kernel,
        grid=(n_rows // TM_OUT,),
        in_specs=[
            row_spec(D_MODEL),
            _mod_spec(layer, 1, TM_OUT),
            _norm_spec(layer, 1),
            row_spec(WIN_WIDTH), row_spec(POOL_WIDTH), row_spec(DIFF_WIDTH),
            pl.BlockSpec((None, D_MODEL, D_MODEL), lambda t: (layer, 0, 0), pipeline_mode=pl.Buffered(1)),
        ],
        out_specs=row_spec(D_MODEL),
        out_shape=jax.ShapeDtypeStruct((n_rows, D_MODEL), F32),
        scratch_shapes=_norm_scratch(TM_OUT),
        compiler_params=_params("arbitrary"),
        name=f"outproj_l{layer}",
    )(s, mods, norm_post, o_a, o_b, o_c, w_out)


def _rope_tables():
    row = jnp.repeat(jnp.arange(SEQ // GRID_W, dtype=F32), GRID_W)
    col = jnp.tile(jnp.arange(GRID_W, dtype=F32), SEQ // GRID_W)
    half = HEAD_DIM // 2
    inv = ROPE_THETA ** (-jnp.arange(0, half, 2, dtype=F32) / half)
    ar = row[:, None] * inv
    ac = col[:, None] * inv
    ang = jnp.concatenate([ar, ar, ac, ac], axis=-1)
    cos, sin = jnp.cos(ang), jnp.sin(ang)
    low = (jnp.arange(HEAD_DIM) % half) < half // 2
    return cos, jnp.where(low, -sin, 0.0), jnp.where(low, 0.0, sin)


def kernel(x, c, ctx, c_ctx, w_ada, b_ada, norm_pre, norm_post, ffn_w_gate, ffn_w_up, ffn_w_down,
           w_in, w_out, attn_sink, pool_w, pool_scale, diff_lambda, diff_subln):
    s = jnp.concatenate([x.reshape(N_LAT, D_MODEL), ctx.reshape(N_CTX, D_MODEL)], axis=0)
    cc = jnp.zeros((MOD_ROWS, D_MODEL), F32).at[:BATCH].set(c).at[BATCH].set(c_ctx)
    mods = _ada_call(cc, w_ada, b_ada).reshape(DEPTH, MOD_ROWS, 3, 3, 1, D_MODEL)
    npre = norm_pre.reshape(DEPTH, 3, 1, D_MODEL)
    npost = norm_post.reshape(DEPTH, 3, 1, D_MODEL)
    wg, wu, wd = ffn_w_gate.astype(BF16), ffn_w_up.astype(BF16), ffn_w_down.astype(BF16)
    w_in16, w_out16, pool_w16 = w_in.astype(BF16), w_out.astype(BF16), pool_w.astype(BF16)
    cos, sina, sinb = _rope_tables()

    for layer in range(DEPTH):
        update_ctx = layer < DEPTH - 1
        n_tiles = ALL_TILES if update_ctx else LAT_TILES
        s = _ffn_call(s, mods, npre, npost, wg, wu, wd, layer, 0, ALL_TILES)
        p, u = _inproj_call(s, mods, npre, w_in16, cos, sina, sinb, layer)
        o_a = _win_call(p, attn_sink, layer, update_ctx)
        o_c = _diff_call(p, diff_lambda, diff_subln, layer, update_ctx)
        o_b = [_pool_call(u, pool_w16, pool_scale, layer, SEQ, 0)]
        if update_ctx:
            o_b.append(_pool_call(u, pool_w16, pool_scale, layer, CTX_LEN, N_LAT // CTX_LEN))
        o_b, o_c = jnp.concatenate(o_b, axis=0), jnp.concatenate(o_c, axis=0)
        s = _outproj_call(s, mods, npost, o_a, o_b, o_c, w_out16, layer, n_tiles * TM)
        s = _ffn_call(s, mods, npre, npost, wg, wu, wd, layer, 1, n_tiles)
    return s.reshape(BATCH, SEQ, D_MODEL)
```

```python
import functools
import math

import jax
import jax.numpy as jnp
from jax import lax
from jax.experimental import pallas as pl
from jax.experimental.pallas import tpu as pltpu

D_MODEL = 4096
BATCH = 4
SEQ = 2048
DEPTH = 2
GRID_W = 64
CTX_LEN = 256
HEAD_DIM = 128
ROPE_THETA = 10000.0
EPS = 1e-6
N_MOD = 9
FFN_DIM = 2 * D_MODEL

WIN_Q_HEADS = 12
WIN_KV_HEADS = 4
WIN_GROUP = WIN_Q_HEADS // WIN_KV_HEADS
WIN_WIDTH = WIN_Q_HEADS * HEAD_DIM
WINDOW = 128
BLOCK = 128

POOL_WINDOWS = (2, 4, 8, 16)
POOL_WIDTH = 1024
POOL_GROUP = POOL_WIDTH // len(POOL_WINDOWS)
POOL_HALO = 8

DIFF_HEADS = 6
DIFF_V_DIM = 2 * HEAD_DIM
DIFF_WIDTH = DIFF_HEADS * DIFF_V_DIM
IN_WIDTH = 8192

COL_QA = 0
COL_KA = COL_QA + WIN_WIDTH
COL_VA = COL_KA + WIN_KV_HEADS * HEAD_DIM
COL_U = COL_VA + WIN_KV_HEADS * HEAD_DIM
COL_QD = COL_U + POOL_WIDTH
COL_KD = COL_QD + 2 * DIFF_HEADS * HEAD_DIM
COL_VD = COL_KD + 2 * DIFF_HEADS * HEAD_DIM

F32 = jnp.float32
BF16 = jnp.bfloat16
LANES = 128
SUBLANES = 8
LOG2E = math.log2(math.e)

N_LAT = BATCH * SEQ
N_CTX = BATCH * CTX_LEN
N_ALL = N_LAT + N_CTX

TM = 512
TM_OUT = 256
TF = 512
TN_IN = 1024
RH = 512
MXU_COLS = 256
TN_ADA = 512
TQ_DIFF = 512
TQ_SUB = 256
ROW_CHUNK = 16
MOD_ROWS = 16
VMEM_LIMIT = 63 * 1024 * 1024

LAT_TILES = N_LAT // TM
ALL_TILES = N_ALL // TM


def _mod_row(t, tm):
    return jnp.where(t < N_LAT // tm, t // (SEQ // tm), BATCH)


def _dot(a, b):
    return jnp.dot(a, b, preferred_element_type=F32)


def _dot_nt(a, b):
    return lax.dot_general(a, b, (((1,), (1,)), ((), ())), preferred_element_type=F32)


def _for_rows(n_rows, chunk, fn, unroll=1):
    def body(i, carry):
        fn(pl.ds(pl.multiple_of(i * chunk, chunk), chunk))
        return carry
    lax.fori_loop(0, n_rows // chunk, body, 0, unroll=unroll)


def _rstd_pass(load_rows, stat_ref, n_rows):
    def rows(r):
        v = load_rows(r)
        sq = v * v
        parts = [sq[:, k * LANES:(k + 1) * LANES] for k in range(D_MODEL // LANES)]
        while len(parts) > 1:
            parts = [parts[k] + parts[k + 1] for k in range(0, len(parts), 2)]
        total = jnp.sum(parts[0], axis=-1, keepdims=True)
        rstd = lax.rsqrt(total * (1.0 / D_MODEL) + EPS)
        stat_ref[r, :] = jnp.broadcast_to(rstd, (SUBLANES, LANES))
    _for_rows(n_rows, SUBLANES, rows, unroll=16)


def _rms(v, g):
    return v * lax.rsqrt(jnp.mean(v * v, axis=-1, keepdims=True) + EPS) * g


def _params(*sem):
    return pltpu.CompilerParams(dimension_semantics=sem, vmem_limit_bytes=VMEM_LIMIT)


def _ada_kernel(c_ref, w_ref, b_ref, o_ref):
    c = c_ref[...]
    s = (c / (1.0 + jnp.exp(-c))).astype(BF16)
    o_ref[...] = _dot(s, w_ref[...].astype(BF16)) + b_ref[...]


def _ada_call(cc, w_ada, b_ada):
    n_out = N_MOD * D_MODEL
    return pl.pallas_call(
        _ada_kernel,
        grid=(DEPTH, n_out // TN_ADA),
        in_specs=[
            pl.BlockSpec((MOD_ROWS, D_MODEL), lambda i, j: (0, 0)),
            pl.BlockSpec((None, D_MODEL, TN_ADA), lambda i, j: (i, 0, j)),
            pl.BlockSpec((None, 1, TN_ADA), lambda i, j: (i, 0, j)),
        ],
        out_specs=pl.BlockSpec((None, MOD_ROWS, TN_ADA), lambda i, j: (i, 0, j)),
        out_shape=jax.ShapeDtypeStruct((DEPTH, MOD_ROWS, n_out), F32),
        compiler_params=_params("arbitrary", "arbitrary"),
        name="adaln",
    )(cc, w_ada, b_ada.reshape(DEPTH, 1, n_out))


def _mod_spec(layer, group, tm):
    return pl.BlockSpec((None, None, None, 3, 1, D_MODEL),
                        lambda t, *_: (layer, _mod_row(t, tm), group, 0, 0, 0))


def _norm_spec(layer, k):
    return pl.BlockSpec((None, None, 1, D_MODEL), lambda t, *_: (layer, k, 0, 0))


def _modulated_norm_to(h_ref, s_ref, stat_ref, vec_ref, m_ref, gpre_ref, n_rows, also=None):
    _rstd_pass(lambda r: s_ref[r, :], stat_ref, n_rows)
    full = (ROW_CHUNK, D_MODEL)
    vec_ref[0:ROW_CHUNK, :] = jnp.broadcast_to(gpre_ref[...] * (1.0 + m_ref[1]), full)
    vec_ref[ROW_CHUNK:, :] = jnp.broadcast_to(m_ref[0], full)

    def rows(r):
        rstd = stat_ref[r, :]
        for k in range(D_MODEL // LANES):
            cs = slice(k * LANES, (k + 1) * LANES)
            h = s_ref[r, cs] * rstd * vec_ref[0:ROW_CHUNK, cs] + vec_ref[ROW_CHUNK:, cs]
            h_ref[r, cs] = h.astype(BF16)
        if also is not None:
            also(r)
    _for_rows(n_rows, ROW_CHUNK, rows, unroll=2)


def _gated_residual_to(o_ref, s_ref, stat_ref, vec_ref, gate_gain, n_rows):
    _rstd_pass(lambda r: o_ref[r, :], stat_ref, n_rows)
    vec_ref[0:ROW_CHUNK, :] = jnp.broadcast_to(gate_gain, (ROW_CHUNK, D_MODEL))

    def rows(r):
        rstd = stat_ref[r, :]
        for k in range(D_MODEL // LANES):
            cs = slice(k * LANES, (k + 1) * LANES)
            o_ref[r, cs] = s_ref[r, cs] + o_ref[r, cs] * rstd * vec_ref[0:ROW_CHUNK, cs]
    _for_rows(n_rows, ROW_CHUNK, rows, unroll=2)


def _norm_scratch(n_rows):
    return [pltpu.VMEM((n_rows, LANES), F32), pltpu.VMEM((2 * ROW_CHUNK, D_MODEL), F32)]


def _hosted_cast(arr, lead, n_inner):
    rows, cols = arr.shape[-2:]
    block = (rows // LAT_TILES, cols // n_inner)

    def index(t, i):
        return jnp.minimum(t, LAT_TILES - 1), jnp.where(t < LAT_TILES, i, n_inner - 1)

    in_spec = pl.BlockSpec((None,) * len(lead) + block, lambda t, i: lead + index(t, i))
    return in_spec, pl.BlockSpec(block, index), jax.ShapeDtypeStruct((rows, cols), BF16)


def _ffn_kernel(s_ref, m_ref, gpre_ref, gpost_ref, wg_ref, wu_ref, wd_ref, *rest):
    n_cast = (len(rest) - 4) // 2
    cast_src, o_ref, cast_dst = rest[:n_cast], rest[n_cast], rest[n_cast + 1:2 * n_cast + 1]
    h_ref, stat_ref, vec_ref = rest[2 * n_cast + 1:]
    f = pl.program_id(1)

    @pl.when(f == 0)
    def _prologue():
        def zero_acc(r):
            o_ref[r, :] = jnp.zeros((ROW_CHUNK, D_MODEL), F32)
        _modulated_norm_to(h_ref, s_ref, stat_ref, vec_ref, m_ref, gpre_ref, TM, also=zero_acc)

    for src, dst in zip(cast_src, cast_dst):
        dst[...] = src[...].astype(BF16)
    h = h_ref[...]
    g = _dot(h, wg_ref[...])
    u = _dot(h, wu_ref[...])
    a = (g / (1.0 + jnp.exp(-g)) * u).astype(BF16)
    for c in range(D_MODEL // 1024):
        cs = slice(c * 1024, (c + 1) * 1024)
        o_ref[:, cs] += _dot(a, wd_ref[:, cs])

    @pl.when(f == pl.num_programs(1) - 1)
    def _epilogue():
        _gated_residual_to(o_ref, s_ref, stat_ref, vec_ref, 0.5 * m_ref[2] * gpost_ref[...], TM)


def _ffn_call(s, mods, norm_pre, norm_post, weights, layer, half, n_tiles, cast_next=()):
    k = 2 * half
    n_f = FFN_DIM // TF
    hosted = [_hosted_cast(arr, lead, n_f) for arr, lead in cast_next]
    out = pl.pallas_call(
        _ffn_kernel,
        grid=(n_tiles, n_f),
        in_specs=[
            pl.BlockSpec((TM, D_MODEL), lambda t, f: (t, 0), pipeline_mode=pl.Buffered(1)),
            _mod_spec(layer, k, TM),
            _norm_spec(layer, k),
            _norm_spec(layer, k),
            pl.BlockSpec((D_MODEL, TF), lambda t, f: (0, f)),
            pl.BlockSpec((D_MODEL, TF), lambda t, f: (0, f)),
            pl.BlockSpec((TF, D_MODEL), lambda t, f: (f, 0)),
        ] + [h[0] for h in hosted],
        out_specs=[pl.BlockSpec((TM, D_MODEL), lambda t, f: (t, 0))] + [h[1] for h in hosted],
        out_shape=[jax.ShapeDtypeStruct((n_tiles * TM, D_MODEL), F32)] + [h[2] for h in hosted],
        scratch_shapes=[pltpu.VMEM((TM, D_MODEL), BF16)] + _norm_scratch(TM),
        compiler_params=_params("arbitrary", "arbitrary"),
        name=f"ffn_l{layer}h{half}",
    )(s, mods, norm_pre, norm_post, *weights, *[arr for arr, _ in cast_next])
    return out[0], tuple(out[1:])


def _is_rope_block(jj):
    blk = lambda col: col // RH
    return (jj < blk(COL_VA)) | ((jj >= blk(COL_QD)) & (jj < blk(COL_VD)))


def _inproj_kernel(s_ref, m_ref, gpre_ref, w_ref, cos_ref, sina_ref, sinb_ref, *rest):
    n_cast = (len(rest) - 5) // 2
    cast_src, (p_ref, u_ref) = rest[:n_cast], rest[n_cast:n_cast + 2]
    cast_dst = rest[n_cast + 2:2 * n_cast + 2]
    h_ref, stat_ref, vec_ref = rest[2 * n_cast + 2:]
    t = pl.program_id(0)
    j = pl.program_id(1)

    @pl.when(j == 0)
    def _prologue():
        _modulated_norm_to(h_ref, s_ref, stat_ref, vec_ref, m_ref, gpre_ref, TM)

    for src, dst in zip(cast_src, cast_dst):
        dst[...] = src[...].astype(BF16)

    h = h_ref[...]
    chunks = []
    for c in range(TN_IN // MXU_COLS):
        if c % (RH // MXU_COLS) == 0:
            rope = _is_rope_block(j * (TN_IN // RH) + c // (RH // MXU_COLS)) & (t < LAT_TILES)
            cos = jnp.where(rope, cos_ref[...], 1.0)
            sina = jnp.where(rope, sina_ref[...], 0.0)
            sinb = jnp.where(rope, sinb_ref[...], 0.0)
        y = _dot(h, w_ref[:, c * MXU_COLS:(c + 1) * MXU_COLS])
        chunks.append(y)
        for hd in range(MXU_COLS // HEAD_DIM):
            v = y[:, hd * HEAD_DIM:(hd + 1) * HEAD_DIM]
            fwd = pltpu.roll(v, HEAD_DIM // 4, 1)
            bwd = pltpu.roll(v, HEAD_DIM - HEAD_DIM // 4, 1)
            col = c * MXU_COLS + hd * HEAD_DIM
            p_ref[:, col:col + HEAD_DIM] = (v * cos + bwd * sina + fwd * sinb).astype(BF16)

    for step in range(IN_WIDTH // TN_IN):
        for c in range(TN_IN // MXU_COLS):
            col = step * TN_IN + c * MXU_COLS
            if COL_U <= col < COL_QD:
                @pl.when(j == step)
                def _pool_channels(c=c, col=col):
                    u_ref[:, col - COL_U:col - COL_U + MXU_COLS] = chunks[c]


def _inproj_call(s, mods, norm_pre, w_in, cos, sina, sinb, layer, cast_next=()):
    n_j = IN_WIDTH // TN_IN
    hosted = [_hosted_cast(arr, lead, n_j) for arr, lead in cast_next]
    rope_spec = pl.BlockSpec((TM, HEAD_DIM), lambda t, j: (t % (SEQ // TM), 0))
    out = pl.pallas_call(
        _inproj_kernel,
        grid=(ALL_TILES, n_j),
        in_specs=[
            pl.BlockSpec((TM, D_MODEL), lambda t, j: (t, 0)),
            _mod_spec(layer, 1, TM),
            _norm_spec(layer, 1),
            pl.BlockSpec((D_MODEL, TN_IN), lambda t, j: (0, j)),
            rope_spec, rope_spec, rope_spec,
        ] + [h[0] for h in hosted],
        out_specs=[
            pl.BlockSpec((TM, TN_IN), lambda t, j: (t, j)),
            pl.BlockSpec((TM, POOL_WIDTH), lambda t, j: (t, 0)),
        ] + [h[1] for h in hosted],
        out_shape=[
            jax.ShapeDtypeStruct((N_ALL, IN_WIDTH), BF16),
            jax.ShapeDtypeStruct((N_ALL, POOL_WIDTH), F32),
        ] + [h[2] for h in hosted],
        scratch_shapes=[pltpu.VMEM((TM, D_MODEL), BF16)] + _norm_scratch(TM),
        compiler_params=_params("arbitrary", "arbitrary"),
        name=f"inproj_l{layer}",
    )(s, mods, norm_pre, w_in, cos, sina, sinb, *[arr for arr, _ in cast_next])
    return out[0], out[1], tuple(out[2:])


LAT_QBLOCKS = SEQ // BLOCK
CTX_QBLOCKS = CTX_LEN // BLOCK
BAND = 3 * BLOCK


def _win_kernel(sink_ref, q_ref, k_ref, v_ref, kc_ref, vc_ref, o_ref):
    n = pl.program_id(1)
    to_log2 = HEAD_DIM ** -0.5 * LOG2E
    row = lax.broadcasted_iota(jnp.int32, (WIN_GROUP * BLOCK, 1), 0)

    def head_slice(i):
        return slice(i * HEAD_DIM, (i + 1) * HEAD_DIM)

    def group_q(h):
        return jnp.concatenate([q_ref[:, head_slice(WIN_GROUP * h + g)] for g in range(WIN_GROUP)], axis=0)

    def group_sink(h):
        s0 = sink_ref[0, WIN_GROUP * h]
        s1 = sink_ref[0, WIN_GROUP * h + 1]
        s2 = sink_ref[0, WIN_GROUP * h + 2]
        return jnp.where(row < BLOCK, s0, jnp.where(row < 2 * BLOCK, s1, s2)) * LOG2E

    def store(h, o):
        for g in range(WIN_GROUP):
            o_ref[:, head_slice(WIN_GROUP * h + g)] = o[g * BLOCK:(g + 1) * BLOCK].astype(BF16)

    @pl.when(n < LAT_QBLOCKS)
    def _latent_queries():
        start = pl.multiple_of(jnp.clip((n - 1) * BLOCK, 0, SEQ - BAND), BLOCK)
        qpos = n * BLOCK + (row & (BLOCK - 1))
        kpos = start + lax.broadcasted_iota(jnp.int32, (1, BAND), 1)
        valid = jnp.abs(kpos - qpos) <= WINDOW
        for h in range(WIN_KV_HEADS):
            hs = head_slice(h)
            q = group_q(h)
            s_loc = jnp.where(valid, _dot_nt(q, k_ref[pl.ds(start, BAND), hs]) * to_log2, -jnp.inf)
            s_ctx = _dot_nt(q, kc_ref[:, hs]) * to_log2
            sink = group_sink(h)
            m = jnp.maximum(jnp.maximum(jnp.max(s_loc, axis=-1, keepdims=True),
                                        jnp.max(s_ctx, axis=-1, keepdims=True)), sink)
            e_loc = jnp.exp2(s_loc - m)
            e_ctx = jnp.exp2(s_ctx - m)
            den = (jnp.sum(e_loc, axis=-1, keepdims=True) + jnp.sum(e_ctx, axis=-1, keepdims=True)
                   + jnp.exp2(sink - m))
            o = _dot(e_loc.astype(BF16), v_ref[pl.ds(start, BAND), hs]) + _dot(e_ctx.astype(BF16), vc_ref[:, hs])
            store(h, o * (1.0 / den))

    @pl.when(n >= LAT_QBLOCKS)
    def _context_queries():
        for h in range(WIN_KV_HEADS):
            hs = head_slice(h)
            s_ctx = _dot_nt(group_q(h), kc_ref[:, hs]) * to_log2
            sink = group_sink(h)
            m = jnp.maximum(jnp.max(s_ctx, axis=-1, keepdims=True), sink)
            e_ctx = jnp.exp2(s_ctx - m)
            den = jnp.sum(e_ctx, axis=-1, keepdims=True) + jnp.exp2(sink - m)
            store(h, _dot(e_ctx.astype(BF16), vc_ref[:, hs]) * (1.0 / den))


def _win_call(p, sink, layer, with_ctx_queries):
    nq = LAT_QBLOCKS + (CTX_QBLOCKS if with_ctx_queries else 0)
    kv_cols = WIN_KV_HEADS * HEAD_DIM

    def q_row(b, n):
        return jnp.where(n < LAT_QBLOCKS, b * LAT_QBLOCKS + n,
                         N_LAT // BLOCK + b * CTX_QBLOCKS + n - LAT_QBLOCKS)

    ctx_row = lambda b: N_LAT // CTX_LEN + b
    return pl.pallas_call(
        _win_kernel,
        grid=(BATCH, nq),
        in_specs=[
            pl.BlockSpec(memory_space=pltpu.SMEM),
            pl.BlockSpec((BLOCK, WIN_WIDTH), lambda b, n: (q_row(b, n), 0)),
            pl.BlockSpec((SEQ, kv_cols), lambda b, n: (b, COL_KA // kv_cols)),
            pl.BlockSpec((SEQ, kv_cols), lambda b, n: (b, COL_VA // kv_cols)),
            pl.BlockSpec((CTX_LEN, kv_cols), lambda b, n: (ctx_row(b), COL_KA // kv_cols)),
            pl.BlockSpec((CTX_LEN, kv_cols), lambda b, n: (ctx_row(b), COL_VA // kv_cols)),
        ],
        out_specs=pl.BlockSpec((BLOCK, WIN_WIDTH), lambda b, n: (q_row(b, n), 0)),
        out_shape=jax.ShapeDtypeStruct((N_ALL if with_ctx_queries else N_LAT, WIN_WIDTH), BF16),
        compiler_params=_params("arbitrary", "arbitrary"),
        name=f"win_attn_l{layer}",
    )(sink[layer].reshape(1, WIN_Q_HEADS), p, p, p, p, p)


LAT_QTILES = SEQ // TQ_DIFF


def _diff_kernel(lam_ref, subln_ref, q_ref, k_ref, v_ref, kc_ref, vc_ref, *rest, lam_init):
    o_ref = rest[-2] if len(rest) == 3 else rest[0]
    i = pl.program_id(2)
    to_log2 = HEAD_DIM ** -0.5 * LOG2E
    dl = lam_ref[...]
    lam = (jnp.exp(jnp.sum(dl[0:1] * dl[1:2], axis=-1, keepdims=True))
           - jnp.exp(jnp.sum(dl[2:3] * dl[3:4], axis=-1, keepdims=True)) + lam_init)
    out_gain = subln_ref[...] * (1.0 - lam_init)

    def attend(q, with_latent):
        maps = []
        for j in range(2):
            cs = slice(j * HEAD_DIM, (j + 1) * HEAD_DIM)
            s_ctx = _dot_nt(q[:, cs], kc_ref[:, cs])
            m = jnp.max(s_ctx, axis=-1, keepdims=True)
            e_lat = None
            if with_latent:
                s_lat = _dot_nt(q[:, cs], k_ref[:, cs])
                m = jnp.maximum(m, jnp.max(s_lat, axis=-1, keepdims=True))
                e_lat = jnp.exp2((s_lat - m) * to_log2)
            e_ctx = jnp.exp2((s_ctx - m) * to_log2)
            den = jnp.sum(e_ctx, axis=-1, keepdims=True)
            if with_latent:
                den = den + jnp.sum(e_lat, axis=-1, keepdims=True)
            maps.append((e_lat, e_ctx, 1.0 / den))
        (e1_lat, e1_ctx, inv1), (e2_lat, e2_ctx, inv2) = maps
        w2 = lam * inv2
        o = _dot((e1_ctx * inv1 - e2_ctx * w2).astype(BF16), vc_ref[...])
        if with_latent:
            o = o + _dot((e1_lat * inv1 - e2_lat * w2).astype(BF16), v_ref[...])
        return (_rms(o, out_gain)).astype(BF16)

    for sub in range(TQ_DIFF // TQ_SUB):
        rs = slice(sub * TQ_SUB, (sub + 1) * TQ_SUB)
        o_ref[rs, :] = attend(q_ref[rs, :], True)

    if len(rest) == 3:
        qc_ref, _, oc_ref = rest

        @pl.when(i == LAT_QTILES - 1)
        def _context_queries():
            oc_ref[...] = attend(qc_ref[...], False)


def _diff_call(p, diff_lambda, diff_subln, layer, with_ctx_queries):
    lam_init = 0.8 - 0.6 * math.exp(-0.3 * layer)
    w = DIFF_V_DIM
    ctx_row = lambda b: N_LAT // CTX_LEN + b
    in_specs = [
        pl.BlockSpec((None, 4, HEAD_DIM), lambda b, h, i: (layer, 0, 0)),
        pl.BlockSpec((None, 1, DIFF_V_DIM), lambda b, h, i: (layer, 0, 0)),
        pl.BlockSpec((TQ_DIFF, w), lambda b, h, i: (b * LAT_QTILES + i, COL_QD // w + h)),
        pl.BlockSpec((SEQ, w), lambda b, h, i: (b, COL_KD // w + h)),
        pl.BlockSpec((SEQ, w), lambda b, h, i: (b, COL_VD // w + h)),
        pl.BlockSpec((CTX_LEN, w), lambda b, h, i: (ctx_row(b), COL_KD // w + h)),
        pl.BlockSpec((CTX_LEN, w), lambda b, h, i: (ctx_row(b), COL_VD // w + h)),
    ]
    out_specs = [pl.BlockSpec((TQ_DIFF, w), lambda b, h, i: (b * LAT_QTILES + i, h))]
    out_shape = [jax.ShapeDtypeStruct((N_LAT, DIFF_WIDTH), BF16)]
    operands = [diff_lambda, diff_subln.reshape(DEPTH, 1, DIFF_V_DIM), p, p, p, p, p]
    if with_ctx_queries:
        in_specs.append(pl.BlockSpec((CTX_LEN, w), lambda b, h, i: (ctx_row(b), COL_QD // w + h)))
        out_specs.append(pl.BlockSpec((CTX_LEN, w), lambda b, h, i: (b, h)))
        out_shape.append(jax.ShapeDtypeStruct((N_CTX, DIFF_WIDTH), BF16))
        operands.append(p)
    return pl.pallas_call(
        functools.partial(_diff_kernel, lam_init=lam_init),
        grid=(BATCH, DIFF_HEADS, LAT_QTILES),
        in_specs=in_specs,
        out_specs=out_specs,
        out_shape=out_shape,
        compiler_params=_params("arbitrary", "arbitrary", "arbitrary"),
        name=f"diff_attn_l{layer}",
    )(*operands)


def _pool_kernel(u_ref, w_ref, scale_ref, o_ref, pad_ref, *, n_tok):
    t = lax.broadcasted_iota(jnp.int32, (n_tok, 1), 0)
    zeros = jnp.zeros((POOL_HALO, POOL_GROUP), F32)
    pad_ref[0:POOL_HALO, :] = zeros
    pad_ref[POOL_HALO + n_tok:, :] = zeros
    for g, w in enumerate(POOL_WINDOWS):
        cs = slice(g * POOL_GROUP, (g + 1) * POOL_GROUP)
        u = u_ref[:, cs]
        pad_ref[POOL_HALO:POOL_HALO + n_tok, :] = u
        total = pad_ref[POOL_HALO - w // 2:POOL_HALO - w // 2 + n_tok, :]
        for d in range(1 - w // 2, w // 2):
            total = total + pad_ref[POOL_HALO + d:POOL_HALO + d + n_tok, :]
        count = (jnp.clip(t + w // 2, 0, n_tok) - jnp.clip(t - w // 2, 0, n_tok)).astype(F32)
        centred = (total / count - u).astype(BF16)
        o_ref[:, cs] = (_dot(centred, w_ref[g]) * scale_ref[:, cs]).astype(BF16)


def _pool_call(u, pool_w, pool_scale, layer, n_tok, row_block0):
    return pl.pallas_call(
        functools.partial(_pool_kernel, n_tok=n_tok),
        grid=(BATCH,),
        in_specs=[
            pl.BlockSpec((n_tok, POOL_WIDTH), lambda b: (row_block0 + b, 0)),
            pl.BlockSpec((None, len(POOL_WINDOWS), POOL_GROUP, POOL_GROUP), lambda b: (layer, 0, 0, 0)),
            pl.BlockSpec((None, 1, POOL_WIDTH), lambda b: (layer, 0, 0)),
        ],
        out_specs=pl.BlockSpec((n_tok, POOL_WIDTH), lambda b: (b, 0)),
        out_shape=jax.ShapeDtypeStruct((BATCH * n_tok, POOL_WIDTH), BF16),
        scratch_shapes=[pltpu.VMEM((n_tok + 2 * POOL_HALO, POOL_GROUP), F32)],
        compiler_params=_params("arbitrary"),
        name=f"pool_l{layer}_n{n_tok}",
    )(u, pool_w, pool_scale.reshape(DEPTH, 1, POOL_WIDTH))


def _outproj_kernel(s_ref, m_ref, gpost_ref, oa_ref, ob_ref, oc_ref, w_ref, o_ref, stat_ref, vec_ref):
    oa, ob, oc = oa_ref[...], ob_ref[...], oc_ref[...]
    b0, c0 = WIN_WIDTH, WIN_WIDTH + POOL_WIDTH
    for n in range(D_MODEL // 1024):
        cs = slice(n * 1024, (n + 1) * 1024)
        o_ref[:, cs] = _dot(oa, w_ref[0:b0, cs]) + _dot(ob, w_ref[b0:c0, cs]) + _dot(oc, w_ref[c0:, cs])
    _gated_residual_to(o_ref, s_ref, stat_ref, vec_ref, m_ref[2] * gpost_ref[...], TM_OUT)


def _outproj_call(s, mods, norm_post, o_a, o_b, o_c, w_out, layer, n_rows):
    row_spec = lambda width: pl.BlockSpec((TM_OUT, width), lambda t: (t, 0))
    return pl.pallas_call(
        _outproj_kernel,
        grid=(n_rows // TM_OUT,),
        in_specs=[
            row_spec(D_MODEL),
            _mod_spec(layer, 1, TM_OUT),
            _norm_spec(layer, 1),
            row_spec(WIN_WIDTH), row_spec(POOL_WIDTH), row_spec(DIFF_WIDTH),
            pl.BlockSpec((D_MODEL, D_MODEL), lambda t: (0, 0), pipeline_mode=pl.Buffered(1)),
        ],
        out_specs=row_spec(D_MODEL),
        out_shape=jax.ShapeDtypeStruct((n_rows, D_MODEL), F32),
        scratch_shapes=_norm_scratch(TM_OUT),
        compiler_params=_params("arbitrary"),
        name=f"outproj_l{layer}",
    )(s, mods, norm_post, o_a, o_b, o_c, w_out)


def _rope_tables():
    row = jnp.repeat(jnp.arange(SEQ // GRID_W, dtype=F32), GRID_W)
    col = jnp.tile(jnp.arange(GRID_W, dtype=F32), SEQ // GRID_W)
    half = HEAD_DIM // 2
    inv = ROPE_THETA ** (-jnp.arange(0, half, 2, dtype=F32) / half)
    ar = row[:, None] * inv
    ac = col[:, None] * inv
    ang = jnp.concatenate([ar, ar, ac, ac], axis=-1)
    cos, sin = jnp.cos(ang), jnp.sin(ang)
    low = (jnp.arange(HEAD_DIM) % half) < half // 2
    return cos, jnp.where(low, -sin, 0.0), jnp.where(low, 0.0, sin)


def kernel(x, c, ctx, c_ctx, w_ada, b_ada, norm_pre, norm_post, ffn_w_gate, ffn_w_up, ffn_w_down,
           w_in, w_out, attn_sink, pool_w, pool_scale, diff_lambda, diff_subln):
    s = jnp.concatenate([x.reshape(N_LAT, D_MODEL), ctx.reshape(N_CTX, D_MODEL)], axis=0)
    cc = jnp.zeros((MOD_ROWS, D_MODEL), F32).at[:BATCH].set(c).at[BATCH].set(c_ctx)
    mods = _ada_call(cc, w_ada, b_ada).reshape(DEPTH, MOD_ROWS, 3, 3, 1, D_MODEL)
    npre = norm_pre.reshape(DEPTH, 3, 1, D_MODEL)
    npost = norm_post.reshape(DEPTH, 3, 1, D_MODEL)
    pool_w16 = pool_w.astype(BF16)
    cos, sina, sinb = _rope_tables()

    ffn_f32 = (ffn_w_gate, ffn_w_up, ffn_w_down)
    ffn_order = [(layer, half) for layer in range(DEPTH) for half in range(2)]
    ffn16 = tuple(w[0, 0].astype(BF16) for w in ffn_f32)
    w_in16, w_out16 = w_in[0].astype(BF16), w_out[0].astype(BF16)

    def next_ffn(layer, half):
        i = ffn_order.index((layer, half)) + 1
        return [(w, ffn_order[i]) for w in ffn_f32] if i < len(ffn_order) else []

    for layer in range(DEPTH):
        update_ctx = layer < DEPTH - 1
        n_tiles = ALL_TILES if update_ctx else LAT_TILES
        s, ffn16 = _ffn_call(s, mods, npre, npost, ffn16, layer, 0, ALL_TILES, next_ffn(layer, 0))
        proj_next = [(w_in, (layer + 1,)), (w_out, (layer + 1,))] if update_ctx else []
        p, u, proj16 = _inproj_call(s, mods, npre, w_in16, cos, sina, sinb, layer, proj_next)
        o_a = _win_call(p, attn_sink, layer, update_ctx)
        o_c = _diff_call(p, diff_lambda, diff_subln, layer, update_ctx)
        o_b = [_pool_call(u, pool_w16, pool_scale, layer, SEQ, 0)]
        if update_ctx:
            o_b.append(_pool_call(u, pool_w16, pool_scale, layer, CTX_LEN, N_LAT // CTX_LEN))
        o_b, o_c = jnp.concatenate(o_b, axis=0), jnp.concatenate(o_c, axis=0)
        s = _outproj_call(s, mods, npost, o_a, o_b, o_c, w_out16, layer, n_tiles * TM)
        s, ffn16 = _ffn_call(s, mods, npre, npost, ffn16, layer, 1, n_tiles, next_ffn(layer, 1))
        if update_ctx:
            w_in16, w_out16 = proj16
    return s.reshape(BATCH, SEQ, D_MODEL)
```

```python
import functools
import math

import jax
import jax.numpy as jnp
from jax import lax
from jax.experimental import pallas as pl
from jax.experimental.pallas import tpu as pltpu

D_MODEL = 4096
BATCH = 4
SEQ = 2048
DEPTH = 2
GRID_W = 64
CTX_LEN = 256
HEAD_DIM = 128
ROPE_THETA = 10000.0
EPS = 1e-6
N_MOD = 9
FFN_DIM = 2 * D_MODEL

WIN_Q_HEADS = 12
WIN_KV_HEADS = 4
WIN_GROUP = WIN_Q_HEADS // WIN_KV_HEADS
WIN_WIDTH = WIN_Q_HEADS * HEAD_DIM
WINDOW = 128
BLOCK = 128

POOL_WINDOWS = (2, 4, 8, 16)
POOL_WIDTH = 1024
POOL_GROUP = POOL_WIDTH // len(POOL_WINDOWS)
POOL_HALO = 8

DIFF_HEADS = 6
DIFF_V_DIM = 2 * HEAD_DIM
DIFF_WIDTH = DIFF_HEADS * DIFF_V_DIM
IN_WIDTH = 8192

COL_QA = 0
COL_KA = COL_QA + WIN_WIDTH
COL_VA = COL_KA + WIN_KV_HEADS * HEAD_DIM
COL_U = COL_VA + WIN_KV_HEADS * HEAD_DIM
COL_QD = COL_U + POOL_WIDTH
COL_KD = COL_QD + 2 * DIFF_HEADS * HEAD_DIM
COL_VD = COL_KD + 2 * DIFF_HEADS * HEAD_DIM

F32 = jnp.float32
BF16 = jnp.bfloat16
LANES = 128
SUBLANES = 8
LOG2E = math.log2(math.e)

N_LAT = BATCH * SEQ
N_CTX = BATCH * CTX_LEN
N_ALL = N_LAT + N_CTX

TM = 512
TM_OUT = 256
TF = 512
TN_IN = 1024
RH = 512
MXU_COLS = 256
TN_ADA = 512
TQ_DIFF = 512
TQ_SUB = 256
ROW_CHUNK = 16
MOD_ROWS = 16
VMEM_LIMIT = 63 * 1024 * 1024

LAT_TILES = N_LAT // TM
ALL_TILES = N_ALL // TM


def _mod_row(t, tm):
    return jnp.where(t < N_LAT // tm, t // (SEQ // tm), BATCH)


def _dot(a, b):
    return jnp.dot(a, b, preferred_element_type=F32)


def _dot_nt(a, b):
    return lax.dot_general(a, b, (((1,), (1,)), ((), ())), preferred_element_type=F32)


def _for_rows(n_rows, chunk, fn, unroll=1):
    def body(i, carry):
        fn(pl.ds(pl.multiple_of(i * chunk, chunk), chunk))
        return carry
    lax.fori_loop(0, n_rows // chunk, body, 0, unroll=unroll)


def _rstd_pass(load_rows, stat_ref, n_rows):
    def rows(r):
        v = load_rows(r)
        sq = v * v
        parts = [sq[:, k * LANES:(k + 1) * LANES] for k in range(D_MODEL // LANES)]
        while len(parts) > 1:
            parts = [parts[k] + parts[k + 1] for k in range(0, len(parts), 2)]
        total = jnp.sum(parts[0], axis=-1, keepdims=True)
        rstd = lax.rsqrt(total * (1.0 / D_MODEL) + EPS)
        stat_ref[r, :] = jnp.broadcast_to(rstd, (SUBLANES, LANES))
    _for_rows(n_rows, SUBLANES, rows, unroll=16)


def _rms(v, g):
    return v * lax.rsqrt(jnp.mean(v * v, axis=-1, keepdims=True) + EPS) * g


def _params(*sem):
    return pltpu.CompilerParams(dimension_semantics=sem, vmem_limit_bytes=VMEM_LIMIT)


def _ada_kernel(c_ref, w_ref, b_ref, o_ref):
    c = c_ref[...]
    s = (c / (1.0 + jnp.exp(-c))).astype(BF16)
    o_ref[...] = _dot(s, w_ref[...].astype(BF16)) + b_ref[...]


def _ada_call(cc, w_ada, b_ada):
    n_out = N_MOD * D_MODEL
    return pl.pallas_call(
        _ada_kernel,
        grid=(DEPTH, n_out // TN_ADA),
        in_specs=[
            pl.BlockSpec((MOD_ROWS, D_MODEL), lambda i, j: (0, 0)),
            pl.BlockSpec((None, D_MODEL, TN_ADA), lambda i, j: (i, 0, j)),
            pl.BlockSpec((None, 1, TN_ADA), lambda i, j: (i, 0, j)),
        ],
        out_specs=pl.BlockSpec((None, MOD_ROWS, TN_ADA), lambda i, j: (i, 0, j)),
        out_shape=jax.ShapeDtypeStruct((DEPTH, MOD_ROWS, n_out), F32),
        compiler_params=_params("arbitrary", "arbitrary"),
        name="adaln",
    )(cc, w_ada, b_ada.reshape(DEPTH, 1, n_out))


def _mod_spec(layer, group, tm):
    return pl.BlockSpec((None, None, None, 3, 1, D_MODEL),
                        lambda t, *_: (layer, _mod_row(t, tm), group, 0, 0, 0))


def _norm_spec(layer, k):
    return pl.BlockSpec((None, None, 1, D_MODEL), lambda t, *_: (layer, k, 0, 0))


def _modulated_norm_to(h_ref, s_ref, stat_ref, vec_ref, m_ref, gpre_ref, n_rows, also=None):
    _rstd_pass(lambda r: s_ref[r, :], stat_ref, n_rows)
    full = (ROW_CHUNK, D_MODEL)
    vec_ref[0:ROW_CHUNK, :] = jnp.broadcast_to(gpre_ref[...] * (1.0 + m_ref[1]), full)
    vec_ref[ROW_CHUNK:, :] = jnp.broadcast_to(m_ref[0], full)

    def rows(r):
        rstd = stat_ref[r, :]
        for k in range(D_MODEL // LANES):
            cs = slice(k * LANES, (k + 1) * LANES)
            h = s_ref[r, cs] * rstd * vec_ref[0:ROW_CHUNK, cs] + vec_ref[ROW_CHUNK:, cs]
            h_ref[r, cs] = h.astype(BF16)
        if also is not None:
            also(r)
    _for_rows(n_rows, ROW_CHUNK, rows, unroll=2)


def _gated_residual_to(o_ref, s_ref, y_ref, stat_ref, vec_ref, gate_gain, n_rows):
    _rstd_pass(lambda r: y_ref[r, :], stat_ref, n_rows)
    vec_ref[0:ROW_CHUNK, :] = jnp.broadcast_to(gate_gain, (ROW_CHUNK, D_MODEL))

    def rows(r):
        rstd = stat_ref[r, :]
        for k in range(D_MODEL // LANES):
            cs = slice(k * LANES, (k + 1) * LANES)
            o_ref[r, cs] = s_ref[r, cs] + y_ref[r, cs] * rstd * vec_ref[0:ROW_CHUNK, cs]
    _for_rows(n_rows, ROW_CHUNK, rows, unroll=2)


def _norm_scratch(n_rows):
    return [pltpu.VMEM((n_rows, LANES), F32), pltpu.VMEM((2 * ROW_CHUNK, D_MODEL), F32)]


def _hosted_cast(arr, lead, n_inner):
    rows, cols = arr.shape[-2:]
    block = (rows // LAT_TILES, cols // n_inner)

    def index(t, i):
        return jnp.minimum(t, LAT_TILES - 1), jnp.where(t < LAT_TILES, i, n_inner - 1)

    in_spec = pl.BlockSpec((None,) * len(lead) + block, lambda t, i: lead + index(t, i))
    return in_spec, pl.BlockSpec(block, index), jax.ShapeDtypeStruct((rows, cols), BF16)


def _ffn_kernel(*refs, n_src, n_cast, n_tiles):
    srcs, refs = refs[:n_src], refs[n_src:]
    m_ref, gpre_ref, gpost_ref, wg_ref, wu_ref, wd_ref = refs[:6]
    cast_src, o_hbm, cast_dst = refs[6:6 + n_cast], refs[6 + n_cast], refs[7 + n_cast:7 + 2 * n_cast]
    slots, acc_ref, h_ref, stat_ref, vec_ref, fetch_sem, store_sem = refs[7 + 2 * n_cast:]
    t = pl.program_id(0)
    f = pl.program_id(1)
    cur = slots.at[lax.rem(t, 2)]

    def tile_rows(ref, tile):
        return ref.at[pl.ds(pl.multiple_of(tile * TM, TM), TM), :]

    def fetch(tile, start):
        slot = lax.rem(tile, 2)

        def run(src, src_tile):
            copy = pltpu.make_async_copy(tile_rows(src, src_tile), slots.at[slot], fetch_sem.at[slot])
            copy.start() if start else copy.wait()
        if n_src == 1:
            run(srcs[0], tile)
        else:
            pl.when(tile < LAT_TILES)(lambda: run(srcs[0], tile))
            pl.when(tile >= LAT_TILES)(lambda: run(srcs[1], tile - LAT_TILES))

    def store(tile):
        slot = lax.rem(tile, 2)
        return pltpu.make_async_copy(slots.at[slot], tile_rows(o_hbm, tile), store_sem.at[slot])

    @pl.when((t == 0) & (f == 0))
    def _first_fetch():
        fetch(t, start=True)

    @pl.when(f == 0)
    def _prologue():
        fetch(t, start=False)

        def zero_acc(r):
            acc_ref[r, :] = jnp.zeros((ROW_CHUNK, D_MODEL), F32)
        _modulated_norm_to(h_ref, cur, stat_ref, vec_ref, m_ref, gpre_ref, TM, also=zero_acc)

    @pl.when(f == 1)
    def _refill_other_slot():
        @pl.when(t >= 1)
        def _():
            store(t - 1).wait()

        @pl.when(t + 1 < n_tiles)
        def _():
            fetch(t + 1, start=True)

    for src, dst in zip(cast_src, cast_dst):
        dst[...] = src[...].astype(BF16)
    h = h_ref[...]
    g = _dot(h, wg_ref[...])
    u = _dot(h, wu_ref[...])
    a = (g / (1.0 + jnp.exp(-g)) * u).astype(BF16)
    for c in range(D_MODEL // 1024):
        cs = slice(c * 1024, (c + 1) * 1024)
        acc_ref[:, cs] += _dot(a, wd_ref[:, cs])

    @pl.when(f == pl.num_programs(1) - 1)
    def _epilogue():
        _gated_residual_to(cur, cur, acc_ref, stat_ref, vec_ref, 0.5 * m_ref[2] * gpost_ref[...], TM)
        store(t).start()

        @pl.when(t == n_tiles - 1)
        def _():
            store(t).wait()


def _ffn_call(srcs, mods, norm_pre, norm_post, weights, layer, half, n_tiles, cast_next=()):
    k = 2 * half
    n_f = FFN_DIM // TF
    assert n_f >= 2 and n_tiles >= 2
    hosted = [_hosted_cast(arr, lead, n_f) for arr, lead in cast_next]
    hbm = pl.BlockSpec(memory_space=pl.ANY)
    out = pl.pallas_call(
        functools.partial(_ffn_kernel, n_src=len(srcs), n_cast=len(hosted), n_tiles=n_tiles),
        grid=(n_tiles, n_f),
        in_specs=[hbm] * len(srcs) + [
            _mod_spec(layer, k, TM),
            _norm_spec(layer, k),
            _norm_spec(layer, k),
            pl.BlockSpec((D_MODEL, TF), lambda t, f: (0, f)),
            pl.BlockSpec((D_MODEL, TF), lambda t, f: (0, f)),
            pl.BlockSpec((TF, D_MODEL), lambda t, f: (f, 0)),
        ] + [h[0] for h in hosted],
        out_specs=[hbm] + [h[1] for h in hosted],
        out_shape=[jax.ShapeDtypeStruct((n_tiles * TM, D_MODEL), F32)] + [h[2] for h in hosted],
        scratch_shapes=[
            pltpu.VMEM((2, TM, D_MODEL), F32),
            pltpu.VMEM((TM, D_MODEL), F32),
            pltpu.VMEM((TM, D_MODEL), BF16),
        ] + _norm_scratch(TM) + [pltpu.SemaphoreType.DMA((2,)), pltpu.SemaphoreType.DMA((2,))],
        compiler_params=_params("arbitrary", "arbitrary"),
        name=f"ffn_l{layer}h{half}",
    )(*srcs, mods, norm_pre, norm_post, *weights, *[arr for arr, _ in cast_next])
    return out[0], tuple(out[1:])


def _is_rope_block(jj):
    blk = lambda col: col // RH
    return (jj < blk(COL_VA)) | ((jj >= blk(COL_QD)) & (jj < blk(COL_VD)))


def _inproj_kernel(s_ref, m_ref, gpre_ref, w_ref, cos_ref, sina_ref, sinb_ref, *rest):
    n_cast = (len(rest) - 5) // 2
    cast_src, (p_ref, u_ref) = rest[:n_cast], rest[n_cast:n_cast + 2]
    cast_dst = rest[n_cast + 2:2 * n_cast + 2]
    h_ref, stat_ref, vec_ref = rest[2 * n_cast + 2:]
    t = pl.program_id(0)
    j = pl.program_id(1)

    @pl.when(j == 0)
    def _prologue():
        _modulated_norm_to(h_ref, s_ref, stat_ref, vec_ref, m_ref, gpre_ref, TM)

    for src, dst in zip(cast_src, cast_dst):
        dst[...] = src[...].astype(BF16)

    h = h_ref[...]
    chunks = []
    for c in range(TN_IN // MXU_COLS):
        if c % (RH // MXU_COLS) == 0:
            rope = _is_rope_block(j * (TN_IN // RH) + c // (RH // MXU_COLS)) & (t < LAT_TILES)
            cos = jnp.where(rope, cos_ref[...], 1.0)
            sina = jnp.where(rope, sina_ref[...], 0.0)
            sinb = jnp.where(rope, sinb_ref[...], 0.0)
        y = _dot(h, w_ref[:, c * MXU_COLS:(c + 1) * MXU_COLS])
        chunks.append(y)
        for hd in range(MXU_COLS // HEAD_DIM):
            v = y[:, hd * HEAD_DIM:(hd + 1) * HEAD_DIM]
            fwd = pltpu.roll(v, HEAD_DIM // 4, 1)
            bwd = pltpu.roll(v, HEAD_DIM - HEAD_DIM // 4, 1)
            col = c * MXU_COLS + hd * HEAD_DIM
            p_ref[:, col:col + HEAD_DIM] = (v * cos + bwd * sina + fwd * sinb).astype(BF16)

    for step in range(IN_WIDTH // TN_IN):
        for c in range(TN_IN // MXU_COLS):
            col = step * TN_IN + c * MXU_COLS
            if COL_U <= col < COL_QD:
                @pl.when(j == step)
                def _pool_channels(c=c, col=col):
                    u_ref[:, col - COL_U:col - COL_U + MXU_COLS] = chunks[c]


def _inproj_call(s, mods, norm_pre, w_in, cos, sina, sinb, layer, cast_next=()):
    n_j = IN_WIDTH // TN_IN
    hosted = [_hosted_cast(arr, lead, n_j) for arr, lead in cast_next]
    rope_spec = pl.BlockSpec((TM, HEAD_DIM), lambda t, j: (t % (SEQ // TM), 0))
    out = pl.pallas_call(
        _inproj_kernel,
        grid=(ALL_TILES, n_j),
        in_specs=[
            pl.BlockSpec((TM, D_MODEL), lambda t, j: (t, 0)),
            _mod_spec(layer, 1, TM),
            _norm_spec(layer, 1),
            pl.BlockSpec((D_MODEL, TN_IN), lambda t, j: (0, j)),
            rope_spec, rope_spec, rope_spec,
        ] + [h[0] for h in hosted],
        out_specs=[
            pl.BlockSpec((TM, TN_IN), lambda t, j: (t, j)),
            pl.BlockSpec((TM, POOL_WIDTH), lambda t, j: (t, 0)),
        ] + [h[1] for h in hosted],
        out_shape=[
            jax.ShapeDtypeStruct((N_ALL, IN_WIDTH), BF16),
            jax.ShapeDtypeStruct((N_ALL, POOL_WIDTH), F32),
        ] + [h[2] for h in hosted],
        scratch_shapes=[pltpu.VMEM((TM, D_MODEL), BF16)] + _norm_scratch(TM),
        compiler_params=_params("arbitrary", "arbitrary"),
        name=f"inproj_l{layer}",
    )(s, mods, norm_pre, w_in, cos, sina, sinb, *[arr for arr, _ in cast_next])
    return out[0], out[1], tuple(out[2:])


LAT_QBLOCKS = SEQ // BLOCK
CTX_QBLOCKS = CTX_LEN // BLOCK
BAND = 3 * BLOCK


def _win_kernel(sink_ref, q_ref, k_ref, v_ref, kc_ref, vc_ref, o_ref):
    n = pl.program_id(1)
    to_log2 = HEAD_DIM ** -0.5 * LOG2E
    row = lax.broadcasted_iota(jnp.int32, (WIN_GROUP * BLOCK, 1), 0)

    def head_slice(i):
        return slice(i * HEAD_DIM, (i + 1) * HEAD_DIM)

    def group_q(h):
        return jnp.concatenate([q_ref[:, head_slice(WIN_GROUP * h + g)] for g in range(WIN_GROUP)], axis=0)

    def group_sink(h):
        s0 = sink_ref[0, WIN_GROUP * h]
        s1 = sink_ref[0, WIN_GROUP * h + 1]
        s2 = sink_ref[0, WIN_GROUP * h + 2]
        return jnp.where(row < BLOCK, s0, jnp.where(row < 2 * BLOCK, s1, s2)) * LOG2E

    def store(h, o):
        for g in range(WIN_GROUP):
            o_ref[:, head_slice(WIN_GROUP * h + g)] = o[g * BLOCK:(g + 1) * BLOCK].astype(BF16)

    @pl.when(n < LAT_QBLOCKS)
    def _latent_queries():
        start = pl.multiple_of(jnp.clip((n - 1) * BLOCK, 0, SEQ - BAND), BLOCK)
        qpos = n * BLOCK + (row & (BLOCK - 1))
        kpos = start + lax.broadcasted_iota(jnp.int32, (1, BAND), 1)
        valid = jnp.abs(kpos - qpos) <= WINDOW
        for h in range(WIN_KV_HEADS):
            hs = head_slice(h)
            q = group_q(h)
            s_loc = jnp.where(valid, _dot_nt(q, k_ref[pl.ds(start, BAND), hs]) * to_log2, -jnp.inf)
            s_ctx = _dot_nt(q, kc_ref[:, hs]) * to_log2
            sink = group_sink(h)
            m = jnp.maximum(jnp.maximum(jnp.max(s_loc, axis=-1, keepdims=True),
                                        jnp.max(s_ctx, axis=-1, keepdims=True)), sink)
            e_loc = jnp.exp2(s_loc - m)
            e_ctx = jnp.exp2(s_ctx - m)
            den = (jnp.sum(e_loc, axis=-1, keepdims=True) + jnp.sum(e_ctx, axis=-1, keepdims=True)
                   + jnp.exp2(sink - m))
            o = _dot(e_loc.astype(BF16), v_ref[pl.ds(start, BAND), hs]) + _dot(e_ctx.astype(BF16), vc_ref[:, hs])
            store(h, o * (1.0 / den))

    @pl.when(n >= LAT_QBLOCKS)
    def _context_queries():
        for h in range(WIN_KV_HEADS):
            hs = head_slice(h)
            s_ctx = _dot_nt(group_q(h), kc_ref[:, hs]) * to_log2
            sink = group_sink(h)
            m = jnp.maximum(jnp.max(s_ctx, axis=-1, keepdims=True), sink)
            e_ctx = jnp.exp2(s_ctx - m)
            den = jnp.sum(e_ctx, axis=-1, keepdims=True) + jnp.exp2(sink - m)
            store(h, _dot(e_ctx.astype(BF16), vc_ref[:, hs]) * (1.0 / den))


def _win_call(p, sink, layer, with_ctx_queries):
    nq = LAT_QBLOCKS + (CTX_QBLOCKS if with_ctx_queries else 0)
    kv_cols = WIN_KV_HEADS * HEAD_DIM

    def q_row(b, n):
        return jnp.where(n < LAT_QBLOCKS, b * LAT_QBLOCKS + n,
                         N_LAT // BLOCK + b * CTX_QBLOCKS + n - LAT_QBLOCKS)

    ctx_row = lambda b: N_LAT // CTX_LEN + b
    return pl.pallas_call(
        _win_kernel,
        grid=(BATCH, nq),
        in_specs=[
            pl.BlockSpec(memory_space=pltpu.SMEM),
            pl.BlockSpec((BLOCK, WIN_WIDTH), lambda b, n: (q_row(b, n), 0)),
            pl.BlockSpec((SEQ, kv_cols), lambda b, n: (b, COL_KA // kv_cols)),
            pl.BlockSpec((SEQ, kv_cols), lambda b, n: (b, COL_VA // kv_cols)),
            pl.BlockSpec((CTX_LEN, kv_cols), lambda b, n: (ctx_row(b), COL_KA // kv_cols)),
            pl.BlockSpec((CTX_LEN, kv_cols), lambda b, n: (ctx_row(b), COL_VA // kv_cols)),
        ],
        out_specs=pl.BlockSpec((BLOCK, WIN_WIDTH), lambda b, n: (q_row(b, n), 0)),
        out_shape=jax.ShapeDtypeStruct((N_ALL if with_ctx_queries else N_LAT, WIN_WIDTH), BF16),
        compiler_params=_params("arbitrary", "arbitrary"),
        name=f"win_attn_l{layer}",
    )(sink[layer].reshape(1, WIN_Q_HEADS), p, p, p, p, p)


LAT_QTILES = SEQ // TQ_DIFF


def _diff_kernel(lam_ref, subln_ref, q_ref, k_ref, v_ref, kc_ref, vc_ref, *rest, lam_init):
    o_ref = rest[-2] if len(rest) == 3 else rest[0]
    i = pl.program_id(2)
    to_log2 = HEAD_DIM ** -0.5 * LOG2E
    dl = lam_ref[...]
    lam = (jnp.exp(jnp.sum(dl[0:1] * dl[1:2], axis=-1, keepdims=True))
           - jnp.exp(jnp.sum(dl[2:3] * dl[3:4], axis=-1, keepdims=True)) + lam_init)
    out_gain = subln_ref[...] * (1.0 - lam_init)

    def attend(q, with_latent):
        maps = []
        for j in range(2):
            cs = slice(j * HEAD_DIM, (j + 1) * HEAD_DIM)
            s_ctx = _dot_nt(q[:, cs], kc_ref[:, cs])
            m = jnp.max(s_ctx, axis=-1, keepdims=True)
            e_lat = None
            if with_latent:
                s_lat = _dot_nt(q[:, cs], k_ref[:, cs])
                m = jnp.maximum(m, jnp.max(s_lat, axis=-1, keepdims=True))
                e_lat = jnp.exp2((s_lat - m) * to_log2)
            e_ctx = jnp.exp2((s_ctx - m) * to_log2)
            den = jnp.sum(e_ctx, axis=-1, keepdims=True)
            if with_latent:
                den = den + jnp.sum(e_lat, axis=-1, keepdims=True)
            maps.append((e_lat, e_ctx, 1.0 / den))
        (e1_lat, e1_ctx, inv1), (e2_lat, e2_ctx, inv2) = maps
        w2 = lam * inv2
        o = _dot((e1_ctx * inv1 - e2_ctx * w2).astype(BF16), vc_ref[...])
        if with_latent:
            o = o + _dot((e1_lat * inv1 - e2_lat * w2).astype(BF16), v_ref[...])
        return (_rms(o, out_gain)).astype(BF16)

    for sub in range(TQ_DIFF // TQ_SUB):
        rs = slice(sub * TQ_SUB, (sub + 1) * TQ_SUB)
        o_ref[rs, :] = attend(q_ref[rs, :], True)

    if len(rest) == 3:
        qc_ref, _, oc_ref = rest

        @pl.when(i == LAT_QTILES - 1)
        def _context_queries():
            oc_ref[...] = attend(qc_ref[...], False)


def _diff_call(p, diff_lambda, diff_subln, layer, with_ctx_queries):
    lam_init = 0.8 - 0.6 * math.exp(-0.3 * layer)
    w = DIFF_V_DIM
    ctx_row = lambda b: N_LAT // CTX_LEN + b
    in_specs = [
        pl.BlockSpec((None, 4, HEAD_DIM), lambda b, h, i: (layer, 0, 0)),
        pl.BlockSpec((None, 1, DIFF_V_DIM), lambda b, h, i: (layer, 0, 0)),
        pl.BlockSpec((TQ_DIFF, w), lambda b, h, i: (b * LAT_QTILES + i, COL_QD // w + h)),
        pl.BlockSpec((SEQ, w), lambda b, h, i: (b, COL_KD // w + h)),
        pl.BlockSpec((SEQ, w), lambda b, h, i: (b, COL_VD // w + h)),
        pl.BlockSpec((CTX_LEN, w), lambda b, h, i: (ctx_row(b), COL_KD // w + h)),
        pl.BlockSpec((CTX_LEN, w), lambda b, h, i: (ctx_row(b), COL_VD // w + h)),
    ]
    out_specs = [pl.BlockSpec((TQ_DIFF, w), lambda b, h, i: (b * LAT_QTILES + i, h))]
    out_shape = [jax.ShapeDtypeStruct((N_LAT, DIFF_WIDTH), BF16)]
    operands = [diff_lambda, diff_subln.reshape(DEPTH, 1, DIFF_V_DIM), p, p, p, p, p]
    if with_ctx_queries:
        in_specs.append(pl.BlockSpec((CTX_LEN, w), lambda b, h, i: (ctx_row(b), COL_QD // w + h)))
        out_specs.append(pl.BlockSpec((CTX_LEN, w), lambda b, h, i: (b, h)))
        out_shape.append(jax.ShapeDtypeStruct((N_CTX, DIFF_WIDTH), BF16))
        operands.append(p)
    return pl.pallas_call(
        functools.partial(_diff_kernel, lam_init=lam_init),
        grid=(BATCH, DIFF_HEADS, LAT_QTILES),
        in_specs=in_specs,
        out_specs=out_specs,
        out_shape=out_shape,
        compiler_params=_params("arbitrary", "arbitrary", "arbitrary"),
        name=f"diff_attn_l{layer}",
    )(*operands)


def _pool_kernel(u_ref, w_ref, scale_ref, o_ref, pad_ref, *, n_tok):
    t = lax.broadcasted_iota(jnp.int32, (n_tok, 1), 0)
    zeros = jnp.zeros((POOL_HALO, POOL_GROUP), F32)
    pad_ref[0:POOL_HALO, :] = zeros
    pad_ref[POOL_HALO + n_tok:, :] = zeros
    for g, w in enumerate(POOL_WINDOWS):
        cs = slice(g * POOL_GROUP, (g + 1) * POOL_GROUP)
        u = u_ref[:, cs]
        pad_ref[POOL_HALO:POOL_HALO + n_tok, :] = u
        total = pad_ref[POOL_HALO - w // 2:POOL_HALO - w // 2 + n_tok, :]
        for d in range(1 - w // 2, w // 2):
            total = total + pad_ref[POOL_HALO + d:POOL_HALO + d + n_tok, :]
        count = (jnp.clip(t + w // 2, 0, n_tok) - jnp.clip(t - w // 2, 0, n_tok)).astype(F32)
        centred = (total / count - u).astype(BF16)
        o_ref[:, cs] = (_dot(centred, w_ref[g]) * scale_ref[:, cs]).astype(BF16)


def _pool_call(u, pool_w, pool_scale, layer, n_tok, row_block0):
    return pl.pallas_call(
        functools.partial(_pool_kernel, n_tok=n_tok),
        grid=(BATCH,),
        in_specs=[
            pl.BlockSpec((n_tok, POOL_WIDTH), lambda b: (row_block0 + b, 0)),
            pl.BlockSpec((None, len(POOL_WINDOWS), POOL_GROUP, POOL_GROUP), lambda b: (layer, 0, 0, 0)),
            pl.BlockSpec((None, 1, POOL_WIDTH), lambda b: (layer, 0, 0)),
        ],
        out_specs=pl.BlockSpec((n_tok, POOL_WIDTH), lambda b: (b, 0)),
        out_shape=jax.ShapeDtypeStruct((BATCH * n_tok, POOL_WIDTH), BF16),
        scratch_shapes=[pltpu.VMEM((n_tok + 2 * POOL_HALO, POOL_GROUP), F32)],
        compiler_params=_params("arbitrary"),
        name=f"pool_l{layer}_n{n_tok}",
    )(u, pool_w, pool_scale.reshape(DEPTH, 1, POOL_WIDTH))


def _outproj_kernel(s_ref, m_ref, gpost_ref, oa_ref, ob_ref, oc_ref, w_ref, o_ref, stat_ref, vec_ref):
    oa, ob, oc = oa_ref[...], ob_ref[...], oc_ref[...]
    b0, c0 = WIN_WIDTH, WIN_WIDTH + POOL_WIDTH
    for n in range(D_MODEL // 1024):
        cs = slice(n * 1024, (n + 1) * 1024)
        o_ref[:, cs] = _dot(oa, w_ref[0:b0, cs]) + _dot(ob, w_ref[b0:c0, cs]) + _dot(oc, w_ref[c0:, cs])
    _gated_residual_to(o_ref, s_ref, o_ref, stat_ref, vec_ref, m_ref[2] * gpost_ref[...], TM_OUT)


def _outproj_call(s, mods, norm_post, o_a, o_b, o_c, w_out, layer, n_rows):
    row_spec = lambda width: pl.BlockSpec((TM_OUT, width), lambda t: (t, 0))
    return pl.pallas_call(
        _outproj_kernel,
        grid=(n_rows // TM_OUT,),
        in_specs=[
            row_spec(D_MODEL),
            _mod_spec(layer, 1, TM_OUT),
            _norm_spec(layer, 1),
            row_spec(WIN_WIDTH), row_spec(POOL_WIDTH), row_spec(DIFF_WIDTH),
            pl.BlockSpec((D_MODEL, D_MODEL), lambda t: (0, 0), pipeline_mode=pl.Buffered(1)),
        ],
        out_specs=row_spec(D_MODEL),
        out_shape=jax.ShapeDtypeStruct((n_rows, D_MODEL), F32),
        scratch_shapes=_norm_scratch(TM_OUT),
        compiler_params=_params("arbitrary"),
        name=f"outproj_l{layer}",
    )(s, mods, norm_post, o_a, o_b, o_c, w_out)


def _rope_tables():
    row = jnp.repeat(jnp.arange(SEQ // GRID_W, dtype=F32), GRID_W)
    col = jnp.tile(jnp.arange(GRID_W, dtype=F32), SEQ // GRID_W)
    half = HEAD_DIM // 2
    inv = ROPE_THETA ** (-jnp.arange(0, half, 2, dtype=F32) / half)
    ar = row[:, None] * inv
    ac = col[:, None] * inv
    ang = jnp.concatenate([ar, ar, ac, ac], axis=-1)
    cos, sin = jnp.cos(ang), jnp.sin(ang)
    low = (jnp.arange(HEAD_DIM) % half) < half // 2
    return cos, jnp.where(low, -sin, 0.0), jnp.where(low, 0.0, sin)


def kernel(x, c, ctx, c_ctx, w_ada, b_ada, norm_pre, norm_post, ffn_w_gate, ffn_w_up, ffn_w_down,
           w_in, w_out, attn_sink, pool_w, pool_scale, diff_lambda, diff_subln):
    srcs = (x.reshape(N_LAT, D_MODEL), ctx.reshape(N_CTX, D_MODEL))
    cc =jnp.zeros((MOD_ROWS, D_MODEL), F32).at[:BATCH].set(c).at[BATCH].set(c_ctx)
    mods = _ada_call(cc, w_ada, b_ada).reshape(DEPTH, MOD_ROWS, 3, 3, 1, D_MODEL)
    npre = norm_pre.reshape(DEPTH, 3, 1, D_MODEL)
    npost = norm_post.reshape(DEPTH, 3, 1, D_MODEL)
    pool_w16 = pool_w.astype(BF16)
    cos, sina, sinb = _rope_tables()

    ffn_f32 = (ffn_w_gate, ffn_w_up, ffn_w_down)
    ffn_order = [(layer, half) for layer in range(DEPTH) for half in range(2)]
    ffn16 = tuple(w[0, 0].astype(BF16) for w in ffn_f32)
    w_in16, w_out16 = w_in[0].astype(BF16), w_out[0].astype(BF16)

    def next_ffn(layer, half):
        i = ffn_order.index((layer, half)) + 1
        return [(w, ffn_order[i]) for w in ffn_f32] if i < len(ffn_order) else []

    for layer in range(DEPTH):
        update_ctx = layer < DEPTH - 1
        n_tiles = ALL_TILES if update_ctx else LAT_TILES
        s, ffn16 = _ffn_call(srcs, mods, npre, npost, ffn16, layer, 0, ALL_TILES, next_ffn(layer, 0))
        proj_next = [(w_in, (layer + 1,)), (w_out, (layer + 1,))] if update_ctx else []
        p, u, proj16 = _inproj_call(s, mods, npre, w_in16, cos, sina, sinb, layer, proj_next)
        o_a = _win_call(p, attn_sink, layer, update_ctx)
        o_c = _diff_call(p, diff_lambda, diff_subln, layer, update_ctx)
        o_b = [_pool_call(u, pool_w16, pool_scale, layer, SEQ, 0)]
        if update_ctx:
            o_b.append(_pool_call(u, pool_w16, pool_scale, layer, CTX_LEN, N_LAT // CTX_LEN))
        o_b, o_c = jnp.concatenate(o_b, axis=0), jnp.concatenate(o_c, axis=0)
        s = _outproj_call(s, mods, npost, o_a, o_b, o_c, w_out16, layer, n_tiles * TM)
        s, ffn16 = _ffn_call((s,), mods, npre, npost, ffn16, layer, 1, n_tiles, next_ffn(layer, 1))
        srcs = (s,)
        if update_ctx:
            w_in16, w_out16 = proj16
    return s.reshape(BATCH, SEQ, D_MODEL)
```

```python
import functools
import math

import jax
import jax.numpy as jnp
from jax import lax
from jax.experimental import pallas as pl
from jax.experimental.pallas import tpu as pltpu

D_MODEL = 4096
BATCH = 4
SEQ = 2048
DEPTH = 2
GRID_W = 64
CTX_LEN = 256
HEAD_DIM = 128
ROPE_THETA = 10000.0
EPS = 1e-6
N_MOD = 9
FFN_DIM = 2 * D_MODEL

WIN_Q_HEADS = 12
WIN_KV_HEADS = 4
WIN_GROUP = WIN_Q_HEADS // WIN_KV_HEADS
WIN_WIDTH = WIN_Q_HEADS * HEAD_DIM
WINDOW = 128
BLOCK = 128

POOL_WINDOWS = (2, 4, 8, 16)
POOL_WIDTH = 1024
POOL_GROUP = POOL_WIDTH // len(POOL_WINDOWS)
POOL_HALO = 8

DIFF_HEADS = 6
DIFF_V_DIM = 2 * HEAD_DIM
DIFF_WIDTH = DIFF_HEADS * DIFF_V_DIM
IN_WIDTH = 8192

COL_QA = 0
COL_KA = COL_QA + WIN_WIDTH
COL_VA = COL_KA + WIN_KV_HEADS * HEAD_DIM
COL_U = COL_VA + WIN_KV_HEADS * HEAD_DIM
COL_QD = COL_U + POOL_WIDTH
COL_KD = COL_QD + 2 * DIFF_HEADS * HEAD_DIM
COL_VD = COL_KD + 2 * DIFF_HEADS * HEAD_DIM

F32 = jnp.float32
BF16 = jnp.bfloat16
LANES = 128
SUBLANES = 8
LOG2E = math.log2(math.e)
Q_TO_LOG2 = HEAD_DIM ** -0.5 * LOG2E
ADA_WIDTH = N_MOD * D_MODEL

N_LAT = BATCH * SEQ
N_CTX = BATCH * CTX_LEN
N_ALL = N_LAT + N_CTX

TM = 512
TM_OUT = 256
TF = 512
TN_IN = 1024
RH = 512
MXU_COLS = 256
TN_ADA = 512
TQ_DIFF = 1024
TQ_SUB = 256
ROW_CHUNK = 16
MOD_ROWS = 16
VMEM_LIMIT = 63 * 1024 * 1024

LAT_TILES = N_LAT // TM
ALL_TILES = N_ALL // TM


def _mod_row(t, tm):
    return jnp.where(t < N_LAT // tm, t // (SEQ // tm), BATCH)


def _dot(a, b):
    return jnp.dot(a, b, preferred_element_type=F32)


def _dot_nt(a, b):
    return lax.dot_general(a, b, (((1,), (1,)), ((), ())), preferred_element_type=F32)


def _for_rows(n_rows, chunk, fn, unroll=1):
    def body(i, carry):
        fn(pl.ds(pl.multiple_of(i * chunk, chunk), chunk))
        return carry
    lax.fori_loop(0, n_rows // chunk, body, 0, unroll=unroll)


def _rstd_pass(load_rows, stat_ref, n_rows):
    def rows(r):
        v = load_rows(r)
        sq = v * v
        parts = [sq[:, k * LANES:(k + 1) * LANES] for k in range(D_MODEL // LANES)]
        while len(parts) > 1:
            parts = [parts[k] + parts[k + 1] for k in range(0, len(parts), 2)]
        total = jnp.sum(parts[0], axis=-1, keepdims=True)
        rstd = lax.rsqrt(total * (1.0 / D_MODEL) + EPS)
        stat_ref[r, :] = jnp.broadcast_to(rstd, (SUBLANES, LANES))
    _for_rows(n_rows, SUBLANES, rows, unroll=16)


def _rms(v, g):
    return v * lax.rsqrt(jnp.mean(v * v, axis=-1, keepdims=True) + EPS) * g


def _params(*sem):
    return pltpu.CompilerParams(dimension_semantics=sem, vmem_limit_bytes=VMEM_LIMIT)


def _ada_kernel(c_ref, w_ref, b_ref, o_ref):
    c = c_ref[...]
    s = (c / (1.0 + jnp.exp(-c))).astype(BF16)
    o_ref[...] = _dot(s, w_ref[...].astype(BF16)) + b_ref[...]


def _ada_call(cc, w_ada, b_ada, layer):
    return pl.pallas_call(
        _ada_kernel,
        grid=(ADA_WIDTH // TN_ADA,),
        in_specs=[
            pl.BlockSpec((MOD_ROWS, D_MODEL), lambda j: (0, 0)),
            pl.BlockSpec((None, D_MODEL, TN_ADA), lambda j: (layer, 0, j)),
            pl.BlockSpec((None, 1, TN_ADA), lambda j: (layer, 0, j)),
        ],
        out_specs=pl.BlockSpec((MOD_ROWS, TN_ADA), lambda j: (0, j)),
        out_shape=jax.ShapeDtypeStruct((MOD_ROWS, ADA_WIDTH), F32),
        compiler_params=_params("arbitrary"),
        name=f"adaln_l{layer}",
    )(cc, w_ada, b_ada.reshape(DEPTH, 1, ADA_WIDTH))


def _mod_spec(group, tm):
    return pl.BlockSpec((None, None, 3, 1, D_MODEL), lambda t, *_: (_mod_row(t, tm), group, 0, 0, 0))


def _norm_spec(layer, k):
    return pl.BlockSpec((None, None, 1, D_MODEL), lambda t, *_: (layer, k, 0, 0))


def _modulated_norm_to(h_ref, s_ref, stat_ref, vec_ref, m_ref, gpre_ref, n_rows, also=None):
    _rstd_pass(lambda r: s_ref[r, :], stat_ref, n_rows)
    full = (ROW_CHUNK, D_MODEL)
    vec_ref[0:ROW_CHUNK, :] = jnp.broadcast_to(gpre_ref[...] * (1.0 + m_ref[1]), full)
    vec_ref[ROW_CHUNK:, :] = jnp.broadcast_to(m_ref[0], full)

    def rows(r):
        rstd = stat_ref[r, :]
        for k in range(D_MODEL // LANES):
            cs = slice(k * LANES, (k + 1) * LANES)
            h = s_ref[r, cs] * rstd * vec_ref[0:ROW_CHUNK, cs] + vec_ref[ROW_CHUNK:, cs]
            h_ref[r, cs] = h.astype(BF16)
        if also is not None:
            also(r)
    _for_rows(n_rows, ROW_CHUNK, rows, unroll=2)


def _gated_residual_to(o_ref, s_ref, y_ref, stat_ref, vec_ref, gate_gain, n_rows):
    _rstd_pass(lambda r: y_ref[r, :], stat_ref, n_rows)
    vec_ref[0:ROW_CHUNK, :] = jnp.broadcast_to(gate_gain, (ROW_CHUNK, D_MODEL))

    def rows(r):
        rstd = stat_ref[r, :]
        for k in range(D_MODEL // LANES):
            cs = slice(k * LANES, (k + 1) * LANES)
            o_ref[r, cs] = s_ref[r, cs] + y_ref[r, cs] * rstd * vec_ref[0:ROW_CHUNK, cs]
    _for_rows(n_rows, ROW_CHUNK, rows, unroll=2)


def _norm_scratch(n_rows):
    return [pltpu.VMEM((n_rows, LANES), F32), pltpu.VMEM((2 * ROW_CHUNK, D_MODEL), F32)]


def _hosted_cast(arr, lead, n_inner):
    rows, cols = arr.shape[-2:]
    block = (rows // LAT_TILES, cols // n_inner)

    def index(t, i):
        return jnp.minimum(t, LAT_TILES - 1), jnp.where(t < LAT_TILES, i, n_inner - 1)

    in_spec = pl.BlockSpec((None,) * len(lead) + block, lambda t, i: lead + index(t, i))
    return in_spec, pl.BlockSpec(block, index), jax.ShapeDtypeStruct((rows, cols), BF16)


def _ffn_kernel(*refs, n_src, n_cast, n_tiles):
    srcs, refs = refs[:n_src], refs[n_src:]
    m_ref, gpre_ref, gpost_ref, wg_ref, wu_ref, wd_ref = refs[:6]
    cast_src, o_hbm, cast_dst = refs[6:6 + n_cast], refs[6 + n_cast], refs[7 + n_cast:7 + 2 * n_cast]
    slots, acc_ref, h_ref, stat_ref, vec_ref, fetch_sem, store_sem = refs[7 + 2 * n_cast:]
    t = pl.program_id(0)
    f = pl.program_id(1)
    cur = slots.at[lax.rem(t, 2)]

    def tile_rows(ref, tile):
        return ref.at[pl.ds(pl.multiple_of(tile * TM, TM), TM), :]

    def fetch(tile, start):
        slot = lax.rem(tile, 2)

        def run(src, src_tile):
            copy = pltpu.make_async_copy(tile_rows(src, src_tile), slots.at[slot], fetch_sem.at[slot])
            copy.start() if start else copy.wait()
        if n_src == 1:
            run(srcs[0], tile)
        else:
            pl.when(tile < LAT_TILES)(lambda: run(srcs[0], tile))
            pl.when(tile >= LAT_TILES)(lambda: run(srcs[1], tile - LAT_TILES))

    def store(tile):
        slot = lax.rem(tile, 2)
        return pltpu.make_async_copy(slots.at[slot], tile_rows(o_hbm, tile), store_sem.at[slot])

    @pl.when((t == 0) & (f == 0))
    def _first_fetch():
        fetch(t, start=True)

    @pl.when(f == 0)
    def _prologue():
        fetch(t, start=False)

        def zero_acc(r):
            acc_ref[r, :] = jnp.zeros((ROW_CHUNK, D_MODEL), F32)
        _modulated_norm_to(h_ref, cur, stat_ref, vec_ref, m_ref, gpre_ref, TM, also=zero_acc)

    @pl.when(f == 1)
    def _refill_other_slot():
        @pl.when(t >= 1)
        def _():
            store(t - 1).wait()

        @pl.when(t + 1 < n_tiles)
        def _():
            fetch(t + 1, start=True)

    h = h_ref[...]
    g = _dot(h, wg_ref[...])
    u = _dot(h, wu_ref[...])
    a = (g / (1.0 + jnp.exp(-g)) * u).astype(BF16)
    for c in range(D_MODEL // 1024):
        cs = slice(c * 1024, (c + 1) * 1024)
        acc_ref[:, cs] += _dot(a, wd_ref[:, cs])
    for src, dst in zip(cast_src, cast_dst):
        dst[...] = src[...].astype(BF16)

    @pl.when(f == pl.num_programs(1) - 1)
    def _epilogue():
        _gated_residual_to(cur, cur, acc_ref, stat_ref, vec_ref, 0.5 * m_ref[2] * gpost_ref[...], TM)
        store(t).start()

        @pl.when(t == n_tiles - 1)
        def _():
            store(t).wait()


def _ffn_call(srcs, mods, norm_pre, norm_post, weights, layer, half, n_tiles, cast_next=()):
    k = 2 * half
    n_f = FFN_DIM // TF
    assert n_f >= 2 and n_tiles >= 2
    hosted = [_hosted_cast(arr, lead, n_f) for arr, lead in cast_next]
    hbm = pl.BlockSpec(memory_space=pl.ANY)
    out = pl.pallas_call(
        functools.partial(_ffn_kernel, n_src=len(srcs), n_cast=len(hosted), n_tiles=n_tiles),
        grid=(n_tiles, n_f),
        in_specs=[hbm] * len(srcs) + [
            _mod_spec(k, TM),
            _norm_spec(layer, k),
            _norm_spec(layer, k),
            pl.BlockSpec((D_MODEL, TF), lambda t, f: (0, f)),
            pl.BlockSpec((D_MODEL, TF), lambda t, f: (0, f)),
            pl.BlockSpec((TF, D_MODEL), lambda t, f: (f, 0)),
        ] + [h[0] for h in hosted],
        out_specs=[hbm] + [h[1] for h in hosted],
        out_shape=[jax.ShapeDtypeStruct((n_tiles * TM, D_MODEL), F32)] + [h[2] for h in hosted],
        scratch_shapes=[
            pltpu.VMEM((2, TM, D_MODEL), F32),
            pltpu.VMEM((TM, D_MODEL), F32),
            pltpu.VMEM((TM, D_MODEL), BF16),
        ] + _norm_scratch(TM) + [pltpu.SemaphoreType.DMA((2,)), pltpu.SemaphoreType.DMA((2,))],
        compiler_params=_params("arbitrary", "arbitrary"),
        name=f"ffn_l{layer}h{half}",
    )(*srcs, mods, norm_pre, norm_post, *weights, *[arr for arr, _ in cast_next])
    return out[0], tuple(out[1:])


def _is_rope_block(jj):
    blk = lambda col: col // RH
    return (jj < blk(COL_VA)) | ((jj >= blk(COL_QD)) & (jj < blk(COL_VD)))


def _is_query_block(jj):
    blk = lambda col: col // RH
    return (jj < blk(COL_KA)) | ((jj >= blk(COL_QD)) & (jj < blk(COL_KD)))


def _inproj_kernel(s_ref, m_ref, gpre_ref, w_ref, cos_ref, sina_ref, sinb_ref, *rest):
    n_cast = (len(rest) - 5) // 2
    cast_src, (p_ref, u_ref) = rest[:n_cast], rest[n_cast:n_cast + 2]
    cast_dst = rest[n_cast + 2:2 * n_cast + 2]
    h_ref, stat_ref, vec_ref = rest[2 * n_cast + 2:]
    t = pl.program_id(0)
    j = pl.program_id(1)

    @pl.when(j == 0)
    def _prologue():
        _modulated_norm_to(h_ref, s_ref, stat_ref, vec_ref, m_ref, gpre_ref, TM)

    for src, dst in zip(cast_src, cast_dst):
        dst[...] = src[...].astype(BF16)

    h = h_ref[...]
    chunks = []
    for c in range(TN_IN // MXU_COLS):
        if c % (RH // MXU_COLS) == 0:
            blk = j * (TN_IN // RH) + c // (RH // MXU_COLS)
            rope = _is_rope_block(blk) & (t < LAT_TILES)
            gain = jnp.where(_is_query_block(blk), Q_TO_LOG2, 1.0)
            cos = jnp.where(rope, cos_ref[...], 1.0) * gain
            sina = jnp.where(rope, sina_ref[...], 0.0) * gain
            sinb = jnp.where(rope, sinb_ref[...], 0.0) * gain
        y = _dot(h, w_ref[:, c * MXU_COLS:(c + 1) * MXU_COLS])
        chunks.append(y)
        for hd in range(MXU_COLS // HEAD_DIM):
            v = y[:, hd * HEAD_DIM:(hd + 1) * HEAD_DIM]
            fwd = pltpu.roll(v, HEAD_DIM // 4, 1)
            bwd = pltpu.roll(v, HEAD_DIM - HEAD_DIM // 4, 1)
            col = c * MXU_COLS + hd * HEAD_DIM
            p_ref[:, col:col + HEAD_DIM] = (v * cos + bwd * sina + fwd * sinb).astype(BF16)

    for step in range(IN_WIDTH // TN_IN):
        for c in range(TN_IN // MXU_COLS):
            col = step * TN_IN + c * MXU_COLS
            if COL_U <= col < COL_QD:
                @pl.when(j == step)
                def _pool_channels(c=c, col=col):
                    u_ref[:, col - COL_U:col - COL_U + MXU_COLS] = chunks[c]


def _inproj_call(s, mods, norm_pre, w_in, cos, sina, sinb, layer, cast_next=()):
    n_j = IN_WIDTH // TN_IN
    hosted = [_hosted_cast(arr, lead, n_j) for arr, lead in cast_next]
    rope_spec = pl.BlockSpec((TM, HEAD_DIM), lambda t, j: (t % (SEQ // TM), 0))
    out = pl.pallas_call(
        _inproj_kernel,
        grid=(ALL_TILES, n_j),
        in_specs=[
            pl.BlockSpec((TM, D_MODEL), lambda t, j: (t, 0)),
            _mod_spec(1, TM),
            _norm_spec(layer, 1),
            pl.BlockSpec((D_MODEL, TN_IN), lambda t, j: (0, j)),
            rope_spec, rope_spec, rope_spec,
        ] + [h[0] for h in hosted],
        out_specs=[
            pl.BlockSpec((TM, TN_IN), lambda t, j: (t, j)),
            pl.BlockSpec((TM, POOL_WIDTH), lambda t, j: (t, 0)),
        ] + [h[1] for h in hosted],
        out_shape=[
            jax.ShapeDtypeStruct((N_ALL, IN_WIDTH), BF16),
            jax.ShapeDtypeStruct((N_ALL, POOL_WIDTH), F32),
        ] + [h[2] for h in hosted],
        scratch_shapes=[pltpu.VMEM((TM, D_MODEL), BF16)] + _norm_scratch(TM),
        compiler_params=_params("arbitrary", "arbitrary"),
        name=f"inproj_l{layer}",
    )(s, mods, norm_pre, w_in, cos, sina, sinb, *[arr for arr, _ in cast_next])
    return out[0], out[1], tuple(out[2:])


LAT_QBLOCKS = SEQ // BLOCK
CTX_QBLOCKS = CTX_LEN // BLOCK
BAND = 3 * BLOCK


def _win_kernel(sink_ref, q_ref, k_ref, v_ref, kc_ref, vc_ref, o_ref):
    n = pl.program_id(1)
    row = lax.broadcasted_iota(jnp.int32, (WIN_GROUP * BLOCK, 1), 0)

    def head_slice(i):
        return slice(i * HEAD_DIM, (i + 1) * HEAD_DIM)

    def group_q(h):
        return jnp.concatenate([q_ref[:, head_slice(WIN_GROUP * h + g)] for g in range(WIN_GROUP)], axis=0)

    def group_sink(h):
        s0 = sink_ref[0, WIN_GROUP * h]
        s1 = sink_ref[0, WIN_GROUP * h + 1]
        s2 = sink_ref[0, WIN_GROUP * h + 2]
        return jnp.where(row < BLOCK, s0, jnp.where(row < 2 * BLOCK, s1, s2)) * LOG2E

    def store(h, o):
        for g in range(WIN_GROUP):
            o_ref[:, head_slice(WIN_GROUP * h + g)] = o[g * BLOCK:(g + 1) * BLOCK].astype(BF16)

    @pl.when(n < LAT_QBLOCKS)
    def _latent_queries():
        start = pl.multiple_of(jnp.clip((n - 1) * BLOCK, 0, SEQ - BAND), BLOCK)
        qpos = n * BLOCK + (row & (BLOCK - 1))
        kpos = start + lax.broadcasted_iota(jnp.int32, (1, BAND), 1)
        valid = jnp.abs(kpos - qpos) <= WINDOW
        for h in range(WIN_KV_HEADS):
            hs = head_slice(h)
            q = group_q(h)
            s_loc = jnp.where(valid, _dot_nt(q, k_ref[pl.ds(start, BAND), hs]), -jnp.inf)
            s_ctx = _dot_nt(q, kc_ref[:, hs])
            sink = group_sink(h)
            m = jnp.maximum(jnp.maximum(jnp.max(s_loc, axis=-1, keepdims=True),
                                        jnp.max(s_ctx, axis=-1, keepdims=True)), sink)
            e_loc = jnp.exp2(s_loc - m)
            e_ctx = jnp.exp2(s_ctx - m)
            den = (jnp.sum(e_loc, axis=-1, keepdims=True) + jnp.sum(e_ctx, axis=-1, keepdims=True)
                   + jnp.exp2(sink - m))
            o = _dot(e_loc.astype(BF16), v_ref[pl.ds(start, BAND), hs]) + _dot(e_ctx.astype(BF16), vc_ref[:, hs])
            store(h, o * (1.0 / den))

    @pl.when(n >= LAT_QBLOCKS)
    def _context_queries():
        for h in range(WIN_KV_HEADS):
            hs = head_slice(h)
            s_ctx = _dot_nt(group_q(h), kc_ref[:, hs])
            sink = group_sink(h)
            m = jnp.maximum(jnp.max(s_ctx, axis=-1, keepdims=True), sink)
            e_ctx = jnp.exp2(s_ctx - m)
            den = jnp.sum(e_ctx, axis=-1, keepdims=True) + jnp.exp2(sink - m)
            store(h, _dot(e_ctx.astype(BF16), vc_ref[:, hs]) * (1.0 / den))


def _win_call(p, sink, layer, with_ctx_queries):
    nq = LAT_QBLOCKS + (CTX_QBLOCKS if with_ctx_queries else 0)
    kv_cols = WIN_KV_HEADS * HEAD_DIM

    def q_row(b, n):
        return jnp.where(n < LAT_QBLOCKS, b * LAT_QBLOCKS + n,
                         N_LAT // BLOCK + b * CTX_QBLOCKS + n - LAT_QBLOCKS)

    ctx_row = lambda b: N_LAT // CTX_LEN + b
    return pl.pallas_call(
        _win_kernel,
        grid=(BATCH, nq),
        in_specs=[
            pl.BlockSpec(memory_space=pltpu.SMEM),
            pl.BlockSpec((BLOCK, WIN_WIDTH), lambda b, n: (q_row(b, n), 0)),
            pl.BlockSpec((SEQ, kv_cols), lambda b, n: (b, COL_KA // kv_cols)),
            pl.BlockSpec((SEQ, kv_cols), lambda b, n: (b, COL_VA // kv_cols)),
            pl.BlockSpec((CTX_LEN, kv_cols), lambda b, n: (ctx_row(b), COL_KA // kv_cols)),
            pl.BlockSpec((CTX_LEN, kv_cols), lambda b, n: (ctx_row(b), COL_VA // kv_cols)),
        ],
        out_specs=pl.BlockSpec((BLOCK, WIN_WIDTH), lambda b, n: (q_row(b, n), 0)),
        out_shape=jax.ShapeDtypeStruct((N_ALL if with_ctx_queries else N_LAT, WIN_WIDTH), BF16),
        compiler_params=_params("arbitrary", "arbitrary"),
        name=f"win_attn_l{layer}",
    )(sink[layer].reshape(1, WIN_Q_HEADS), p, p, p, p, p)


LAT_QTILES = SEQ // TQ_DIFF


def _diff_kernel(*refs, lam_init, with_ctx_queries, with_ada):
    lam_ref, subln_ref, q_ref, k_ref, v_ref, kc_ref, vc_ref = refs[:7]
    rest = list(refs[7:])
    s_ref = rest.pop(-1)
    qc_ref = rest.pop(0) if with_ctx_queries else None
    ada_in = [rest.pop(0) for _ in range(3)] if with_ada else None
    o_ref = rest.pop(0)
    oc_ref = rest.pop(0) if with_ctx_queries else None
    i = pl.program_id(2)
    if with_ada:
        _ada_kernel(*ada_in, rest.pop(0))
    dl = lam_ref[...]
    lam = (jnp.exp(jnp.sum(dl[0:1] * dl[1:2], axis=-1, keepdims=True))
           - jnp.exp(jnp.sum(dl[2:3] * dl[3:4], axis=-1, keepdims=True)) + lam_init)
    out_gain = subln_ref[...] * (1.0 - lam_init)

    def head_cols(j):
        return slice(j * HEAD_DIM, (j + 1) * HEAD_DIM)

    def mix(logits):
        maps = []
        for s in logits:
            e = jnp.exp2(s - jnp.max(s, axis=-1, keepdims=True))
            maps.append((e, 1.0 / jnp.sum(e, axis=-1, keepdims=True)))
        (e1, inv1), (e2, inv2) = maps
        return (e1 * inv1 - e2 * (lam * inv2)).astype(BF16)

    def sub_norm(o):
        return _rms(o, out_gain).astype(BF16)

    n_sub = TQ_DIFF // TQ_SUB

    def logits_to_slot(k, slot):
        rows = slice(k * TQ_SUB, (k + 1) * TQ_SUB)
        for j in range(2):
            q = q_ref[rows, head_cols(j)]
            s_ref[slot, j, :, 0:SEQ] = _dot_nt(q, k_ref[:, head_cols(j)])
            s_ref[slot, j, :, SEQ:] = _dot_nt(q, kc_ref[:, head_cols(j)])

    logits_to_slot(0, 0)
    for k in range(n_sub):
        slot = k % 2
        if k + 1 < n_sub:
            logits_to_slot(k + 1, 1 - slot)
        a = mix([s_ref[slot, 0], s_ref[slot, 1]])
        o = _dot(a[:, 0:SEQ], v_ref[...]) + _dot(a[:, SEQ:], vc_ref[...])
        o_ref[k * TQ_SUB:(k + 1) * TQ_SUB, :] = sub_norm(o)

    if with_ctx_queries:
        @pl.when(i == LAT_QTILES - 1)
        def _context_queries():
            q = qc_ref[...]
            a = mix([_dot_nt(q[:, head_cols(j)], kc_ref[:, head_cols(j)]) for j in range(2)])
            oc_ref[...] = sub_norm(_dot(a, vc_ref[...]))


def _diff_call(p, diff_lambda, diff_subln, layer, with_ctx_queries, ada_next=None):
    lam_init = 0.8 - 0.6 * math.exp(-0.3 * layer)
    w = DIFF_V_DIM
    ctx_row = lambda b: N_LAT // CTX_LEN + b
    n_steps = BATCH * DIFF_HEADS * LAT_QTILES
    step = lambda b, h, i: (b * DIFF_HEADS + h) * LAT_QTILES + i
    in_specs = [
        pl.BlockSpec((None, 4, HEAD_DIM), lambda b, h, i: (layer, 0, 0)),
        pl.BlockSpec((None, 1, DIFF_V_DIM), lambda b, h, i: (layer, 0, 0)),
        pl.BlockSpec((TQ_DIFF, w), lambda b, h, i: (b * LAT_QTILES + i, COL_QD // w + h)),
        pl.BlockSpec((SEQ, w), lambda b, h, i: (b, COL_KD // w + h)),
        pl.BlockSpec((SEQ, w), lambda b, h, i: (b, COL_VD // w + h)),
        pl.BlockSpec((CTX_LEN, w), lambda b, h, i: (ctx_row(b), COL_KD // w + h)),
        pl.BlockSpec((CTX_LEN, w), lambda b, h, i: (ctx_row(b), COL_VD // w + h)),
    ]
    out_specs = [pl.BlockSpec((TQ_DIFF, w), lambda b, h, i: (b * LAT_QTILES + i, h))]
    out_shape = [jax.ShapeDtypeStruct((N_LAT, DIFF_WIDTH), BF16)]
    operands = [diff_lambda, diff_subln.reshape(DEPTH, 1, DIFF_V_DIM), p, p, p, p, p]
    if with_ctx_queries:
        in_specs.append(pl.BlockSpec((CTX_LEN, w), lambda b, h, i: (ctx_row(b), COL_QD // w + h)))
        out_specs.append(pl.BlockSpec((CTX_LEN, w), lambda b, h, i: (b, h)))
        out_shape.append(jax.ShapeDtypeStruct((N_CTX, DIFF_WIDTH), BF16))
        operands.append(p)
    if ada_next is not None:
        cc, w_ada, b_ada, ada_layer = ada_next
        tn = ADA_WIDTH // n_steps
        in_specs += [
            pl.BlockSpec((MOD_ROWS, D_MODEL), lambda b, h, i: (0, 0)),
            pl.BlockSpec((None, D_MODEL, tn), lambda b, h, i: (ada_layer, 0, step(b, h, i))),
            pl.BlockSpec((None, 1, tn), lambda b, h, i: (ada_layer, 0, step(b, h, i))),
        ]
        out_specs.append(pl.BlockSpec((MOD_ROWS, tn), lambda b, h, i: (0, step(b, h, i))))
        out_shape.append(jax.ShapeDtypeStruct((MOD_ROWS, ADA_WIDTH), F32))
        operands += [cc, w_ada, b_ada.reshape(DEPTH, 1, ADA_WIDTH)]
    out = pl.pallas_call(
        functools.partial(_diff_kernel, lam_init=lam_init, with_ctx_queries=with_ctx_queries,
                          with_ada=ada_next is not None),
        grid=(BATCH, DIFF_HEADS, LAT_QTILES),
        in_specs=in_specs,
        out_specs=out_specs,
        out_shape=out_shape,
        scratch_shapes=[pltpu.VMEM((2, 2, TQ_SUB, SEQ + CTX_LEN), F32)],
        compiler_params=_params("arbitrary", "arbitrary", "arbitrary"),
        name=f"diff_attn_l{layer}",
    )(*operands)
    return (out[:-1], out[-1]) if ada_next is not None else (out, None)


def _pool_kernel(u_ref, w_ref, scale_ref, o_ref, pad_ref, *, n_tok):
    t = lax.broadcasted_iota(jnp.int32, (n_tok, 1), 0)
    zeros = jnp.zeros((POOL_HALO, POOL_GROUP), F32)
    pad_ref[0:POOL_HALO, :] = zeros
    pad_ref[POOL_HALO + n_tok:, :] = zeros
    for g, w in enumerate(POOL_WINDOWS):
        cs = slice(g * POOL_GROUP, (g + 1) * POOL_GROUP)
        u = u_ref[:, cs]
        pad_ref[POOL_HALO:POOL_HALO + n_tok, :] = u
        total = pad_ref[POOL_HALO - w // 2:POOL_HALO - w // 2 + n_tok, :]
        for d in range(1 - w // 2, w // 2):
            total = total + pad_ref[POOL_HALO + d:POOL_HALO + d + n_tok, :]
        count = (jnp.clip(t + w // 2, 0, n_tok) - jnp.clip(t - w // 2, 0, n_tok)).astype(F32)
        centred = (total / count - u).astype(BF16)
        o_ref[:, cs] = (_dot(centred, w_ref[g]) * scale_ref[:, cs]).astype(BF16)


def _pool_call(u, pool_w, pool_scale, layer, n_tok, row_block0):
    return pl.pallas_call(
        functools.partial(_pool_kernel, n_tok=n_tok),
        grid=(BATCH,),
        in_specs=[
            pl.BlockSpec((n_tok, POOL_WIDTH), lambda b: (row_block0 + b, 0)),
            pl.BlockSpec((None, len(POOL_WINDOWS), POOL_GROUP, POOL_GROUP), lambda b: (layer, 0, 0, 0)),
            pl.BlockSpec((None, 1, POOL_WIDTH), lambda b: (layer, 0, 0)),
        ],
        out_specs=pl.BlockSpec((n_tok, POOL_WIDTH), lambda b: (b, 0)),
        out_shape=jax.ShapeDtypeStruct((BATCH * n_tok, POOL_WIDTH), BF16),
        scratch_shapes=[pltpu.VMEM((n_tok + 2 * POOL_HALO, POOL_GROUP), F32)],
        compiler_params=_params("arbitrary"),
        name=f"pool_l{layer}_n{n_tok}",
    )(u, pool_w, pool_scale.reshape(DEPTH, 1, POOL_WIDTH))


def _outproj_kernel(s_ref, m_ref, gpost_ref, oa_ref, ob_ref, oc_ref, w_ref, o_ref, stat_ref, vec_ref):
    oa, ob, oc = oa_ref[...], ob_ref[...], oc_ref[...]
    b0, c0 = WIN_WIDTH, WIN_WIDTH + POOL_WIDTH
    for n in range(D_MODEL // 1024):
        cs = slice(n * 1024, (n + 1) * 1024)
        o_ref[:, cs] = _dot(oa, w_ref[0:b0, cs]) + _dot(ob, w_ref[b0:c0, cs]) + _dot(oc, w_ref[c0:, cs])
    _gated_residual_to(o_ref, s_ref, o_ref, stat_ref, vec_ref, m_ref[2] * gpost_ref[...], TM_OUT)


def _outproj_call(s, mods, norm_post, o_a, o_b, o_c, w_out, layer, n_rows):
    row_spec = lambda width: pl.BlockSpec((TM_OUT, width), lambda t: (t, 0))
    return pl.pallas_call(
        _outproj_kernel,
        grid=(n_rows // TM_OUT,),
        in_specs=[
            row_spec(D_MODEL),
            _mod_spec(1, TM_OUT),
            _norm_spec(layer, 1),
            row_spec(WIN_WIDTH), row_spec(POOL_WIDTH), row_spec(DIFF_WIDTH),
            pl.BlockSpec((D_MODEL, D_MODEL), lambda t: (0, 0), pipeline_mode=pl.Buffered(1)),
        ],
        out_specs=row_spec(D_MODEL),
        out_shape=jax.ShapeDtypeStruct((n_rows, D_MODEL), F32),
        scratch_shapes=_norm_scratch(TM_OUT),
        compiler_params=_params("arbitrary"),
        name=f"outproj_l{layer}",
    )(s, mods, norm_post, o_a, o_b, o_c, w_out)


def _rope_tables():
    row = jnp.repeat(jnp.arange(SEQ // GRID_W, dtype=F32), GRID_W)
    col = jnp.tile(jnp.arange(GRID_W, dtype=F32), SEQ // GRID_W)
    half = HEAD_DIM // 2
    inv = ROPE_THETA ** (-jnp.arange(0, half, 2, dtype=F32) / half)
    ar = row[:, None] * inv
    ac = col[:, None] * inv
    ang = jnp.concatenate([ar, ar, ac, ac], axis=-1)
    cos, sin = jnp.cos(ang), jnp.sin(ang)
    low = (jnp.arange(HEAD_DIM) % half) < half // 2
    return cos, jnp.where(low, -sin, 0.0), jnp.where(low, 0.0, sin)


def kernel(x, c, ctx, c_ctx, w_ada, b_ada, norm_pre, norm_post, ffn_w_gate, ffn_w_up, ffn_w_down,
           w_in, w_out, attn_sink, pool_w, pool_scale, diff_lambda, diff_subln):
    srcs = (x.reshape(N_LAT, D_MODEL), ctx.reshape(N_CTX, D_MODEL))
    cc = jnp.zeros((MOD_ROWS, D_MODEL), F32).at[:BATCH].set(c).at[BATCH].set(c_ctx)
    mods_shape = (MOD_ROWS, 3, 3, 1, D_MODEL)
    mods = _ada_call(cc, w_ada, b_ada, 0).reshape(mods_shape)
    npre = norm_pre.reshape(DEPTH, 3, 1, D_MODEL)
    npost = norm_post.reshape(DEPTH, 3, 1, D_MODEL)
    pool_w16 = pool_w.astype(BF16)
    cos, sina, sinb = _rope_tables()

    ffn_f32 = (ffn_w_gate, ffn_w_up, ffn_w_down)
    ffn_order = [(layer, half) for layer in range(DEPTH) for half in range(2)]
    ffn16 = tuple(w[0, 0].astype(BF16) for w in ffn_f32)
    w_in16, w_out16 = w_in[0].astype(BF16), w_out[0].astype(BF16)

    def next_ffn(layer, half):
        i = ffn_order.index((layer, half)) + 1
        return [(w, ffn_order[i]) for w in ffn_f32] if i < len(ffn_order) else []

    for layer in range(DEPTH):
        update_ctx = layer < DEPTH - 1
        n_tiles = ALL_TILES if update_ctx else LAT_TILES
        s, ffn16 = _ffn_call(srcs, mods, npre, npost, ffn16, layer, 0, ALL_TILES, next_ffn(layer, 0))
        proj_next = [(w_in, (layer + 1,)), (w_out, (layer + 1,))] if update_ctx else []
        p, u, proj16 = _inproj_call(s, mods, npre, w_in16, cos, sina, sinb, layer, proj_next)
        o_a = _win_call(p, attn_sink, layer, update_ctx)
        ada_next = (cc, w_ada, b_ada, layer + 1) if update_ctx else None
        o_c, mods_next = _diff_call(p, diff_lambda, diff_subln, layer, update_ctx, ada_next)
        o_b =[_pool_call(u, pool_w16, pool_scale, layer, SEQ, 0)]
        if update_ctx:
            o_b.append(_pool_call(u, pool_w16, pool_scale, layer, CTX_LEN, N_LAT // CTX_LEN))
        o_b, o_c = jnp.concatenate(o_b, axis=0), jnp.concatenate(o_c, axis=0)
        s = _outproj_call(s, mods, npost, o_a, o_b, o_c, w_out16, layer, n_tiles * TM)
        s, ffn16 = _ffn_call((s,), mods, npre, npost, ffn16, layer, 1, n_tiles, next_ffn(layer, 1))
        srcs = (s,)
        if update_ctx:
            w_in16, w_out16 = proj16
            mods = mods_next.reshape(mods_shape)
    return s.reshape(BATCH, SEQ, D_MODEL)
```

```python
import functools
import math

import jax
import jax.numpy as jnp
from jax import lax
from jax.experimental import pallas as pl
from jax.experimental.pallas import tpu as pltpu

D_MODEL = 4096
BATCH = 4
SEQ = 2048
DEPTH = 2
GRID_W = 64
CTX_LEN = 256
HEAD_DIM = 128
ROPE_THETA = 10000.0
EPS = 1e-6
N_MOD = 9
FFN_DIM = 2 * D_MODEL

WIN_Q_HEADS = 12
WIN_KV_HEADS = 4
WIN_GROUP = WIN_Q_HEADS // WIN_KV_HEADS
WIN_WIDTH = WIN_Q_HEADS * HEAD_DIM
WINDOW = 128
BLOCK = 128

POOL_WINDOWS = (2, 4, 8, 16)
POOL_WIDTH = 1024
POOL_GROUP = POOL_WIDTH // len(POOL_WINDOWS)
POOL_HALO = 8

DIFF_HEADS = 6
DIFF_V_DIM = 2 * HEAD_DIM
DIFF_WIDTH = DIFF_HEADS * DIFF_V_DIM
IN_WIDTH = 8192

COL_QA = 0
COL_KA = COL_QA + WIN_WIDTH
COL_VA = COL_KA + WIN_KV_HEADS * HEAD_DIM
COL_U = COL_VA + WIN_KV_HEADS * HEAD_DIM
COL_QD = COL_U + POOL_WIDTH
COL_KD = COL_QD + 2 * DIFF_HEADS * HEAD_DIM
COL_VD = COL_KD + 2 * DIFF_HEADS * HEAD_DIM

F32 = jnp.float32
BF16 = jnp.bfloat16
LANES = 128
SUBLANES = 8
LOG2E = math.log2(math.e)
Q_TO_LOG2 = HEAD_DIM ** -0.5 * LOG2E
ADA_WIDTH = N_MOD * D_MODEL

N_LAT = BATCH * SEQ
N_CTX = BATCH * CTX_LEN
N_ALL = N_LAT + N_CTX

TM = 512
TM_OUT = 256
TF = 512
TN_IN = 1024
RH = 512
MXU_COLS = 256
TN_ADA = 512
TQ_DIFF = 1024
TQ_SUB = 256
ROW_CHUNK = 16
MOD_ROWS = 16
VMEM_LIMIT = 63 * 1024 * 1024

LAT_TILES = N_LAT // TM
ALL_TILES = N_ALL // TM


def _mod_row(t, tm):
    return jnp.where(t < N_LAT // tm, t // (SEQ // tm), BATCH)


def _dot(a, b):
    return jnp.dot(a, b, preferred_element_type=F32)


def _dot_nt(a, b):
    return lax.dot_general(a, b, (((1,), (1,)), ((), ())), preferred_element_type=F32)


def _for_rows(n_rows, chunk, fn, unroll=1):
    def body(i, carry):
        fn(pl.ds(pl.multiple_of(i * chunk, chunk), chunk))
        return carry
    lax.fori_loop(0, n_rows // chunk, body, 0, unroll=unroll)


def _rstd_pass(load_rows, stat_ref, n_rows):
    def rows(r):
        v = load_rows(r)
        sq = v * v
        parts = [sq[:, k * LANES:(k + 1) * LANES] for k in range(D_MODEL // LANES)]
        while len(parts) > 1:
            parts = [parts[k] + parts[k + 1] for k in range(0, len(parts), 2)]
        total = jnp.sum(parts[0], axis=-1, keepdims=True)
        rstd = lax.rsqrt(total * (1.0 / D_MODEL) + EPS)
        stat_ref[r, :] = jnp.broadcast_to(rstd, (SUBLANES, LANES))
    _for_rows(n_rows, SUBLANES, rows, unroll=16)


def _rms(v, g):
    return v * lax.rsqrt(jnp.mean(v * v, axis=-1, keepdims=True) + EPS) * g


def _params(*sem):
    return pltpu.CompilerParams(dimension_semantics=sem, vmem_limit_bytes=VMEM_LIMIT)


def _ada_kernel(c_ref, w_ref, b_ref, o_ref):
    c = c_ref[...]
    s = (c / (1.0 + jnp.exp(-c))).astype(BF16)
    o_ref[...] = _dot(s, w_ref[...].astype(BF16)) + b_ref[...]


def _ada_call(cc, w_ada, b_ada, layer):
    return pl.pallas_call(
        _ada_kernel,
        grid=(ADA_WIDTH // TN_ADA,),
        in_specs=[
            pl.BlockSpec((MOD_ROWS, D_MODEL), lambda j: (0, 0)),
            pl.BlockSpec((None, D_MODEL, TN_ADA), lambda j: (layer, 0, j)),
            pl.BlockSpec((None, 1, TN_ADA), lambda j: (layer, 0, j)),
        ],
        out_specs=pl.BlockSpec((MOD_ROWS, TN_ADA), lambda j: (0, j)),
        out_shape=jax.ShapeDtypeStruct((MOD_ROWS, ADA_WIDTH), F32),
        compiler_params=_params("arbitrary"),
        name=f"adaln_l{layer}",
    )(cc, w_ada, b_ada.reshape(DEPTH, 1, ADA_WIDTH))


def _mod_spec(group, tm):
    return pl.BlockSpec((None, None, 3, 1, D_MODEL), lambda t, *_: (_mod_row(t, tm), group, 0, 0, 0))


def _norm_spec(layer, k):
    return pl.BlockSpec((None, None, 1, D_MODEL), lambda t, *_: (layer, k, 0, 0))


def _modulated_norm_to(h_ref, s_ref, stat_ref, vec_ref, m_ref, gpre_ref, n_rows, also=None):
    _rstd_pass(lambda r: s_ref[r, :], stat_ref, n_rows)
    full = (ROW_CHUNK, D_MODEL)
    vec_ref[0:ROW_CHUNK, :] = jnp.broadcast_to(gpre_ref[...] * (1.0 + m_ref[1]), full)
    vec_ref[ROW_CHUNK:, :] = jnp.broadcast_to(m_ref[0], full)

    def rows(r):
        rstd = stat_ref[r, :]
        for k in range(D_MODEL // LANES):
            cs = slice(k * LANES, (k + 1) * LANES)
            h = s_ref[r, cs] * rstd * vec_ref[0:ROW_CHUNK, cs] + vec_ref[ROW_CHUNK:, cs]
            h_ref[r, cs] = h.astype(BF16)
        if also is not None:
            also(r)
    _for_rows(n_rows, ROW_CHUNK, rows, unroll=2)


def _gated_residual_to(o_ref, s_ref, y_ref, stat_ref, vec_ref, gate_gain, n_rows):
    _rstd_pass(lambda r: y_ref[r, :], stat_ref, n_rows)
    vec_ref[0:ROW_CHUNK, :] = jnp.broadcast_to(gate_gain, (ROW_CHUNK, D_MODEL))

    def rows(r):
        rstd = stat_ref[r, :]
        for k in range(D_MODEL // LANES):
            cs = slice(k * LANES, (k + 1) * LANES)
            o_ref[r, cs] = s_ref[r, cs] + y_ref[r, cs] * rstd * vec_ref[0:ROW_CHUNK, cs]
    _for_rows(n_rows, ROW_CHUNK, rows, unroll=2)


def _norm_scratch(n_rows):
    return [pltpu.VMEM((n_rows, LANES), F32), pltpu.VMEM((2 * ROW_CHUNK, D_MODEL), F32)]


def _hosted_cast(arr, lead, n_inner):
    rows, cols = arr.shape[-2:]
    block = (rows // LAT_TILES, cols // n_inner)

    def index(t, i):
        return jnp.minimum(t, LAT_TILES - 1), jnp.where(t < LAT_TILES, i, n_inner - 1)

    in_spec = pl.BlockSpec((None,) * len(lead) + block, lambda t, i: lead + index(t, i))
    return in_spec, pl.BlockSpec(block, index), jax.ShapeDtypeStruct((rows, cols), BF16)


def _ffn_kernel(*refs, n_src, n_cast, n_tiles):
    srcs, refs = refs[:n_src], refs[n_src:]
    m_ref, gpre_ref, gpost_ref, wg_ref, wu_ref, wd_ref = refs[:6]
    cast_src, o_hbm, cast_dst = refs[6:6 + n_cast], refs[6 + n_cast], refs[7 + n_cast:7 + 2 * n_cast]
    slots, acc_ref, h_ref, stat_ref, vec_ref, fetch_sem, store_sem = refs[7 + 2 * n_cast:]
    t = pl.program_id(0)
    f = pl.program_id(1)
    cur = slots.at[lax.rem(t, 2)]

    def tile_rows(ref, tile):
        return ref.at[pl.ds(pl.multiple_of(tile * TM, TM), TM), :]

    def fetch(tile, start):
        slot = lax.rem(tile, 2)

        def run(src, src_tile):
            copy = pltpu.make_async_copy(tile_rows(src, src_tile), slots.at[slot], fetch_sem.at[slot])
            copy.start() if start else copy.wait()
        if n_src == 1:
            run(srcs[0], tile)
        else:
            pl.when(tile < LAT_TILES)(lambda: run(srcs[0], tile))
            pl.when(tile >= LAT_TILES)(lambda: run(srcs[1], tile - LAT_TILES))

    def store(tile):
        slot = lax.rem(tile, 2)
        return pltpu.make_async_copy(slots.at[slot], tile_rows(o_hbm, tile), store_sem.at[slot])

    @pl.when((t == 0) & (f == 0))
    def _first_fetch():
        fetch(t, start=True)

    @pl.when(f == 0)
    def _prologue():
        fetch(t, start=False)

        def zero_acc(r):
            acc_ref[r, :] = jnp.zeros((ROW_CHUNK, D_MODEL), F32)
        _modulated_norm_to(h_ref, cur, stat_ref, vec_ref, m_ref, gpre_ref, TM, also=zero_acc)

    @pl.when(f == 1)
    def _refill_other_slot():
        @pl.when(t >= 1)
        def _():
            store(t - 1).wait()

        @pl.when(t + 1 < n_tiles)
        def _():
            fetch(t + 1, start=True)

    h = h_ref[...]
    g = _dot(h, wg_ref[...])
    u = _dot(h, wu_ref[...])
    a = (g / (1.0 + jnp.exp(-g)) * u).astype(BF16)
    for c in range(D_MODEL // 1024):
        cs = slice(c * 1024, (c + 1) * 1024)
        acc_ref[:, cs] += _dot(a, wd_ref[:, cs])
    for src, dst in zip(cast_src, cast_dst):
        dst[...] = src[...].astype(BF16)

    @pl.when(f == pl.num_programs(1) - 1)
    def _epilogue():
        _gated_residual_to(cur, cur, acc_ref, stat_ref, vec_ref, 0.5 * m_ref[2] * gpost_ref[...], TM)
        store(t).start()

        @pl.when(t == n_tiles - 1)
        def _():
            store(t).wait()


def _ffn_call(srcs, mods, norm_pre, norm_post, weights, layer, half, n_tiles, cast_next=()):
    k = 2 * half
    n_f = FFN_DIM // TF
    assert n_f >= 2 and n_tiles >= 2
    hosted = [_hosted_cast(arr, lead, n_f) for arr, lead in cast_next]
    hbm = pl.BlockSpec(memory_space=pl.ANY)
    out = pl.pallas_call(
        functools.partial(_ffn_kernel, n_src=len(srcs), n_cast=len(hosted), n_tiles=n_tiles),
        grid=(n_tiles, n_f),
        in_specs=[hbm] * len(srcs) + [
            _mod_spec(k, TM),
            _norm_spec(layer, k),
            _norm_spec(layer, k),
            pl.BlockSpec((D_MODEL, TF), lambda t, f: (0, f)),
            pl.BlockSpec((D_MODEL, TF), lambda t, f: (0, f)),
            pl.BlockSpec((TF, D_MODEL), lambda t, f: (f, 0)),
        ] + [h[0] for h in hosted],
        out_specs=[hbm] + [h[1] for h in hosted],
        out_shape=[jax.ShapeDtypeStruct((n_tiles * TM, D_MODEL), F32)] + [h[2] for h in hosted],
        scratch_shapes=[
            pltpu.VMEM((2, TM, D_MODEL), F32),
            pltpu.VMEM((TM, D_MODEL), F32),
            pltpu.VMEM((TM, D_MODEL), BF16),
        ] + _norm_scratch(TM) + [pltpu.SemaphoreType.DMA((2,)), pltpu.SemaphoreType.DMA((2,))],
        compiler_params=_params("arbitrary", "arbitrary"),
        name=f"ffn_l{layer}h{half}",
    )(*srcs, mods, norm_pre, norm_post, *weights, *[arr for arr, _ in cast_next])
    return out[0], tuple(out[1:])


def _is_rope_block(jj):
    blk = lambda col: col // RH
    return (jj < blk(COL_VA)) | ((jj >= blk(COL_QD)) & (jj < blk(COL_VD)))


def _is_query_block(jj):
    blk = lambda col: col // RH
    return (jj < blk(COL_KA)) | ((jj >= blk(COL_QD)) & (jj < blk(COL_KD)))


def _inproj_kernel(s_ref, m_ref, gpre_ref, w_ref, cos_ref, sina_ref, sinb_ref, *rest):
    n_cast = (len(rest) - 5) // 2
    cast_src, (p_ref, u_ref) = rest[:n_cast], rest[n_cast:n_cast + 2]
    cast_dst = rest[n_cast + 2:2 * n_cast + 2]
    h_ref, stat_ref, vec_ref = rest[2 * n_cast + 2:]
    t = pl.program_id(0)
    j = pl.program_id(1)

    @pl.when(j == 0)
    def _prologue():
        _modulated_norm_to(h_ref, s_ref, stat_ref, vec_ref, m_ref, gpre_ref, TM)

    for src, dst in zip(cast_src, cast_dst):
        dst[...] = src[...].astype(BF16)

    h = h_ref[...]
    chunks = []
    for c in range(TN_IN // MXU_COLS):
        if c % (RH // MXU_COLS) == 0:
            blk = j * (TN_IN // RH) + c // (RH // MXU_COLS)
            rope = _is_rope_block(blk) & (t < LAT_TILES)
            gain = jnp.where(_is_query_block(blk), Q_TO_LOG2, 1.0)
            cos = jnp.where(rope, cos_ref[...], 1.0) * gain
            sina = jnp.where(rope, sina_ref[...], 0.0) * gain
            sinb = jnp.where(rope, sinb_ref[...], 0.0) * gain
        y = _dot(h, w_ref[:, c * MXU_COLS:(c + 1) * MXU_COLS])
        chunks.append(y)
        for hd in range(MXU_COLS // HEAD_DIM):
            v = y[:, hd * HEAD_DIM:(hd + 1) * HEAD_DIM]
            fwd = pltpu.roll(v, HEAD_DIM // 4, 1)
            bwd = pltpu.roll(v, HEAD_DIM - HEAD_DIM // 4, 1)
            col = c * MXU_COLS + hd * HEAD_DIM
            p_ref[:, col:col + HEAD_DIM] = (v * cos + bwd * sina + fwd * sinb).astype(BF16)

    for step in range(IN_WIDTH // TN_IN):
        for c in range(TN_IN // MXU_COLS):
            col = step * TN_IN + c * MXU_COLS
            if COL_U <= col < COL_QD:
                @pl.when(j == step)
                def _pool_channels(c=c, col=col):
                    u_ref[:, col - COL_U:col - COL_U + MXU_COLS] = chunks[c]


def _inproj_call(s, mods, norm_pre, w_in, cos, sina, sinb, layer, cast_next=()):
    n_j = IN_WIDTH // TN_IN
    hosted = [_hosted_cast(arr, lead, n_j) for arr, lead in cast_next]
    rope_spec = pl.BlockSpec((TM, HEAD_DIM), lambda t, j: (t % (SEQ // TM), 0))
    out = pl.pallas_call(
        _inproj_kernel,
        grid=(ALL_TILES, n_j),
        in_specs=[
            pl.BlockSpec((TM, D_MODEL), lambda t, j: (t, 0)),
            _mod_spec(1, TM),
            _norm_spec(layer, 1),
            pl.BlockSpec((D_MODEL, TN_IN), lambda t, j: (0, j)),
            rope_spec, rope_spec, rope_spec,
        ] + [h[0] for h in hosted],
        out_specs=[
            pl.BlockSpec((TM, TN_IN), lambda t, j: (t, j)),
            pl.BlockSpec((TM, POOL_WIDTH), lambda t, j: (t, 0)),
        ] + [h[1] for h in hosted],
        out_shape=[
            jax.ShapeDtypeStruct((N_ALL, IN_WIDTH), BF16),
            jax.ShapeDtypeStruct((N_ALL, POOL_WIDTH), F32),
        ] + [h[2] for h in hosted],
        scratch_shapes=[pltpu.VMEM((TM, D_MODEL), BF16)] + _norm_scratch(TM),
        compiler_params=_params("arbitrary", "arbitrary"),
        name=f"inproj_l{layer}",
    )(s, mods, norm_pre, w_in, cos, sina, sinb, *[arr for arr, _ in cast_next])
    return out[0], out[1], tuple(out[2:])


LAT_QBLOCKS = SEQ // BLOCK
CTX_QBLOCKS = CTX_LEN // BLOCK
BAND = 3 * BLOCK


def _win_kernel(sink_ref, q_ref, k_ref, v_ref, kc_ref, vc_ref, o_ref, s_ref):
    n = pl.program_id(1)
    row = lax.broadcasted_iota(jnp.int32, (WIN_GROUP * BLOCK, 1), 0)

    def head_slice(i):
        return slice(i * HEAD_DIM, (i + 1) * HEAD_DIM)

    def group_q(h):
        return jnp.concatenate([q_ref[:, head_slice(WIN_GROUP * h + g)] for g in range(WIN_GROUP)], axis=0)

    def group_sink(h):
        s0 = sink_ref[0, WIN_GROUP * h]
        s1 = sink_ref[0, WIN_GROUP * h + 1]
        s2 = sink_ref[0, WIN_GROUP * h + 2]
        return jnp.where(row < BLOCK, s0, jnp.where(row < 2 * BLOCK, s1, s2)) * LOG2E

    def store(h, o):
        for g in range(WIN_GROUP):
            o_ref[:, head_slice(WIN_GROUP * h + g)] = o[g * BLOCK:(g + 1) * BLOCK].astype(BF16)

    @pl.when(n < LAT_QBLOCKS)
    def _latent_queries():
        start = pl.multiple_of(jnp.clip((n - 1) * BLOCK, 0, SEQ - BAND), BLOCK)
        qpos = n * BLOCK + (row & (BLOCK - 1))
        kpos = start + lax.broadcasted_iota(jnp.int32, (1, BAND), 1)
        valid = jnp.abs(kpos - qpos) <= WINDOW

        def logits_to_slot(h, slot):
            q = group_q(h)
            s_ref[slot, :, 0:BAND] = jnp.where(valid, _dot_nt(q, k_ref[pl.ds(start, BAND), head_slice(h)]),
                                               -jnp.inf)
            s_ref[slot, :, BAND:] = _dot_nt(q, kc_ref[:, head_slice(h)])

        logits_to_slot(0, 0)
        for h in range(WIN_KV_HEADS):
            hs = head_slice(h)
            slot = h % 2
            if h + 1 < WIN_KV_HEADS:
                logits_to_slot(h + 1, 1 - slot)
            s = s_ref[slot]
            sink = group_sink(h)
            m = jnp.maximum(jnp.max(s, axis=-1, keepdims=True), sink)
            e = jnp.exp2(s - m)
            den = jnp.sum(e, axis=-1, keepdims=True) + jnp.exp2(sink - m)
            e = e.astype(BF16)
            o = _dot(e[:, 0:BAND], v_ref[pl.ds(start, BAND), hs]) + _dot(e[:, BAND:], vc_ref[:, hs])
            store(h, o * (1.0 / den))

    @pl.when(n >= LAT_QBLOCKS)
    def _context_queries():
        for h in range(WIN_KV_HEADS):
            hs = head_slice(h)
            s_ctx = _dot_nt(group_q(h), kc_ref[:, hs])
            sink = group_sink(h)
            m = jnp.maximum(jnp.max(s_ctx, axis=-1, keepdims=True), sink)
            e_ctx = jnp.exp2(s_ctx - m)
            den = jnp.sum(e_ctx, axis=-1, keepdims=True) + jnp.exp2(sink - m)
            store(h, _dot(e_ctx.astype(BF16), vc_ref[:, hs]) * (1.0 / den))


def _win_call(p, sink, layer, with_ctx_queries):
    nq = LAT_QBLOCKS + (CTX_QBLOCKS if with_ctx_queries else 0)
    kv_cols = WIN_KV_HEADS * HEAD_DIM

    def q_row(b, n):
        return jnp.where(n < LAT_QBLOCKS, b * LAT_QBLOCKS + n,
                         N_LAT // BLOCK + b * CTX_QBLOCKS + n - LAT_QBLOCKS)

    ctx_row = lambda b: N_LAT // CTX_LEN + b
    return pl.pallas_call(
        _win_kernel,
        grid=(BATCH, nq),
        in_specs=[
            pl.BlockSpec(memory_space=pltpu.SMEM),
            pl.BlockSpec((BLOCK, WIN_WIDTH), lambda b, n: (q_row(b, n), 0)),
            pl.BlockSpec((SEQ, kv_cols), lambda b, n: (b, COL_KA // kv_cols)),
            pl.BlockSpec((SEQ, kv_cols), lambda b, n: (b, COL_VA // kv_cols)),
            pl.BlockSpec((CTX_LEN, kv_cols), lambda b, n: (ctx_row(b), COL_KA // kv_cols)),
            pl.BlockSpec((CTX_LEN, kv_cols), lambda b, n: (ctx_row(b), COL_VA // kv_cols)),
        ],
        out_specs=pl.BlockSpec((BLOCK, WIN_WIDTH), lambda b, n: (q_row(b, n), 0)),
        out_shape=jax.ShapeDtypeStruct((N_ALL if with_ctx_queries else N_LAT, WIN_WIDTH), BF16),
        scratch_shapes=[pltpu.VMEM((2, WIN_GROUP * BLOCK, BAND + CTX_LEN), F32)],
        compiler_params=_params("arbitrary", "arbitrary"),
        name=f"win_attn_l{layer}",
    )(sink[layer].reshape(1, WIN_Q_HEADS), p, p, p, p, p)


LAT_QTILES = SEQ // TQ_DIFF


def _diff_kernel(*refs, lam_init, with_ctx_queries, with_ada):
    lam_ref, subln_ref, q_ref, k_ref, v_ref, kc_ref, vc_ref = refs[:7]
    rest = list(refs[7:])
    s_ref = rest.pop(-1)
    qc_ref = rest.pop(0) if with_ctx_queries else None
    ada_in = [rest.pop(0) for _ in range(3)] if with_ada else None
    o_ref = rest.pop(0)
    oc_ref = rest.pop(0) if with_ctx_queries else None
    i = pl.program_id(2)
    if with_ada:
        _ada_kernel(*ada_in, rest.pop(0))
    dl = lam_ref[...]
    lam = (jnp.exp(jnp.sum(dl[0:1] * dl[1:2], axis=-1, keepdims=True))
           - jnp.exp(jnp.sum(dl[2:3] * dl[3:4], axis=-1, keepdims=True)) + lam_init)
    out_gain = subln_ref[...] * (1.0 - lam_init)

    def head_cols(j):
        return slice(j * HEAD_DIM, (j + 1) * HEAD_DIM)

    def mix(logits):
        maps = []
        for s in logits:
            e = jnp.exp2(s - jnp.max(s, axis=-1, keepdims=True))
            maps.append((e, 1.0 / jnp.sum(e, axis=-1, keepdims=True)))
        (e1, inv1), (e2, inv2) = maps
        return (e1 * inv1 - e2 * (lam * inv2)).astype(BF16)

    def sub_norm(o):
        return _rms(o, out_gain).astype(BF16)

    n_sub = TQ_DIFF // TQ_SUB

    def logits_to_slot(k, slot):
        rows = slice(k * TQ_SUB, (k + 1) * TQ_SUB)
        for j in range(2):
            q = q_ref[rows, head_cols(j)]
            s_ref[slot, j, :, 0:SEQ] = _dot_nt(q, k_ref[:, head_cols(j)])
            s_ref[slot, j, :, SEQ:] = _dot_nt(q, kc_ref[:, head_cols(j)])

    logits_to_slot(0, 0)
    for k in range(n_sub):
        slot = k % 2
        if k + 1 < n_sub:
            logits_to_slot(k + 1, 1 - slot)
        a = mix([s_ref[slot, 0], s_ref[slot, 1]])
        o = _dot(a[:, 0:SEQ], v_ref[...]) + _dot(a[:, SEQ:], vc_ref[...])
        o_ref[k * TQ_SUB:(k + 1) * TQ_SUB, :] = sub_norm(o)

    if with_ctx_queries:
        @pl.when(i == LAT_QTILES - 1)
        def _context_queries():
            q = qc_ref[...]
            a = mix([_dot_nt(q[:, head_cols(j)], kc_ref[:, head_cols(j)]) for j in range(2)])
            oc_ref[...] = sub_norm(_dot(a, vc_ref[...]))


def _diff_call(p, diff_lambda, diff_subln, layer, with_ctx_queries, ada_next=None):
    lam_init = 0.8 - 0.6 * math.exp(-0.3 * layer)
    w = DIFF_V_DIM
    ctx_row = lambda b: N_LAT // CTX_LEN + b
    n_steps = BATCH * DIFF_HEADS * LAT_QTILES
    step = lambda b, h, i: (b * DIFF_HEADS + h) * LAT_QTILES + i
    in_specs = [
        pl.BlockSpec((None, 4, HEAD_DIM), lambda b, h, i: (layer, 0, 0)),
        pl.BlockSpec((None, 1, DIFF_V_DIM), lambda b, h, i: (layer, 0, 0)),
        pl.BlockSpec((TQ_DIFF, w), lambda b, h, i: (b * LAT_QTILES + i, COL_QD // w + h)),
        pl.BlockSpec((SEQ, w), lambda b, h, i: (b, COL_KD // w + h)),
        pl.BlockSpec((SEQ, w), lambda b, h, i: (b, COL_VD // w + h)),
        pl.BlockSpec((CTX_LEN, w), lambda b, h, i: (ctx_row(b), COL_KD // w + h)),
        pl.BlockSpec((CTX_LEN, w), lambda b, h, i: (ctx_row(b), COL_VD // w + h)),
    ]
    out_specs = [pl.BlockSpec((TQ_DIFF, w), lambda b, h, i: (b * LAT_QTILES + i, h))]
    out_shape = [jax.ShapeDtypeStruct((N_LAT, DIFF_WIDTH), BF16)]
    operands = [diff_lambda, diff_subln.reshape(DEPTH, 1, DIFF_V_DIM), p, p, p, p, p]
    if with_ctx_queries:
        in_specs.append(pl.BlockSpec((CTX_LEN, w), lambda b, h, i: (ctx_row(b), COL_QD // w + h)))
        out_specs.append(pl.BlockSpec((CTX_LEN, w), lambda b, h, i: (b, h)))
        out_shape.append(jax.ShapeDtypeStruct((N_CTX, DIFF_WIDTH), BF16))
        operands.append(p)
    if ada_next is not None:
        cc, w_ada, b_ada, ada_layer = ada_next
        tn = ADA_WIDTH // n_steps
        in_specs += [
            pl.BlockSpec((MOD_ROWS, D_MODEL), lambda b, h, i: (0, 0)),
            pl.BlockSpec((None, D_MODEL, tn), lambda b, h, i: (ada_layer, 0, step(b, h, i))),
            pl.BlockSpec((None, 1, tn), lambda b, h, i: (ada_layer, 0, step(b, h, i))),
        ]
        out_specs.append(pl.BlockSpec((MOD_ROWS, tn), lambda b, h, i: (0, step(b, h, i))))
        out_shape.append(jax.ShapeDtypeStruct((MOD_ROWS, ADA_WIDTH), F32))
        operands += [cc, w_ada, b_ada.reshape(DEPTH, 1, ADA_WIDTH)]
    out = pl.pallas_call(
        functools.partial(_diff_kernel, lam_init=lam_init, with_ctx_queries=with_ctx_queries,
                          with_ada=ada_next is not None),
        grid=(BATCH, DIFF_HEADS, LAT_QTILES),
        in_specs=in_specs,
        out_specs=out_specs,
        out_shape=out_shape,
        scratch_shapes=[pltpu.VMEM((2, 2, TQ_SUB, SEQ + CTX_LEN), F32)],
        compiler_params=_params("arbitrary", "arbitrary", "arbitrary"),
        name=f"diff_attn_l{layer}",
    )(*operands)
    return (out[:-1], out[-1]) if ada_next is not None else (out, None)


def _pool_kernel(u_ref, w_ref, scale_ref, o_ref, pad_ref, *, n_tok):
    t = lax.broadcasted_iota(jnp.int32, (n_tok, 1), 0)
    zeros = jnp.zeros((POOL_HALO, POOL_GROUP), F32)
    pad_ref[0:POOL_HALO, :] = zeros
    pad_ref[POOL_HALO + n_tok:, :] = zeros
    for g, w in enumerate(POOL_WINDOWS):
        cs = slice(g * POOL_GROUP, (g + 1) * POOL_GROUP)
        u = u_ref[:, cs]
        pad_ref[POOL_HALO:POOL_HALO + n_tok, :] = u
        total = pad_ref[POOL_HALO - w // 2:POOL_HALO - w // 2 + n_tok, :]
        for d in range(1 - w // 2, w // 2):
            total = total + pad_ref[POOL_HALO + d:POOL_HALO + d + n_tok, :]
        count = (jnp.clip(t + w // 2, 0, n_tok) - jnp.clip(t - w // 2, 0, n_tok)).astype(F32)
        centred = (total / count - u).astype(BF16)
        o_ref[:, cs] = (_dot(centred, w_ref[g]) * scale_ref[:, cs]).astype(BF16)


def _pool_call(u, pool_w, pool_scale, layer, n_tok, row_block0):
    return pl.pallas_call(
        functools.partial(_pool_kernel, n_tok=n_tok),
        grid=(BATCH,),
        in_specs=[
            pl.BlockSpec((n_tok, POOL_WIDTH), lambda b: (row_block0 + b, 0)),
            pl.BlockSpec((None, len(POOL_WINDOWS), POOL_GROUP, POOL_GROUP), lambda b: (layer, 0, 0, 0)),
            pl.BlockSpec((None, 1, POOL_WIDTH), lambda b: (layer, 0, 0)),
        ],
        out_specs=pl.BlockSpec((n_tok, POOL_WIDTH), lambda b: (b, 0)),
        out_shape=jax.ShapeDtypeStruct((BATCH * n_tok, POOL_WIDTH), BF16),
        scratch_shapes=[pltpu.VMEM((n_tok + 2 * POOL_HALO, POOL_GROUP), F32)],
        compiler_params=_params("arbitrary"),
        name=f"pool_l{layer}_n{n_tok}",
    )(u, pool_w, pool_scale.reshape(DEPTH, 1, POOL_WIDTH))


def _outproj_kernel(s_ref, m_ref, gpost_ref, oa_ref, *rest):
    w_ref, o_ref, stat_ref, vec_ref = rest[-4:]
    oa, ob, oc = oa_ref[...], rest[0][...], rest[1][...]
    if len(rest) == 8:
        latent = pl.program_id(0) < N_LAT // TM_OUT
        ob = jnp.where(latent, ob, rest[2][...])
        oc = jnp.where(latent, oc, rest[3][...])
    b0, c0 = WIN_WIDTH, WIN_WIDTH + POOL_WIDTH
    for n in range(D_MODEL // 1024):
        cs = slice(n * 1024, (n + 1) * 1024)
        o_ref[:, cs] = _dot(oa, w_ref[0:b0, cs]) + _dot(ob, w_ref[b0:c0, cs]) + _dot(oc, w_ref[c0:, cs])
    _gated_residual_to(o_ref, s_ref, o_ref, stat_ref, vec_ref, m_ref[2] * gpost_ref[...], TM_OUT)


def _outproj_call(s, mods, norm_post, o_a, o_b, o_c, w_out, layer, n_rows):
    lat_tiles = N_LAT // TM_OUT
    row_spec = lambda width: pl.BlockSpec((TM_OUT, width), lambda t: (t, 0))
    lat_spec = lambda width: pl.BlockSpec((TM_OUT, width), lambda t: (jnp.minimum(t, lat_tiles - 1), 0))
    ctx_spec = lambda width: pl.BlockSpec((TM_OUT, width), lambda t: (jnp.maximum(t - lat_tiles, 0), 0))
    mixer_specs = [lat_spec(POOL_WIDTH), lat_spec(DIFF_WIDTH)]
    mixers = [o_b[0], o_c[0]]
    if len(o_b) == 2:
        mixer_specs += [ctx_spec(POOL_WIDTH), ctx_spec(DIFF_WIDTH)]
        mixers += [o_b[1], o_c[1]]
    return pl.pallas_call(
        _outproj_kernel,
        grid=(n_rows // TM_OUT,),
        in_specs=[
            row_spec(D_MODEL),
            _mod_spec(1, TM_OUT),
            _norm_spec(layer, 1),
            row_spec(WIN_WIDTH),
        ] + mixer_specs + [
            pl.BlockSpec((D_MODEL, D_MODEL), lambda t: (0, 0), pipeline_mode=pl.Buffered(1)),
        ],
        out_specs=row_spec(D_MODEL),
        out_shape=jax.ShapeDtypeStruct((n_rows, D_MODEL), F32),
        scratch_shapes=_norm_scratch(TM_OUT),
        compiler_params=_params("arbitrary"),
        name=f"outproj_l{layer}",
    )(s, mods, norm_post, o_a, *mixers, w_out)


def _rope_tables():
    row = jnp.repeat(jnp.arange(SEQ // GRID_W, dtype=F32), GRID_W)
    col = jnp.tile(jnp.arange(GRID_W, dtype=F32), SEQ // GRID_W)
    half = HEAD_DIM // 2
    inv = ROPE_THETA ** (-jnp.arange(0, half, 2, dtype=F32) / half)
    ar = row[:, None] * inv
    ac = col[:, None] * inv
    ang = jnp.concatenate([ar, ar, ac, ac], axis=-1)
    cos, sin = jnp.cos(ang), jnp.sin(ang)
    low = (jnp.arange(HEAD_DIM) % half) < half // 2
    return cos, jnp.where(low, -sin, 0.0), jnp.where(low, 0.0, sin)


def kernel(x, c, ctx, c_ctx, w_ada, b_ada, norm_pre, norm_post, ffn_w_gate, ffn_w_up, ffn_w_down,
           w_in, w_out, attn_sink, pool_w, pool_scale, diff_lambda, diff_subln):
    srcs = (x.reshape(N_LAT, D_MODEL), ctx.reshape(N_CTX, D_MODEL))
    cc = jnp.zeros((MOD_ROWS, D_MODEL), F32).at[:BATCH].set(c).at[BATCH].set(c_ctx)
    mods_shape = (MOD_ROWS, 3, 3, 1, D_MODEL)
    mods = _ada_call(cc, w_ada, b_ada, 0).reshape(mods_shape)
    npre = norm_pre.reshape(DEPTH, 3, 1, D_MODEL)
    npost = norm_post.reshape(DEPTH, 3, 1, D_MODEL)
    pool_w16 = pool_w.astype(BF16)
    cos, sina, sinb = _rope_tables()

    ffn_f32 = (ffn_w_gate, ffn_w_up, ffn_w_down)
    ffn_order = [(layer, half) for layer in range(DEPTH) for half in range(2)]
    ffn16 = tuple(w[0, 0].astype(BF16) for w in ffn_f32)
    w_out16 = w_out[0].astype(BF16)

    def next_ffn(layer, half):
        i = ffn_order.index((layer, half)) + 1
        return [(w, ffn_order[i]) for w in ffn_f32] if i < len(ffn_order) else []

    for layer in range(DEPTH):
        update_ctx = layer < DEPTH - 1
        n_tiles = ALL_TILES if update_ctx else LAT_TILES
        jobs = next_ffn(layer, 0)
        if layer == 0:
            jobs, down_job = jobs[:2] + [(w_in, (0,))], jobs[2:]
        s, done = _ffn_call(srcs, mods, npre, npost, ffn16, layer, 0, ALL_TILES, jobs)
        if layer == 0:
            ffn16, w_in16 = done[:2], done[2]
        else:
            ffn16, down_job = done, []
        proj_next = ([(w_in, (layer + 1,)), (w_out, (layer + 1,))] if update_ctx else []) + down_job
        p, u, proj16 = _inproj_call(s, mods, npre, w_in16, cos, sina, sinb, layer, proj_next)
        if down_job:
            ffn16, proj16 = ffn16 + proj16[-1:], proj16[:-1]
        o_a = _win_call(p, attn_sink, layer, update_ctx)
        ada_next = (cc, w_ada, b_ada, layer + 1) if update_ctx else None
        o_c, mods_next = _diff_call(p, diff_lambda, diff_subln, layer, update_ctx, ada_next)
        o_b =[_pool_call(u, pool_w16, pool_scale, layer, SEQ, 0)]
        if update_ctx:
            o_b.append(_pool_call(u, pool_w16, pool_scale, layer, CTX_LEN, N_LAT // CTX_LEN))
        s = _outproj_call(s, mods, npost, o_a, o_b, o_c, w_out16, layer, n_tiles * TM)
        s, ffn16 = _ffn_call((s,), mods, npre, npost, ffn16, layer, 1, n_tiles, next_ffn(layer, 1))
        srcs = (s,)
        if update_ctx:
            w_in16, w_out16 = proj16
            mods = mods_next.reshape(mods_shape)
    return s.reshape(BATCH, SEQ, D_MODEL)
```

```python
import functools
import math

import jax
import jax.numpy as jnp
from jax import lax
from jax.experimental import pallas as pl
from jax.experimental.pallas import tpu as pltpu

D_MODEL = 4096
BATCH = 4
SEQ = 2048
DEPTH = 2
GRID_W = 64
CTX_LEN = 256
HEAD_DIM = 128
ROPE_THETA = 10000.0
EPS = 1e-6
N_MOD = 9
FFN_DIM = 2 * D_MODEL

WIN_Q_HEADS = 12
WIN_KV_HEADS = 4
WIN_GROUP = WIN_Q_HEADS // WIN_KV_HEADS
WIN_WIDTH = WIN_Q_HEADS * HEAD_DIM
WINDOW = 128
BLOCK = 128

POOL_WINDOWS = (2, 4, 8, 16)
POOL_WIDTH = 1024
POOL_GROUP = POOL_WIDTH // len(POOL_WINDOWS)
POOL_HALO = 8

DIFF_HEADS = 6
DIFF_V_DIM = 2 * HEAD_DIM
DIFF_WIDTH = DIFF_HEADS * DIFF_V_DIM
IN_WIDTH = 8192

COL_QA = 0
COL_KA = COL_QA + WIN_WIDTH
COL_VA = COL_KA + WIN_KV_HEADS * HEAD_DIM
COL_U = COL_VA + WIN_KV_HEADS * HEAD_DIM
COL_QD = COL_U + POOL_WIDTH
COL_KD = COL_QD + 2 * DIFF_HEADS * HEAD_DIM
COL_VD = COL_KD + 2 * DIFF_HEADS * HEAD_DIM

F32 = jnp.float32
BF16 = jnp.bfloat16
LANES = 128
SUBLANES = 8
LOG2E = math.log2(math.e)
Q_TO_LOG2 = HEAD_DIM ** -0.5 * LOG2E
ADA_WIDTH = N_MOD * D_MODEL

N_LAT = BATCH * SEQ
N_CTX = BATCH * CTX_LEN
N_ALL = N_LAT + N_CTX

TM = 512
TM_OUT = 256
TF = 512
TN_IN = 1024
RH = 512
MXU_COLS = 256
TN_ADA = 512
TQ_DIFF = 1024
TQ_SUB = 256
ROW_CHUNK = 16
PASS_CHUNKS = 2
MOD_ROWS = 16
VMEM_LIMIT = 63 * 1024 * 1024

LAT_TILES = N_LAT // TM
ALL_TILES = N_ALL // TM


def _mod_row(t, tm):
    return jnp.where(t < N_LAT // tm, t // (SEQ // tm), BATCH)


def _dot(a, b):
    return jnp.dot(a, b, preferred_element_type=F32)


def _dot_nt(a, b):
    return lax.dot_general(a, b, (((1,), (1,)), ((), ())), preferred_element_type=F32)


def _for_rows(n_rows, chunk, fn, unroll=1):
    def body(i, carry):
        fn(pl.ds(pl.multiple_of(i * chunk, chunk), chunk))
        return carry
    lax.fori_loop(0, n_rows // chunk, body, 0, unroll=unroll)


def _for_row_starts(n_rows, chunk, fn):
    def body(i, carry):
        fn(pl.multiple_of(i * chunk, chunk))
        return carry
    lax.fori_loop(0, n_rows // chunk, body, 0)


def _rstd_pass(load_rows, stat_ref, n_rows):
    def rows(r):
        v = load_rows(r)
        sq = v * v
        parts = [sq[:, k * LANES:(k + 1) * LANES] for k in range(D_MODEL // LANES)]
        while len(parts) > 1:
            parts = [parts[k] + parts[k + 1] for k in range(0, len(parts), 2)]
        total = jnp.sum(parts[0], axis=-1, keepdims=True)
        rstd = lax.rsqrt(total * (1.0 / D_MODEL) + EPS)
        stat_ref[r, :] = jnp.broadcast_to(rstd, (SUBLANES, LANES))
    _for_rows(n_rows, SUBLANES, rows, unroll=min(64, n_rows // SUBLANES))


def _rms(v, g):
    return v * lax.rsqrt(jnp.mean(v * v, axis=-1, keepdims=True) + EPS) * g


def _params(*sem):
    return pltpu.CompilerParams(dimension_semantics=sem, vmem_limit_bytes=VMEM_LIMIT)


def _ada_kernel(c_ref, w_ref, b_ref, o_ref):
    c = c_ref[...]
    s = (c / (1.0 + jnp.exp(-c))).astype(BF16)
    o_ref[...] = _dot(s, w_ref[...].astype(BF16)) + b_ref[...]


def _ada_call(cc, w_ada, b_ada, layer):
    return pl.pallas_call(
        _ada_kernel,
        grid=(ADA_WIDTH // TN_ADA,),
        in_specs=[
            pl.BlockSpec((MOD_ROWS, D_MODEL), lambda j: (0, 0)),
            pl.BlockSpec((None, D_MODEL, TN_ADA), lambda j: (layer, 0, j)),
            pl.BlockSpec((None, 1, TN_ADA), lambda j: (layer, 0, j)),
        ],
        out_specs=pl.BlockSpec((MOD_ROWS, TN_ADA), lambda j: (0, j)),
        out_shape=jax.ShapeDtypeStruct((MOD_ROWS, ADA_WIDTH), F32),
        compiler_params=_params("arbitrary"),
        name=f"adaln_l{layer}",
    )(cc, w_ada, b_ada.reshape(DEPTH, 1, ADA_WIDTH))


def _mod_spec(group, tm):
    return pl.BlockSpec((None, None, 3, 1, D_MODEL), lambda t, *_: (_mod_row(t, tm), group, 0, 0, 0))


def _norm_spec(layer, k):
    return pl.BlockSpec((None, None, 1, D_MODEL), lambda t, *_: (layer, k, 0, 0))


def _modulated_norm_to(h_ref, s_ref, stat_ref, vec_ref, m_ref, gpre_ref, n_rows, also=None):
    _rstd_pass(lambda r: s_ref[r, :], stat_ref, n_rows)
    full = (ROW_CHUNK, D_MODEL)
    vec_ref[0:ROW_CHUNK, :] = jnp.broadcast_to(gpre_ref[...] * (1.0 + m_ref[1]), full)
    vec_ref[ROW_CHUNK:, :] = jnp.broadcast_to(m_ref[0], full)

    def rows(r0):
        rs = [pl.ds(r0 + c * ROW_CHUNK, ROW_CHUNK) for c in range(PASS_CHUNKS)]
        rstd = [stat_ref[r, :] for r in rs]
        for k in range(D_MODEL // LANES):
            cs = slice(k * LANES, (k + 1) * LANES)
            gain, shift = vec_ref[0:ROW_CHUNK, cs], vec_ref[ROW_CHUNK:, cs]
            for r, rstd_r in zip(rs, rstd):
                h_ref[r, cs] = (s_ref[r, cs] * rstd_r * gain + shift).astype(BF16)
        if also is not None:
            for r in rs:
                also(r)
    _for_row_starts(n_rows, PASS_CHUNKS * ROW_CHUNK, rows)


def _gated_residual_to(o_ref, s_ref, y_ref, stat_ref, vec_ref, gate_gain, n_rows):
    _rstd_pass(lambda r: y_ref[r, :], stat_ref, n_rows)
    vec_ref[0:ROW_CHUNK, :] = jnp.broadcast_to(gate_gain, (ROW_CHUNK, D_MODEL))

    def rows(r):
        rstd = stat_ref[r, :]
        for k in range(D_MODEL // LANES):
            cs = slice(k * LANES, (k + 1) * LANES)
            o_ref[r, cs] = s_ref[r, cs] + y_ref[r, cs] * rstd * vec_ref[0:ROW_CHUNK, cs]
    _for_rows(n_rows, ROW_CHUNK, rows, unroll=2)


def _norm_scratch(n_rows):
    return [pltpu.VMEM((n_rows, LANES), F32), pltpu.VMEM((2 * ROW_CHUNK, D_MODEL), F32)]


def _hosted_cast(arr, lead, n_inner):
    rows, cols = arr.shape[-2:]
    block = (rows // LAT_TILES, cols // n_inner)

    def index(t, i):
        return jnp.minimum(t, LAT_TILES - 1), jnp.where(t < LAT_TILES, i, n_inner - 1)

    in_spec = pl.BlockSpec((None,) * len(lead) + block, lambda t, i: lead + index(t, i))
    return in_spec, pl.BlockSpec(block, index), jax.ShapeDtypeStruct((rows, cols), BF16)


def _ffn_kernel(*refs, n_src, n_cast, n_tiles):
    srcs, refs = refs[:n_src], refs[n_src:]
    m_ref, gpre_ref, gpost_ref, wg_ref, wu_ref, wd_ref = refs[:6]
    cast_src, o_hbm, cast_dst = refs[6:6 + n_cast], refs[6 + n_cast], refs[7 + n_cast:7 + 2 * n_cast]
    slots, acc_ref, h_ref, stat_ref, vec_ref, fetch_sem, store_sem = refs[7 + 2 * n_cast:]
    t = pl.program_id(0)
    f = pl.program_id(1)
    cur = slots.at[lax.rem(t, 2)]

    def tile_rows(ref, tile):
        return ref.at[pl.ds(pl.multiple_of(tile * TM, TM), TM), :]

    def fetch(tile, start):
        slot = lax.rem(tile, 2)

        def run(src, src_tile):
            copy = pltpu.make_async_copy(tile_rows(src, src_tile), slots.at[slot], fetch_sem.at[slot])
            copy.start() if start else copy.wait()
        if n_src == 1:
            run(srcs[0], tile)
        else:
            pl.when(tile < LAT_TILES)(lambda: run(srcs[0], tile))
            pl.when(tile >= LAT_TILES)(lambda: run(srcs[1], tile - LAT_TILES))

    def store(tile):
        slot = lax.rem(tile, 2)
        return pltpu.make_async_copy(slots.at[slot], tile_rows(o_hbm, tile), store_sem.at[slot])

    @pl.when((t == 0) & (f == 0))
    def _first_fetch():
        fetch(t, start=True)

    @pl.when(f == 0)
    def _prologue():
        fetch(t, start=False)

        def zero_acc(r):
            acc_ref[r, :] = jnp.zeros((ROW_CHUNK, D_MODEL), F32)
        _modulated_norm_to(h_ref, cur, stat_ref, vec_ref, m_ref, gpre_ref, TM, also=zero_acc)

    @pl.when(f == 1)
    def _refill_other_slot():
        @pl.when(t >= 1)
        def _():
            store(t - 1).wait()

        @pl.when(t + 1 < n_tiles)
        def _():
            fetch(t + 1, start=True)

    h = h_ref[...]
    g = _dot(h, wg_ref[...])
    u = _dot(h, wu_ref[...])
    a = (g / (1.0 + jnp.exp(-g)) * u).astype(BF16)
    for c in range(D_MODEL // 1024):
        cs = slice(c * 1024, (c + 1) * 1024)
        acc_ref[:, cs] += _dot(a, wd_ref[:, cs])
    for src, dst in zip(cast_src, cast_dst):
        dst[...] = src[...].astype(BF16)

    @pl.when(f == pl.num_programs(1) - 1)
    def _epilogue():
        _gated_residual_to(cur, cur, acc_ref, stat_ref, vec_ref, 0.5 * m_ref[2] * gpost_ref[...], TM)
        store(t).start()

        @pl.when(t == n_tiles - 1)
        def _():
            store(t).wait()


def _ffn_call(srcs, mods, norm_pre, norm_post, weights, layer, half, n_tiles, cast_next=()):
    k = 2 * half
    n_f = FFN_DIM // TF
    assert n_f >= 2 and n_tiles >= 2
    hosted = [_hosted_cast(arr, lead, n_f) for arr, lead in cast_next]
    hbm = pl.BlockSpec(memory_space=pl.ANY)
    out = pl.pallas_call(
        functools.partial(_ffn_kernel, n_src=len(srcs), n_cast=len(hosted), n_tiles=n_tiles),
        grid=(n_tiles, n_f),
        in_specs=[hbm] * len(srcs) + [
            _mod_spec(k, TM),
            _norm_spec(layer, k),
            _norm_spec(layer, k),
            pl.BlockSpec((D_MODEL, TF), lambda t, f: (0, f)),
            pl.BlockSpec((D_MODEL, TF), lambda t, f: (0, f)),
            pl.BlockSpec((TF, D_MODEL), lambda t, f: (f, 0)),
        ] + [h[0] for h in hosted],
        out_specs=[hbm] + [h[1] for h in hosted],
        out_shape=[jax.ShapeDtypeStruct((n_tiles * TM, D_MODEL), F32)] + [h[2] for h in hosted],
        scratch_shapes=[
            pltpu.VMEM((2, TM, D_MODEL), F32),
            pltpu.VMEM((TM, D_MODEL), F32),
            pltpu.VMEM((TM, D_MODEL), BF16),
        ] + _norm_scratch(TM) + [pltpu.SemaphoreType.DMA((2,)), pltpu.SemaphoreType.DMA((2,))],
        compiler_params=_params("arbitrary", "arbitrary"),
        name=f"ffn_l{layer}h{half}",
    )(*srcs, mods, norm_pre, norm_post, *weights, *[arr for arr, _ in cast_next])
    return out[0], tuple(out[1:])


def _is_rope_block(jj):
    blk = lambda col: col // RH
    return (jj < blk(COL_VA)) | ((jj >= blk(COL_QD)) & (jj < blk(COL_VD)))


def _is_query_block(jj):
    blk = lambda col: col // RH
    return (jj < blk(COL_KA)) | ((jj >= blk(COL_QD)) & (jj < blk(COL_KD)))


def _inproj_kernel(s_ref, m_ref, gpre_ref, w_ref, cos_ref, sina_ref, sinb_ref, *rest):
    n_cast = (len(rest) - 5) // 2
    cast_src, (p_ref, u_ref) = rest[:n_cast], rest[n_cast:n_cast + 2]
    cast_dst = rest[n_cast + 2:2 * n_cast + 2]
    h_ref, stat_ref, vec_ref = rest[2 * n_cast + 2:]
    t = pl.program_id(0)
    j = pl.program_id(1)

    @pl.when(j == 0)
    def _prologue():
        _modulated_norm_to(h_ref, s_ref, stat_ref, vec_ref, m_ref, gpre_ref, TM)

    for src, dst in zip(cast_src, cast_dst):
        dst[...] = src[...].astype(BF16)

    h = h_ref[...]
    chunks = []
    for c in range(TN_IN // MXU_COLS):
        if c % (RH // MXU_COLS) == 0:
            blk = j * (TN_IN // RH) + c // (RH // MXU_COLS)
            rope = _is_rope_block(blk) & (t < LAT_TILES)
            gain = jnp.where(_is_query_block(blk), Q_TO_LOG2, 1.0)
            cos = jnp.where(rope, cos_ref[...], 1.0) * gain
            sina = jnp.where(rope, sina_ref[...], 0.0) * gain
            sinb = jnp.where(rope, sinb_ref[...], 0.0) * gain
        y = _dot(h, w_ref[:, c * MXU_COLS:(c + 1) * MXU_COLS])
        chunks.append(y)
        for hd in range(MXU_COLS // HEAD_DIM):
            v = y[:, hd * HEAD_DIM:(hd + 1) * HEAD_DIM]
            fwd = pltpu.roll(v, HEAD_DIM // 4, 1)
            bwd = pltpu.roll(v, HEAD_DIM - HEAD_DIM // 4, 1)
            col = c * MXU_COLS + hd * HEAD_DIM
            p_ref[:, col:col + HEAD_DIM] = (v * cos + bwd * sina + fwd * sinb).astype(BF16)

    for step in range(IN_WIDTH // TN_IN):
        for c in range(TN_IN // MXU_COLS):
            col = step * TN_IN + c * MXU_COLS
            if COL_U <= col < COL_QD:
                @pl.when(j == step)
                def _pool_channels(c=c, col=col):
                    u_ref[:, col - COL_U:col - COL_U + MXU_COLS] = chunks[c]


def _inproj_call(s, mods, norm_pre, w_in, cos, sina, sinb, layer, cast_next=()):
    n_j = IN_WIDTH // TN_IN
    hosted = [_hosted_cast(arr, lead, n_j) for arr, lead in cast_next]
    rope_spec = pl.BlockSpec((TM, HEAD_DIM), lambda t, j: (t % (SEQ // TM), 0))
    out = pl.pallas_call(
        _inproj_kernel,
        grid=(ALL_TILES, n_j),
        in_specs=[
            pl.BlockSpec((TM, D_MODEL), lambda t, j: (t, 0)),
            _mod_spec(1, TM),
            _norm_spec(layer, 1),
            pl.BlockSpec((D_MODEL, TN_IN), lambda t, j: (0, j)),
            rope_spec, rope_spec, rope_spec,
        ] + [h[0] for h in hosted],
        out_specs=[
            pl.BlockSpec((TM, TN_IN), lambda t, j: (t, j)),
            pl.BlockSpec((TM, POOL_WIDTH), lambda t, j: (t, 0)),
        ] + [h[1] for h in hosted],
        out_shape=[
            jax.ShapeDtypeStruct((N_ALL, IN_WIDTH), BF16),
            jax.ShapeDtypeStruct((N_ALL, POOL_WIDTH), F32),
        ] + [h[2] for h in hosted],
        scratch_shapes=[pltpu.VMEM((TM, D_MODEL), BF16)] + _norm_scratch(TM),
        compiler_params=_params("arbitrary", "arbitrary"),
        name=f"inproj_l{layer}",
    )(s, mods, norm_pre, w_in, cos, sina, sinb, *[arr for arr, _ in cast_next])
    return out[0], out[1], tuple(out[2:])


LAT_QBLOCKS = SEQ // BLOCK
CTX_QBLOCKS = CTX_LEN // BLOCK
BAND = 3 * BLOCK


def _win_kernel(sink_ref, q_ref, k_ref, v_ref, kc_ref, vc_ref, o_ref, s_ref):
    n = pl.program_id(1)
    row = lax.broadcasted_iota(jnp.int32, (WIN_GROUP * BLOCK, 1), 0)

    def head_slice(i):
        return slice(i * HEAD_DIM, (i + 1) * HEAD_DIM)

    def group_q(h):
        return jnp.concatenate([q_ref[:, head_slice(WIN_GROUP * h + g)] for g in range(WIN_GROUP)], axis=0)

    def group_sink(h):
        s0 = sink_ref[0, WIN_GROUP * h]
        s1 = sink_ref[0, WIN_GROUP * h + 1]
        s2 = sink_ref[0, WIN_GROUP * h + 2]
        return jnp.where(row < BLOCK, s0, jnp.where(row < 2 * BLOCK, s1, s2)) * LOG2E

    def store(h, o):
        for g in range(WIN_GROUP):
            o_ref[:, head_slice(WIN_GROUP * h + g)] = o[g * BLOCK:(g + 1) * BLOCK].astype(BF16)

    @pl.when(n < LAT_QBLOCKS)
    def _latent_queries():
        start = pl.multiple_of(jnp.clip((n - 1) * BLOCK, 0, SEQ - BAND), BLOCK)
        qpos = n * BLOCK + (row & (BLOCK - 1))
        kpos = start + lax.broadcasted_iota(jnp.int32, (1, BAND), 1)
        valid = jnp.abs(kpos - qpos) <= WINDOW

        def logits_to_slot(h, slot):
            q = group_q(h)
            s_ref[slot, :, 0:BAND] = jnp.where(valid, _dot_nt(q, k_ref[pl.ds(start, BAND), head_slice(h)]),
                                               -jnp.inf)
            s_ref[slot, :, BAND:] = _dot_nt(q, kc_ref[:, head_slice(h)])

        logits_to_slot(0, 0)
        for h in range(WIN_KV_HEADS):
            hs = head_slice(h)
            slot = h % 2
            if h + 1 < WIN_KV_HEADS:
                logits_to_slot(h + 1, 1 - slot)
            s = s_ref[slot]
            sink = group_sink(h)
            m = jnp.maximum(jnp.max(s, axis=-1, keepdims=True), sink)
            e = jnp.exp2(s - m)
            den = jnp.sum(e, axis=-1, keepdims=True) + jnp.exp2(sink - m)
            e = e.astype(BF16)
            o = _dot(e[:, 0:BAND], v_ref[pl.ds(start, BAND), hs]) + _dot(e[:, BAND:], vc_ref[:, hs])
            store(h, o * (1.0 / den))

    @pl.when(n >= LAT_QBLOCKS)
    def _context_queries():
        for h in range(WIN_KV_HEADS):
            hs = head_slice(h)
            s_ctx = _dot_nt(group_q(h), kc_ref[:, hs])
            sink = group_sink(h)
            m = jnp.maximum(jnp.max(s_ctx, axis=-1, keepdims=True), sink)
            e_ctx = jnp.exp2(s_ctx - m)
            den = jnp.sum(e_ctx, axis=-1, keepdims=True) + jnp.exp2(sink - m)
            store(h, _dot(e_ctx.astype(BF16), vc_ref[:, hs]) * (1.0 / den))


def _win_call(p, sink, layer, with_ctx_queries):
    nq = LAT_QBLOCKS + (CTX_QBLOCKS if with_ctx_queries else 0)
    kv_cols = WIN_KV_HEADS * HEAD_DIM

    def q_row(b, n):
        return jnp.where(n < LAT_QBLOCKS, b * LAT_QBLOCKS + n,
                         N_LAT // BLOCK + b * CTX_QBLOCKS + n - LAT_QBLOCKS)

    ctx_row = lambda b: N_LAT // CTX_LEN + b
    return pl.pallas_call(
        _win_kernel,
        grid=(BATCH, nq),
        in_specs=[
            pl.BlockSpec(memory_space=pltpu.SMEM),
            pl.BlockSpec((BLOCK, WIN_WIDTH), lambda b, n: (q_row(b, n), 0)),
            pl.BlockSpec((SEQ, kv_cols), lambda b, n: (b, COL_KA // kv_cols)),
            pl.BlockSpec((SEQ, kv_cols), lambda b, n: (b, COL_VA // kv_cols)),
            pl.BlockSpec((CTX_LEN, kv_cols), lambda b, n: (ctx_row(b), COL_KA // kv_cols)),
            pl.BlockSpec((CTX_LEN, kv_cols), lambda b, n: (ctx_row(b), COL_VA // kv_cols)),
        ],
        out_specs=pl.BlockSpec((BLOCK, WIN_WIDTH), lambda b, n: (q_row(b, n), 0)),
        out_shape=jax.ShapeDtypeStruct((N_ALL if with_ctx_queries else N_LAT, WIN_WIDTH), BF16),
        scratch_shapes=[pltpu.VMEM((2, WIN_GROUP * BLOCK, BAND + CTX_LEN), F32)],
        compiler_params=_params("arbitrary", "arbitrary"),
        name=f"win_attn_l{layer}",
    )(sink[layer].reshape(1, WIN_Q_HEADS), p, p, p, p, p)


LAT_QTILES = SEQ // TQ_DIFF


def _diff_kernel(*refs, lam_init, with_ctx_queries, with_ada):
    lam_ref, subln_ref, q_ref, k_ref, v_ref, kc_ref, vc_ref = refs[:7]
    rest = list(refs[7:])
    s_ref = rest.pop(-1)
    qc_ref = rest.pop(0) if with_ctx_queries else None
    ada_in = [rest.pop(0) for _ in range(3)] if with_ada else None
    o_ref = rest.pop(0)
    oc_ref = rest.pop(0) if with_ctx_queries else None
    i = pl.program_id(2)
    if with_ada:
        _ada_kernel(*ada_in, rest.pop(0))
    dl = lam_ref[...]
    lam = (jnp.exp(jnp.sum(dl[0:1] * dl[1:2], axis=-1, keepdims=True))
           - jnp.exp(jnp.sum(dl[2:3] * dl[3:4], axis=-1, keepdims=True)) + lam_init)
    out_gain = subln_ref[...] * (1.0 - lam_init)

    def head_cols(j):
        return slice(j * HEAD_DIM, (j + 1) * HEAD_DIM)

    def mix(logits):
        maps = []
        for s in logits:
            e = jnp.exp2(s - jnp.max(s, axis=-1, keepdims=True))
            maps.append((e, 1.0 / jnp.sum(e, axis=-1, keepdims=True)))
        (e1, inv1), (e2, inv2) = maps
        return (e1 * inv1 - e2 * (lam * inv2)).astype(BF16)

    def sub_norm(o):
        return _rms(o, out_gain).astype(BF16)

    n_sub = TQ_DIFF // TQ_SUB

    def logits_to_slot(k, slot):
        rows = slice(k * TQ_SUB, (k + 1) * TQ_SUB)
        for j in range(2):
            q = q_ref[rows, head_cols(j)]
            s_ref[slot, j, :, 0:SEQ] = _dot_nt(q, k_ref[:, head_cols(j)])
            s_ref[slot, j, :, SEQ:] = _dot_nt(q, kc_ref[:, head_cols(j)])

    logits_to_slot(0, 0)
    for k in range(n_sub):
        slot = k % 2
        if k + 1 < n_sub:
            logits_to_slot(k + 1, 1 - slot)
        a = mix([s_ref[slot, 0], s_ref[slot, 1]])
        o = _dot(a[:, 0:SEQ], v_ref[...]) + _dot(a[:, SEQ:], vc_ref[...])
        o_ref[k * TQ_SUB:(k + 1) * TQ_SUB, :] = sub_norm(o)

    if with_ctx_queries:
        @pl.when(i == LAT_QTILES - 1)
        def _context_queries():
            q = qc_ref[...]
            a = mix([_dot_nt(q[:, head_cols(j)], kc_ref[:, head_cols(j)]) for j in range(2)])
            oc_ref[...] = sub_norm(_dot(a, vc_ref[...]))


def _diff_call(p, diff_lambda, diff_subln, layer, with_ctx_queries, ada_next=None):
    lam_init = 0.8 - 0.6 * math.exp(-0.3 * layer)
    w = DIFF_V_DIM
    ctx_row = lambda b: N_LAT // CTX_LEN + b
    n_steps = BATCH * DIFF_HEADS * LAT_QTILES
    step = lambda b, h, i: (b * DIFF_HEADS + h) * LAT_QTILES + i
    in_specs = [
        pl.BlockSpec((None, 4, HEAD_DIM), lambda b, h, i: (layer, 0, 0)),
        pl.BlockSpec((None, 1, DIFF_V_DIM), lambda b, h, i: (layer, 0, 0)),
        pl.BlockSpec((TQ_DIFF, w), lambda b, h, i: (b * LAT_QTILES + i, COL_QD // w + h)),
        pl.BlockSpec((SEQ, w), lambda b, h, i: (b, COL_KD // w + h)),
        pl.BlockSpec((SEQ, w), lambda b, h, i: (b, COL_VD // w + h)),
        pl.BlockSpec((CTX_LEN, w), lambda b, h, i: (ctx_row(b), COL_KD // w + h)),
        pl.BlockSpec((CTX_LEN, w), lambda b, h, i: (ctx_row(b), COL_VD // w + h)),
    ]
    out_specs = [pl.BlockSpec((TQ_DIFF, w), lambda b, h, i: (b * LAT_QTILES + i, h))]
    out_shape = [jax.ShapeDtypeStruct((N_LAT, DIFF_WIDTH), BF16)]
    operands = [diff_lambda, diff_subln.reshape(DEPTH, 1, DIFF_V_DIM), p, p, p, p, p]
    if with_ctx_queries:
        in_specs.append(pl.BlockSpec((CTX_LEN, w), lambda b, h, i: (ctx_row(b), COL_QD // w + h)))
        out_specs.append(pl.BlockSpec((CTX_LEN, w), lambda b, h, i: (b, h)))
        out_shape.append(jax.ShapeDtypeStruct((N_CTX, DIFF_WIDTH), BF16))
        operands.append(p)
    if ada_next is not None:
        cc, w_ada, b_ada, ada_layer = ada_next
        tn = ADA_WIDTH // n_steps
        in_specs += [
            pl.BlockSpec((MOD_ROWS, D_MODEL), lambda b, h, i: (0, 0)),
            pl.BlockSpec((None, D_MODEL, tn), lambda b, h, i: (ada_layer, 0, step(b, h, i))),
            pl.BlockSpec((None, 1, tn), lambda b, h, i: (ada_layer, 0, step(b, h, i))),
        ]
        out_specs.append(pl.BlockSpec((MOD_ROWS, tn), lambda b, h, i: (0, step(b, h, i))))
        out_shape.append(jax.ShapeDtypeStruct((MOD_ROWS, ADA_WIDTH), F32))
        operands += [cc, w_ada, b_ada.reshape(DEPTH, 1, ADA_WIDTH)]
    out = pl.pallas_call(
        functools.partial(_diff_kernel, lam_init=lam_init, with_ctx_queries=with_ctx_queries,
                          with_ada=ada_next is not None),
        grid=(BATCH, DIFF_HEADS, LAT_QTILES),
        in_specs=in_specs,
        out_specs=out_specs,
        out_shape=out_shape,
        scratch_shapes=[pltpu.VMEM((2, 2, TQ_SUB, SEQ + CTX_LEN), F32)],
        compiler_params=_params("arbitrary", "arbitrary", "arbitrary"),
        name=f"diff_attn_l{layer}",
    )(*operands)
    return (out[:-1], out[-1]) if ada_next is not None else (out, None)


def _pool_kernel(u_ref, w_ref, scale_ref, o_ref, pad_ref, *, n_tok):
    t = lax.broadcasted_iota(jnp.int32, (n_tok, 1), 0)
    zeros = jnp.zeros((POOL_HALO, POOL_GROUP), F32)
    pad_ref[0:POOL_HALO, :] = zeros
    pad_ref[POOL_HALO + n_tok:, :] = zeros
    for g, w in enumerate(POOL_WINDOWS):
        cs = slice(g * POOL_GROUP, (g + 1) * POOL_GROUP)
        u = u_ref[:, cs]
        pad_ref[POOL_HALO:POOL_HALO + n_tok, :] = u
        total = pad_ref[POOL_HALO - w // 2:POOL_HALO - w // 2 + n_tok, :]
        for d in range(1 - w // 2, w // 2):
            total = total + pad_ref[POOL_HALO + d:POOL_HALO + d + n_tok, :]
        count = (jnp.clip(t + w // 2, 0, n_tok) - jnp.clip(t - w // 2, 0, n_tok)).astype(F32)
        centred = (total / count - u).astype(BF16)
        o_ref[:, cs] = (_dot(centred, w_ref[g]) * scale_ref[:, cs]).astype(BF16)


def _pool_call(u, pool_w, pool_scale, layer, n_tok, row_block0):
    return pl.pallas_call(
        functools.partial(_pool_kernel, n_tok=n_tok),
        grid=(BATCH,),
        in_specs=[
            pl.BlockSpec((n_tok, POOL_WIDTH), lambda b: (row_block0 + b, 0)),
            pl.BlockSpec((None, len(POOL_WINDOWS), POOL_GROUP, POOL_GROUP), lambda b: (layer, 0, 0, 0)),
            pl.BlockSpec((None, 1, POOL_WIDTH), lambda b: (layer, 0, 0)),
        ],
        out_specs=pl.BlockSpec((n_tok, POOL_WIDTH), lambda b: (b, 0)),
        out_shape=jax.ShapeDtypeStruct((BATCH * n_tok, POOL_WIDTH), BF16),
        scratch_shapes=[pltpu.VMEM((n_tok + 2 * POOL_HALO, POOL_GROUP), F32)],
        compiler_params=_params("arbitrary"),
        name=f"pool_l{layer}_n{n_tok}",
    )(u, pool_w, pool_scale.reshape(DEPTH, 1, POOL_WIDTH))


def _outproj_kernel(s_ref, m_ref, gpost_ref, oa_ref, *rest):
    w_ref, o_ref, stat_ref, vec_ref = rest[-4:]
    oa, ob, oc = oa_ref[...], rest[0][...], rest[1][...]
    if len(rest) == 8:
        latent = pl.program_id(0) < N_LAT // TM_OUT
        ob = jnp.where(latent, ob, rest[2][...])
        oc = jnp.where(latent, oc, rest[3][...])
    b0, c0 = WIN_WIDTH, WIN_WIDTH + POOL_WIDTH
    for n in range(D_MODEL // 1024):
        cs = slice(n * 1024, (n + 1) * 1024)
        o_ref[:, cs] = _dot(oa, w_ref[0:b0, cs]) + _dot(ob, w_ref[b0:c0, cs]) + _dot(oc, w_ref[c0:, cs])
    _gated_residual_to(o_ref, s_ref, o_ref, stat_ref, vec_ref, m_ref[2] * gpost_ref[...], TM_OUT)


def _outproj_call(s, mods, norm_post, o_a, o_b, o_c, w_out, layer, n_rows):
    lat_tiles = N_LAT // TM_OUT
    row_spec = lambda width: pl.BlockSpec((TM_OUT, width), lambda t: (t, 0))
    lat_spec = lambda width: pl.BlockSpec((TM_OUT, width), lambda t: (jnp.minimum(t, lat_tiles - 1), 0))
    ctx_spec = lambda width: pl.BlockSpec((TM_OUT, width), lambda t: (jnp.maximum(t - lat_tiles, 0), 0))
    mixer_specs = [lat_spec(POOL_WIDTH), lat_spec(DIFF_WIDTH)]
    mixers = [o_b[0], o_c[0]]
    if len(o_b) == 2:
        mixer_specs += [ctx_spec(POOL_WIDTH), ctx_spec(DIFF_WIDTH)]
        mixers += [o_b[1], o_c[1]]
    return pl.pallas_call(
        _outproj_kernel,
        grid=(n_rows // TM_OUT,),
        in_specs=[
            row_spec(D_MODEL),
            _mod_spec(1, TM_OUT),
            _norm_spec(layer, 1),
            row_spec(WIN_WIDTH),
        ] + mixer_specs + [
            pl.BlockSpec((D_MODEL, D_MODEL), lambda t: (0, 0), pipeline_mode=pl.Buffered(1)),
        ],
        out_specs=row_spec(D_MODEL),
        out_shape=jax.ShapeDtypeStruct((n_rows, D_MODEL), F32),
        scratch_shapes=_norm_scratch(TM_OUT),
        compiler_params=_params("arbitrary"),
        name=f"outproj_l{layer}",
    )(s, mods, norm_post, o_a, *mixers, w_out)


def _rope_tables():
    row = jnp.repeat(jnp.arange(SEQ // GRID_W, dtype=F32), GRID_W)
    col = jnp.tile(jnp.arange(GRID_W, dtype=F32), SEQ // GRID_W)
    half = HEAD_DIM // 2
    inv = ROPE_THETA ** (-jnp.arange(0, half, 2, dtype=F32) / half)
    ar = row[:, None] * inv
    ac = col[:, None] * inv
    ang = jnp.concatenate([ar, ar, ac, ac], axis=-1)
    cos, sin = jnp.cos(ang), jnp.sin(ang)
    low = (jnp.arange(HEAD_DIM) % half) < half // 2
    return cos, jnp.where(low, -sin, 0.0), jnp.where(low, 0.0, sin)


def kernel(x, c, ctx, c_ctx, w_ada, b_ada, norm_pre, norm_post, ffn_w_gate, ffn_w_up, ffn_w_down,
           w_in, w_out, attn_sink, pool_w, pool_scale, diff_lambda, diff_subln):
    srcs = (x.reshape(N_LAT, D_MODEL), ctx.reshape(N_CTX, D_MODEL))
    cc = jnp.zeros((MOD_ROWS, D_MODEL), F32).at[:BATCH].set(c).at[BATCH].set(c_ctx)
    mods_shape = (MOD_ROWS, 3, 3, 1, D_MODEL)
    mods = _ada_call(cc, w_ada, b_ada, 0).reshape(mods_shape)
    npre = norm_pre.reshape(DEPTH, 3, 1, D_MODEL)
    npost = norm_post.reshape(DEPTH, 3, 1, D_MODEL)
    pool_w16 = pool_w.astype(BF16)
    cos, sina, sinb = _rope_tables()

    ffn_f32 = (ffn_w_gate, ffn_w_up, ffn_w_down)
    ffn_order = [(layer, half) for layer in range(DEPTH) for half in range(2)]
    ffn16 = tuple(w[0, 0].astype(BF16) for w in ffn_f32)
    w_out16 = w_out[0].astype(BF16)

    def next_ffn(layer, half):
        i = ffn_order.index((layer, half)) + 1
        return [(w, ffn_order[i]) for w in ffn_f32] if i < len(ffn_order) else []

    for layer in range(DEPTH):
        update_ctx = layer < DEPTH - 1
        n_tiles = ALL_TILES if update_ctx else LAT_TILES
        jobs = next_ffn(layer, 0)
        if layer == 0:
            jobs, down_job = jobs[:2] + [(w_in, (0,))], jobs[2:]
        s, done = _ffn_call(srcs, mods, npre, npost, ffn16, layer, 0, ALL_TILES, jobs)
        if layer == 0:
            ffn16, w_in16 = done[:2], done[2]
        else:
            ffn16, down_job = done, []
        proj_next = ([(w_in, (layer + 1,)), (w_out, (layer + 1,))] if update_ctx else []) + down_job
        p, u, proj16 = _inproj_call(s, mods, npre, w_in16, cos, sina, sinb, layer, proj_next)
        if down_job:
            ffn16, proj16 = ffn16 + proj16[-1:], proj16[:-1]
        o_a = _win_call(p, attn_sink, layer, update_ctx)
        ada_next = (cc, w_ada, b_ada, layer + 1) if update_ctx else None
        o_c, mods_next = _diff_call(p, diff_lambda, diff_subln, layer, update_ctx, ada_next)
        o_b =[_pool_call(u, pool_w16, pool_scale, layer, SEQ, 0)]
        if update_ctx:
            o_b.append(_pool_call(u, pool_w16, pool_scale, layer, CTX_LEN, N_LAT // CTX_LEN))
        s = _outproj_call(s, mods, npost, o_a, o_b, o_c, w_out16, layer, n_tiles * TM)
        s, ffn16 = _ffn_call((s,), mods, npre, npost, ffn16, layer, 1, n_tiles, next_ffn(layer, 1))
        srcs = (s,)
        if update_ctx:
            w_in16, w_out16 = proj16
            mods = mods_next.reshape(mods_shape)
    return s.reshape(BATCH, SEQ, D_MODEL)
```

```python
import functools
import math

import jax
import jax.numpy as jnp
from jax import lax
from jax.experimental import pallas as pl
from jax.experimental.pallas import tpu as pltpu

D_MODEL = 4096
BATCH = 4
SEQ = 2048
DEPTH = 2
GRID_W = 64
CTX_LEN = 256
HEAD_DIM = 128
ROPE_THETA = 10000.0
EPS = 1e-6
N_MOD = 9
FFN_DIM = 2 * D_MODEL

WIN_Q_HEADS = 12
WIN_KV_HEADS = 4
WIN_GROUP = WIN_Q_HEADS // WIN_KV_HEADS
WIN_WIDTH = WIN_Q_HEADS * HEAD_DIM
WINDOW = 128
BLOCK = 128

POOL_WINDOWS = (2, 4, 8, 16)
POOL_WIDTH = 1024
POOL_GROUP = POOL_WIDTH // len(POOL_WINDOWS)
POOL_HALO = 8

DIFF_HEADS = 6
DIFF_V_DIM = 2 * HEAD_DIM
DIFF_WIDTH = DIFF_HEADS * DIFF_V_DIM
IN_WIDTH = 8192

COL_QA = 0
COL_KA = COL_QA + WIN_WIDTH
COL_VA = COL_KA + WIN_KV_HEADS * HEAD_DIM
COL_U = COL_VA + WIN_KV_HEADS * HEAD_DIM
COL_QD = COL_U + POOL_WIDTH
COL_KD = COL_QD + 2 * DIFF_HEADS * HEAD_DIM
COL_VD = COL_KD + 2 * DIFF_HEADS * HEAD_DIM

F32 = jnp.float32
BF16 = jnp.bfloat16
LANES = 128
SUBLANES = 8
LOG2E = math.log2(math.e)
Q_TO_LOG2 = HEAD_DIM ** -0.5 * LOG2E
ADA_WIDTH = N_MOD * D_MODEL

N_LAT = BATCH * SEQ
N_CTX = BATCH * CTX_LEN
N_ALL = N_LAT + N_CTX

TM = 512
TM_OUT = 256
TF = 512
TF_F32 = 256
TN_IN = 1024
RH = 512
MXU_COLS = 256
TN_ADA = 512
TQ_DIFF = 1024
TQ_SUB = 256
ROW_CHUNK = 16
PASS_CHUNKS = 2
MOD_ROWS = 16
VMEM_LIMIT = 63 * 1024 * 1024

LAT_TILES = N_LAT // TM
ALL_TILES = N_ALL // TM


def _mod_row(t, tm):
    return jnp.where(t < N_LAT // tm, t // (SEQ // tm), BATCH)


def _dot(a, b):
    return jnp.dot(a, b, preferred_element_type=F32)


def _dot_nt(a, b):
    return lax.dot_general(a, b, (((1,), (1,)), ((), ())), preferred_element_type=F32)


def _for_rows(n_rows, chunk, fn, unroll=1):
    def body(i, carry):
        fn(pl.ds(pl.multiple_of(i * chunk, chunk), chunk))
        return carry
    lax.fori_loop(0, n_rows // chunk, body, 0, unroll=unroll)


def _for_row_starts(n_rows, chunk, fn):
    def body(i, carry):
        fn(pl.multiple_of(i * chunk, chunk))
        return carry
    lax.fori_loop(0, n_rows // chunk, body, 0)


def _rstd_pass(load_rows, stat_ref, n_rows):
    def rows(r):
        v = load_rows(r)
        sq = v * v
        parts = [sq[:, k * LANES:(k + 1) * LANES] for k in range(D_MODEL // LANES)]
        while len(parts) > 1:
            parts = [parts[k] + parts[k + 1] for k in range(0, len(parts), 2)]
        total = jnp.sum(parts[0], axis=-1, keepdims=True)
        rstd = lax.rsqrt(total * (1.0 / D_MODEL) + EPS)
        stat_ref[r, :] = jnp.broadcast_to(rstd, (SUBLANES, LANES))
    _for_rows(n_rows, SUBLANES, rows, unroll=min(64, n_rows // SUBLANES))


def _rms(v, g):
    return v * lax.rsqrt(jnp.mean(v * v, axis=-1, keepdims=True) + EPS) * g


def _params(*sem):
    return pltpu.CompilerParams(dimension_semantics=sem, vmem_limit_bytes=VMEM_LIMIT)


def _ada_kernel(c_ref, w_ref, b_ref, o_ref):
    c = c_ref[...]
    s = (c / (1.0 + jnp.exp(-c))).astype(BF16)
    o_ref[...] = _dot(s, w_ref[...].astype(BF16)) + b_ref[...]


def _ada_call(cc, w_ada, b_ada, layer):
    return pl.pallas_call(
        _ada_kernel,
        grid=(ADA_WIDTH // TN_ADA,),
        in_specs=[
            pl.BlockSpec((MOD_ROWS, D_MODEL), lambda j: (0, 0)),
            pl.BlockSpec((None, D_MODEL, TN_ADA), lambda j: (layer, 0, j)),
            pl.BlockSpec((None, 1, TN_ADA), lambda j: (layer, 0, j)),
        ],
        out_specs=pl.BlockSpec((MOD_ROWS, TN_ADA), lambda j: (0, j)),
        out_shape=jax.ShapeDtypeStruct((MOD_ROWS, ADA_WIDTH), F32),
        compiler_params=_params("arbitrary"),
        name=f"adaln_l{layer}",
    )(cc, w_ada, b_ada.reshape(DEPTH, 1, ADA_WIDTH))


def _mod_spec(group, tm, tile0=0):
    return pl.BlockSpec((None, None, 3, 1, D_MODEL), lambda t, *_: (_mod_row(t + tile0, tm), group, 0, 0, 0))


def _norm_spec(layer, k):
    return pl.BlockSpec((None, None, 1, D_MODEL), lambda t, *_: (layer, k, 0, 0))


def _modulated_norm_to(h_ref, s_ref, stat_ref, vec_ref, m_ref, gpre_ref, n_rows, also=None):
    _rstd_pass(lambda r: s_ref[r, :], stat_ref, n_rows)
    full = (ROW_CHUNK, D_MODEL)
    vec_ref[0:ROW_CHUNK, :] = jnp.broadcast_to(gpre_ref[...] * (1.0 + m_ref[1]), full)
    vec_ref[ROW_CHUNK:, :] = jnp.broadcast_to(m_ref[0], full)

    def rows(r0):
        rs = [pl.ds(r0 + c * ROW_CHUNK, ROW_CHUNK) for c in range(PASS_CHUNKS)]
        rstd = [stat_ref[r, :] for r in rs]
        for k in range(D_MODEL // LANES):
            cs = slice(k * LANES, (k + 1) * LANES)
            gain, shift = vec_ref[0:ROW_CHUNK, cs], vec_ref[ROW_CHUNK:, cs]
            for r, rstd_r in zip(rs, rstd):
                h_ref[r, cs] = (s_ref[r, cs] * rstd_r * gain + shift).astype(BF16)
        if also is not None:
            for r in rs:
                also(r)
    _for_row_starts(n_rows, PASS_CHUNKS * ROW_CHUNK, rows)


def _gated_residual_to(o_ref, s_ref, y_ref, stat_ref, vec_ref, gate_gain, n_rows):
    _rstd_pass(lambda r: y_ref[r, :], stat_ref, n_rows)
    vec_ref[0:ROW_CHUNK, :] = jnp.broadcast_to(gate_gain, (ROW_CHUNK, D_MODEL))

    def rows(r):
        rstd = stat_ref[r, :]
        for k in range(D_MODEL // LANES):
            cs = slice(k * LANES, (k + 1) * LANES)
            o_ref[r, cs] = s_ref[r, cs] + y_ref[r, cs] * rstd * vec_ref[0:ROW_CHUNK, cs]
    _for_rows(n_rows, ROW_CHUNK, rows, unroll=2)


def _norm_scratch(n_rows):
    return [pltpu.VMEM((n_rows, LANES), F32), pltpu.VMEM((2 * ROW_CHUNK, D_MODEL), F32)]


def _hosted_cast(arr, lead, n_inner):
    rows, cols = arr.shape[-2:]
    block = (rows // LAT_TILES, cols // n_inner)

    def index(t, i):
        return jnp.minimum(t, LAT_TILES - 1), jnp.where(t < LAT_TILES, i, n_inner - 1)

    in_spec = pl.BlockSpec((None,) * len(lead) + block, lambda t, i: lead + index(t, i))
    return in_spec, pl.BlockSpec(block, index), jax.ShapeDtypeStruct((rows, cols), BF16)


def _ffn_kernel(*refs, n_src, n_cast, n_tiles, tile0, has_head, convert_weights):
    head_hbm, refs = (refs[0], refs[1:]) if has_head else (None, refs)
    srcs, refs = refs[:n_src], refs[n_src:]
    m_ref, gpre_ref, gpost_ref, wg_ref, wu_ref, wd_ref = refs[:6]
    n_conv = 3 if convert_weights else 0
    cast_src, o_hbm = refs[6:6 + n_cast], refs[6 + n_cast]
    cast_dst = refs[7 + n_cast:7 + 2 * n_cast]
    conv_dst = refs[7 + 2 * n_cast:7 + 2 * n_cast + n_conv]
    slots, acc_ref, h_ref, stat_ref, vec_ref, fetch_sem, store_sem, head_sem = refs[7 + 2 * n_cast + n_conv:]
    n_slots = slots.shape[0]

    def head_copy():
        return pltpu.make_async_copy(head_hbm, o_hbm.at[pl.ds(0, tile0 * TM), :], head_sem.at[0])
    t = pl.program_id(0)
    f = pl.program_id(1)
    cur = slots.at[lax.rem(t, n_slots)]

    def tile_rows(ref, tile):
        return ref.at[pl.ds(pl.multiple_of(tile * TM, TM), TM), :]

    def fetch(local, start):
        slot = lax.rem(local, n_slots)
        tile = local + tile0

        def run(src, src_tile):
            copy = pltpu.make_async_copy(tile_rows(src, src_tile), slots.at[slot], fetch_sem.at[slot])
            copy.start() if start else copy.wait()
        if n_src == 1:
            run(srcs[0], tile)
        else:
            pl.when(tile < LAT_TILES)(lambda: run(srcs[0], tile))
            pl.when(tile >= LAT_TILES)(lambda: run(srcs[1], tile - LAT_TILES))

    def store(local):
        slot = lax.rem(local, n_slots)
        return pltpu.make_async_copy(slots.at[slot], tile_rows(o_hbm, local + tile0), store_sem.at[slot])

    @pl.when((t == 0) & (f == 0))
    def _first_fetch():
        fetch(t, start=True)
        if has_head:
            head_copy().start()

    @pl.when(f == 0)
    def _prologue():
        fetch(t, start=False)

        def zero_acc(r):
            acc_ref[r, :] = jnp.zeros((ROW_CHUNK, D_MODEL), F32)
        _modulated_norm_to(h_ref, cur, stat_ref, vec_ref, m_ref, gpre_ref, TM, also=zero_acc)

    @pl.when(f == 1)
    def _refill_other_slot():
        @pl.when(t >= 1)
        def _():
            store(t - 1).wait()

        @pl.when(t + 1 < n_tiles)
        def _():
            fetch(t + 1, start=True)

    if convert_weights:
        for dst, w_ref in zip(conv_dst, (wg_ref, wu_ref, wd_ref)):
            dst[...] = w_ref[...].astype(BF16)
        wg_ref, wu_ref, wd_ref = conv_dst
    h = h_ref[...]
    g = _dot(h, wg_ref[...])
    u = _dot(h, wu_ref[...])
    a = (g / (1.0 + jnp.exp(-g)) * u).astype(BF16)
    for c in range(D_MODEL // 1024):
        cs = slice(c * 1024, (c + 1) * 1024)
        acc_ref[:, cs] += _dot(a, wd_ref[:, cs])
    for src, dst in zip(cast_src, cast_dst):
        dst[...] = src[...].astype(BF16)

    @pl.when(f == pl.num_programs(1) - 1)
    def _epilogue():
        _gated_residual_to(cur, cur, acc_ref, stat_ref, vec_ref, 0.5 * m_ref[2] * gpost_ref[...], TM)
        store(t).start()

        @pl.when(t == n_tiles - 1)
        def _():
            store(t).wait()
            if has_head:
                head_copy().wait()


def _ffn_call(srcs, mods, norm_pre, norm_post, weights, layer, half, n_tiles, cast_next=(), *,
              tile0=0, head=None, f32_weights=False):
    k = 2 * half
    tf = TF_F32 if f32_weights else TF
    n_f = FFN_DIM // tf
    assert n_f >= 2 and (head is None) == (tile0 == 0)
    out_rows = (tile0 + n_tiles) * TM
    hosted = [_hosted_cast(arr, lead, n_f) for arr, lead in cast_next]
    hbm = pl.BlockSpec(memory_space=pl.ANY)
    lead = (layer, half) if f32_weights else ()
    none = (None,) * len(lead)
    w_specs = [
        pl.BlockSpec(none + (D_MODEL, tf), lambda t, f: lead + (0, f)),
        pl.BlockSpec(none + (D_MODEL, tf), lambda t, f: lead + (0, f)),
        pl.BlockSpec(none + (tf, D_MODEL), lambda t, f: lead + (f, 0)),
    ]
    conv_specs, conv_shapes = [], []
    if f32_weights:
        conv_specs = [pl.BlockSpec((D_MODEL, tf), lambda t, f: (0, f)),
                      pl.BlockSpec((D_MODEL, tf), lambda t, f: (0, f)),
                      pl.BlockSpec((tf, D_MODEL), lambda t, f: (f, 0))]
        conv_shapes = [jax.ShapeDtypeStruct(w.shape[-2:], BF16) for w in weights]
    heads = [] if head is None else [head]
    out = pl.pallas_call(
        functools.partial(_ffn_kernel, n_src=len(srcs), n_cast=len(hosted), n_tiles=n_tiles, tile0=tile0,
                          has_head=head is not None, convert_weights=f32_weights),
        grid=(n_tiles, n_f),
        in_specs=[hbm] * (len(heads) + len(srcs)) + [
            _mod_spec(k, TM, tile0),
            _norm_spec(layer, k),
            _norm_spec(layer, k),
        ] + w_specs + [h[0] for h in hosted],
        out_specs=[hbm] + [h[1] for h in hosted] + conv_specs,
        out_shape=[jax.ShapeDtypeStruct((out_rows, D_MODEL), F32)] + [h[2] for h in hosted] + conv_shapes,
        scratch_shapes=[
            pltpu.VMEM((min(2, n_tiles), TM, D_MODEL), F32),
            pltpu.VMEM((TM, D_MODEL), F32),
            pltpu.VMEM((TM, D_MODEL), BF16),
        ] + _norm_scratch(TM) + [pltpu.SemaphoreType.DMA((2,)), pltpu.SemaphoreType.DMA((2,)),
                                 pltpu.SemaphoreType.DMA((1,))],
        compiler_params=_params("arbitrary", "arbitrary"),
        name=f"ffn_l{layer}h{half}" + ("_first" if f32_weights else ""),
    )(*heads, *srcs, mods, norm_pre, norm_post, *weights, *[arr for arr, _ in cast_next])
    return out[0], tuple(out[1:])


def _is_rope_block(jj):
    blk = lambda col: col // RH
    return (jj < blk(COL_VA)) | ((jj >= blk(COL_QD)) & (jj < blk(COL_VD)))


def _is_query_block(jj):
    blk = lambda col: col // RH
    return (jj < blk(COL_KA)) | ((jj >= blk(COL_QD)) & (jj < blk(COL_KD)))


def _inproj_kernel(s_ref, m_ref, gpre_ref, w_ref, cos_ref, sina_ref, sinb_ref, *rest):
    n_cast = (len(rest) - 5) // 2
    cast_src, (p_ref, u_ref) = rest[:n_cast], rest[n_cast:n_cast + 2]
    cast_dst = rest[n_cast + 2:2 * n_cast + 2]
    h_ref, stat_ref, vec_ref = rest[2 * n_cast + 2:]
    t = pl.program_id(0)
    j = pl.program_id(1)

    @pl.when(j == 0)
    def _prologue():
        _modulated_norm_to(h_ref, s_ref, stat_ref, vec_ref, m_ref, gpre_ref, TM)

    for src, dst in zip(cast_src, cast_dst):
        dst[...] = src[...].astype(BF16)

    h = h_ref[...]
    chunks = []
    for c in range(TN_IN // MXU_COLS):
        if c % (RH // MXU_COLS) == 0:
            blk = j * (TN_IN // RH) + c // (RH // MXU_COLS)
            rope = _is_rope_block(blk) & (t < LAT_TILES)
            gain = jnp.where(_is_query_block(blk), Q_TO_LOG2, 1.0)
            cos = jnp.where(rope, cos_ref[...], 1.0) * gain
            sina = jnp.where(rope, sina_ref[...], 0.0) * gain
            sinb = jnp.where(rope, sinb_ref[...], 0.0) * gain
        y = _dot(h, w_ref[:, c * MXU_COLS:(c + 1) * MXU_COLS])
        chunks.append(y)
        for hd in range(MXU_COLS // HEAD_DIM):
            v = y[:, hd * HEAD_DIM:(hd + 1) * HEAD_DIM]
            fwd = pltpu.roll(v, HEAD_DIM // 4, 1)
            bwd = pltpu.roll(v, HEAD_DIM - HEAD_DIM // 4, 1)
            col = c * MXU_COLS + hd * HEAD_DIM
            p_ref[:, col:col + HEAD_DIM] = (v * cos + bwd * sina + fwd * sinb).astype(BF16)

    for step in range(IN_WIDTH // TN_IN):
        for c in range(TN_IN // MXU_COLS):
            col = step * TN_IN + c * MXU_COLS
            if COL_U <= col < COL_QD:
                @pl.when(j == step)
                def _pool_channels(c=c, col=col):
                    u_ref[:, col - COL_U:col - COL_U + MXU_COLS] = chunks[c]


def _inproj_call(s, mods, norm_pre, w_in, cos, sina, sinb, layer, cast_next=()):
    n_j = IN_WIDTH // TN_IN
    hosted = [_hosted_cast(arr, lead, n_j) for arr, lead in cast_next]
    rope_spec = pl.BlockSpec((TM, HEAD_DIM), lambda t, j: (t % (SEQ // TM), 0))
    out = pl.pallas_call(
        _inproj_kernel,
        grid=(ALL_TILES, n_j),
        in_specs=[
            pl.BlockSpec((TM, D_MODEL), lambda t, j: (t, 0)),
            _mod_spec(1, TM),
            _norm_spec(layer, 1),
            pl.BlockSpec((D_MODEL, TN_IN), lambda t, j: (0, j)),
            rope_spec, rope_spec, rope_spec,
        ] + [h[0] for h in hosted],
        out_specs=[
            pl.BlockSpec((TM, TN_IN), lambda t, j: (t, j)),
            pl.BlockSpec((TM, POOL_WIDTH), lambda t, j: (t, 0)),
        ] + [h[1] for h in hosted],
        out_shape=[
            jax.ShapeDtypeStruct((N_ALL, IN_WIDTH), BF16),
            jax.ShapeDtypeStruct((N_ALL, POOL_WIDTH), F32),
        ] + [h[2] for h in hosted],
        scratch_shapes=[pltpu.VMEM((TM, D_MODEL), BF16)] + _norm_scratch(TM),
        compiler_params=_params("arbitrary", "arbitrary"),
        name=f"inproj_l{layer}",
    )(s, mods, norm_pre, w_in, cos, sina, sinb, *[arr for arr, _ in cast_next])
    return out[0], out[1], tuple(out[2:])


LAT_QBLOCKS = SEQ // BLOCK
CTX_QBLOCKS = CTX_LEN // BLOCK
BAND = 3 * BLOCK


def _win_kernel(sink_ref, q_ref, k_ref, v_ref, kc_ref, vc_ref, o_ref, s_ref):
    n = pl.program_id(1)
    row = lax.broadcasted_iota(jnp.int32, (WIN_GROUP * BLOCK, 1), 0)

    def head_slice(i):
        return slice(i * HEAD_DIM, (i + 1) * HEAD_DIM)

    def group_q(h):
        return jnp.concatenate([q_ref[:, head_slice(WIN_GROUP * h + g)] for g in range(WIN_GROUP)], axis=0)

    def group_sink(h):
        s0 = sink_ref[0, WIN_GROUP * h]
        s1 = sink_ref[0, WIN_GROUP * h + 1]
        s2 = sink_ref[0, WIN_GROUP * h + 2]
        return jnp.where(row < BLOCK, s0, jnp.where(row < 2 * BLOCK, s1, s2)) * LOG2E

    def store(h, o):
        for g in range(WIN_GROUP):
            o_ref[:, head_slice(WIN_GROUP * h + g)] = o[g * BLOCK:(g + 1) * BLOCK].astype(BF16)

    @pl.when(n < LAT_QBLOCKS)
    def _latent_queries():
        start = pl.multiple_of(jnp.clip((n - 1) * BLOCK, 0, SEQ - BAND), BLOCK)
        qpos = n * BLOCK + (row & (BLOCK - 1))
        kpos = start + lax.broadcasted_iota(jnp.int32, (1, BAND), 1)
        valid = jnp.abs(kpos - qpos) <= WINDOW

        def logits_to_slot(h, slot):
            q = group_q(h)
            s_ref[slot, :, 0:BAND] = jnp.where(valid, _dot_nt(q, k_ref[pl.ds(start, BAND), head_slice(h)]),
                                               -jnp.inf)
            s_ref[slot, :, BAND:] = _dot_nt(q, kc_ref[:, head_slice(h)])

        logits_to_slot(0, 0)
        for h in range(WIN_KV_HEADS):
            hs = head_slice(h)
            slot = h % 2
            if h + 1 < WIN_KV_HEADS:
                logits_to_slot(h + 1, 1 - slot)
            s = s_ref[slot]
            sink = group_sink(h)
            m = jnp.maximum(jnp.max(s, axis=-1, keepdims=True), sink)
            e = jnp.exp2(s - m)
            den = jnp.sum(e, axis=-1, keepdims=True) + jnp.exp2(sink - m)
            e = e.astype(BF16)
            o = _dot(e[:, 0:BAND], v_ref[pl.ds(start, BAND), hs]) + _dot(e[:, BAND:], vc_ref[:, hs])
            store(h, o * (1.0 / den))

    @pl.when(n >= LAT_QBLOCKS)
    def _context_queries():
        for h in range(WIN_KV_HEADS):
            hs = head_slice(h)
            s_ctx = _dot_nt(group_q(h), kc_ref[:, hs])
            sink = group_sink(h)
            m = jnp.maximum(jnp.max(s_ctx, axis=-1, keepdims=True), sink)
            e_ctx = jnp.exp2(s_ctx - m)
            den = jnp.sum(e_ctx, axis=-1, keepdims=True) + jnp.exp2(sink - m)
            store(h, _dot(e_ctx.astype(BF16), vc_ref[:, hs]) * (1.0 / den))


def _win_call(p, sink, layer, with_ctx_queries):
    nq = LAT_QBLOCKS + (CTX_QBLOCKS if with_ctx_queries else 0)
    kv_cols = WIN_KV_HEADS * HEAD_DIM

    def q_row(b, n):
        return jnp.where(n < LAT_QBLOCKS, b * LAT_QBLOCKS + n,
                         N_LAT // BLOCK + b * CTX_QBLOCKS + n - LAT_QBLOCKS)

    ctx_row = lambda b: N_LAT // CTX_LEN + b
    return pl.pallas_call(
        _win_kernel,
        grid=(BATCH, nq),
        in_specs=[
            pl.BlockSpec(memory_space=pltpu.SMEM),
            pl.BlockSpec((BLOCK, WIN_WIDTH), lambda b, n: (q_row(b, n), 0)),
            pl.BlockSpec((SEQ, kv_cols), lambda b, n: (b, COL_KA // kv_cols)),
            pl.BlockSpec((SEQ, kv_cols), lambda b, n: (b, COL_VA // kv_cols)),
            pl.BlockSpec((CTX_LEN, kv_cols), lambda b, n: (ctx_row(b), COL_KA // kv_cols)),
            pl.BlockSpec((CTX_LEN, kv_cols), lambda b, n: (ctx_row(b), COL_VA // kv_cols)),
        ],
        out_specs=pl.BlockSpec((BLOCK, WIN_WIDTH), lambda b, n: (q_row(b, n), 0)),
        out_shape=jax.ShapeDtypeStruct((N_ALL if with_ctx_queries else N_LAT, WIN_WIDTH), BF16),
        scratch_shapes=[pltpu.VMEM((2, WIN_GROUP * BLOCK, BAND + CTX_LEN), F32)],
        compiler_params=_params("arbitrary", "arbitrary"),
        name=f"win_attn_l{layer}",
    )(sink[layer].reshape(1, WIN_Q_HEADS), p, p, p, p, p)


LAT_QTILES = SEQ // TQ_DIFF


def _diff_kernel(*refs, lam_init, with_ctx_queries, with_ada):
    lam_ref, subln_ref, q_ref, k_ref, v_ref, kc_ref, vc_ref = refs[:7]
    rest = list(refs[7:])
    s_ref = rest.pop(-1)
    qc_ref = rest.pop(0) if with_ctx_queries else None
    ada_in = [rest.pop(0) for _ in range(3)] if with_ada else None
    o_ref = rest.pop(0)
    oc_ref = rest.pop(0) if with_ctx_queries else None
    i = pl.program_id(2)
    if with_ada:
        _ada_kernel(*ada_in, rest.pop(0))
    dl = lam_ref[...]
    lam = (jnp.exp(jnp.sum(dl[0:1] * dl[1:2], axis=-1, keepdims=True))
           - jnp.exp(jnp.sum(dl[2:3] * dl[3:4], axis=-1, keepdims=True)) + lam_init)
    out_gain = subln_ref[...] * (1.0 - lam_init)

    def head_cols(j):
        return slice(j * HEAD_DIM, (j + 1) * HEAD_DIM)

    def mix(logits):
        maps = []
        for s in logits:
            e = jnp.exp2(s - jnp.max(s, axis=-1, keepdims=True))
            maps.append((e, 1.0 / jnp.sum(e, axis=-1, keepdims=True)))
        (e1, inv1), (e2, inv2) = maps
        return (e1 * inv1 - e2 * (lam * inv2)).astype(BF16)

    def sub_norm(o):
        return _rms(o, out_gain).astype(BF16)

    n_sub = TQ_DIFF // TQ_SUB

    def logits_to_slot(k, slot):
        rows = slice(k * TQ_SUB, (k + 1) * TQ_SUB)
        for j in range(2):
            q = q_ref[rows, head_cols(j)]
            s_ref[slot, j, :, 0:SEQ] = _dot_nt(q, k_ref[:, head_cols(j)])
            s_ref[slot, j, :, SEQ:] = _dot_nt(q, kc_ref[:, head_cols(j)])

    logits_to_slot(0, 0)
    for k in range(n_sub):
        slot = k % 2
        if k + 1 < n_sub:
            logits_to_slot(k + 1, 1 - slot)
        a = mix([s_ref[slot, 0], s_ref[slot, 1]])
        o = _dot(a[:, 0:SEQ], v_ref[...]) + _dot(a[:, SEQ:], vc_ref[...])
        o_ref[k * TQ_SUB:(k + 1) * TQ_SUB, :] = sub_norm(o)

    if with_ctx_queries:
        @pl.when(i == LAT_QTILES - 1)
        def _context_queries():
            q = qc_ref[...]
            a = mix([_dot_nt(q[:, head_cols(j)], kc_ref[:, head_cols(j)]) for j in range(2)])
            oc_ref[...] = sub_norm(_dot(a, vc_ref[...]))


def _diff_call(p, diff_lambda, diff_subln, layer, with_ctx_queries, ada_next=None):
    lam_init = 0.8 - 0.6 * math.exp(-0.3 * layer)
    w = DIFF_V_DIM
    ctx_row = lambda b: N_LAT // CTX_LEN + b
    n_steps = BATCH * DIFF_HEADS * LAT_QTILES
    step = lambda b, h, i: (b * DIFF_HEADS + h) * LAT_QTILES + i
    in_specs = [
        pl.BlockSpec((None, 4, HEAD_DIM), lambda b, h, i: (layer, 0, 0)),
        pl.BlockSpec((None, 1, DIFF_V_DIM), lambda b, h, i: (layer, 0, 0)),
        pl.BlockSpec((TQ_DIFF, w), lambda b, h, i: (b * LAT_QTILES + i, COL_QD // w + h)),
        pl.BlockSpec((SEQ, w), lambda b, h, i: (b, COL_KD // w + h)),
        pl.BlockSpec((SEQ, w), lambda b, h, i: (b, COL_VD // w + h)),
        pl.BlockSpec((CTX_LEN, w), lambda b, h, i: (ctx_row(b), COL_KD // w + h)),
        pl.BlockSpec((CTX_LEN, w), lambda b, h, i: (ctx_row(b), COL_VD // w + h)),
    ]
    out_specs = [pl.BlockSpec((TQ_DIFF, w), lambda b, h, i: (b * LAT_QTILES + i, h))]
    out_shape = [jax.ShapeDtypeStruct((N_LAT, DIFF_WIDTH), BF16)]
    operands = [diff_lambda, diff_subln.reshape(DEPTH, 1, DIFF_V_DIM), p, p, p, p, p]
    if with_ctx_queries:
        in_specs.append(pl.BlockSpec((CTX_LEN, w), lambda b, h, i: (ctx_row(b), COL_QD // w + h)))
        out_specs.append(pl.BlockSpec((CTX_LEN, w), lambda b, h, i: (b, h)))
        out_shape.append(jax.ShapeDtypeStruct((N_CTX, DIFF_WIDTH), BF16))
        operands.append(p)
    if ada_next is not None:
        cc, w_ada, b_ada, ada_layer = ada_next
        tn = ADA_WIDTH // n_steps
        in_specs += [
            pl.BlockSpec((MOD_ROWS, D_MODEL), lambda b, h, i: (0, 0)),
            pl.BlockSpec((None, D_MODEL, tn), lambda b, h, i: (ada_layer, 0, step(b, h, i))),
            pl.BlockSpec((None, 1, tn), lambda b, h, i: (ada_layer, 0, step(b, h, i))),
        ]
        out_specs.append(pl.BlockSpec((MOD_ROWS, tn), lambda b, h, i: (0, step(b, h, i))))
        out_shape.append(jax.ShapeDtypeStruct((MOD_ROWS, ADA_WIDTH), F32))
        operands += [cc, w_ada, b_ada.reshape(DEPTH, 1, ADA_WIDTH)]
    out = pl.pallas_call(
        functools.partial(_diff_kernel, lam_init=lam_init, with_ctx_queries=with_ctx_queries,
                          with_ada=ada_next is not None),
        grid=(BATCH, DIFF_HEADS, LAT_QTILES),
        in_specs=in_specs,
        out_specs=out_specs,
        out_shape=out_shape,
        scratch_shapes=[pltpu.VMEM((2, 2, TQ_SUB, SEQ + CTX_LEN), F32)],
        compiler_params=_params("arbitrary", "arbitrary", "arbitrary"),
        name=f"diff_attn_l{layer}",
    )(*operands)
    return (out[:-1], out[-1]) if ada_next is not None else (out, None)


def _pool_kernel(u_ref, w_ref, scale_ref, o_ref, pad_ref, *, n_tok):
    t = lax.broadcasted_iota(jnp.int32, (n_tok, 1), 0)
    zeros = jnp.zeros((POOL_HALO, POOL_GROUP), F32)
    pad_ref[0:POOL_HALO, :] = zeros
    pad_ref[POOL_HALO + n_tok:, :] = zeros
    for g, w in enumerate(POOL_WINDOWS):
        cs = slice(g * POOL_GROUP, (g + 1) * POOL_GROUP)
        u = u_ref[:, cs]
        pad_ref[POOL_HALO:POOL_HALO + n_tok, :] = u
        total = pad_ref[POOL_HALO - w // 2:POOL_HALO - w // 2 + n_tok, :]
        for d in range(1 - w // 2, w // 2):
            total = total + pad_ref[POOL_HALO + d:POOL_HALO + d + n_tok, :]
        count = (jnp.clip(t + w // 2, 0, n_tok) - jnp.clip(t - w // 2, 0, n_tok)).astype(F32)
        centred = (total / count - u).astype(BF16)
        o_ref[:, cs] = (_dot(centred, w_ref[g]) * scale_ref[:, cs]).astype(BF16)


def _pool_call(u, pool_w, pool_scale, layer, n_tok, row_block0):
    return pl.pallas_call(
        functools.partial(_pool_kernel, n_tok=n_tok),
        grid=(BATCH,),
        in_specs=[
            pl.BlockSpec((n_tok, POOL_WIDTH), lambda b: (row_block0 + b, 0)),
            pl.BlockSpec((None, len(POOL_WINDOWS), POOL_GROUP, POOL_GROUP), lambda b: (layer, 0, 0, 0)),
            pl.BlockSpec((None, 1, POOL_WIDTH), lambda b: (layer, 0, 0)),
        ],
        out_specs=pl.BlockSpec((n_tok, POOL_WIDTH), lambda b: (b, 0)),
        out_shape=jax.ShapeDtypeStruct((BATCH * n_tok, POOL_WIDTH), BF16),
        scratch_shapes=[pltpu.VMEM((n_tok + 2 * POOL_HALO, POOL_GROUP), F32)],
        compiler_params=_params("arbitrary"),
        name=f"pool_l{layer}_n{n_tok}",
    )(u, pool_w, pool_scale.reshape(DEPTH, 1, POOL_WIDTH))


def _outproj_kernel(s_ref, m_ref, gpost_ref, oa_ref, *rest):
    w_ref, o_ref, stat_ref, vec_ref = rest[-4:]
    oa, ob, oc = oa_ref[...], rest[0][...], rest[1][...]
    if len(rest) == 8:
        latent = pl.program_id(0) < N_LAT // TM_OUT
        ob = jnp.where(latent, ob, rest[2][...])
        oc = jnp.where(latent, oc, rest[3][...])
    b0, c0 = WIN_WIDTH, WIN_WIDTH + POOL_WIDTH
    for n in range(D_MODEL // 1024):
        cs = slice(n * 1024, (n + 1) * 1024)
        o_ref[:, cs] = _dot(oa, w_ref[0:b0, cs]) + _dot(ob, w_ref[b0:c0, cs]) + _dot(oc, w_ref[c0:, cs])
    _gated_residual_to(o_ref, s_ref, o_ref, stat_ref, vec_ref, m_ref[2] * gpost_ref[...], TM_OUT)


def _outproj_call(s, mods, norm_post, o_a, o_b, o_c, w_out, layer, n_rows):
    lat_tiles = N_LAT // TM_OUT
    row_spec = lambda width: pl.BlockSpec((TM_OUT, width), lambda t: (t, 0))
    lat_spec = lambda width: pl.BlockSpec((TM_OUT, width), lambda t: (jnp.minimum(t, lat_tiles - 1), 0))
    ctx_spec = lambda width: pl.BlockSpec((TM_OUT, width), lambda t: (jnp.maximum(t - lat_tiles, 0), 0))
    mixer_specs = [lat_spec(POOL_WIDTH), lat_spec(DIFF_WIDTH)]
    mixers = [o_b[0], o_c[0]]
    if len(o_b) == 2:
        mixer_specs += [ctx_spec(POOL_WIDTH), ctx_spec(DIFF_WIDTH)]
        mixers += [o_b[1], o_c[1]]
    return pl.pallas_call(
        _outproj_kernel,
        grid=(n_rows // TM_OUT,),
        in_specs=[
            row_spec(D_MODEL),
            _mod_spec(1, TM_OUT),
            _norm_spec(layer, 1),
            row_spec(WIN_WIDTH),
        ] + mixer_specs + [
            pl.BlockSpec((D_MODEL, D_MODEL), lambda t: (0, 0), pipeline_mode=pl.Buffered(1)),
        ],
        out_specs=row_spec(D_MODEL),
        out_shape=jax.ShapeDtypeStruct((n_rows, D_MODEL), F32),
        scratch_shapes=_norm_scratch(TM_OUT),
        compiler_params=_params("arbitrary"),
        name=f"outproj_l{layer}",
    )(s, mods, norm_post, o_a, *mixers, w_out)


def _rope_tables():
    row = jnp.repeat(jnp.arange(SEQ // GRID_W, dtype=F32), GRID_W)
    col = jnp.tile(jnp.arange(GRID_W, dtype=F32), SEQ // GRID_W)
    half = HEAD_DIM // 2
    inv = ROPE_THETA ** (-jnp.arange(0, half, 2, dtype=F32) / half)
    ar = row[:, None] * inv
    ac = col[:, None] * inv
    ang = jnp.concatenate([ar, ar, ac, ac], axis=-1)
    cos, sin = jnp.cos(ang), jnp.sin(ang)
    low = (jnp.arange(HEAD_DIM) % half) < half // 2
    return cos, jnp.where(low, -sin, 0.0), jnp.where(low, 0.0, sin)


def kernel(x, c, ctx, c_ctx, w_ada, b_ada, norm_pre, norm_post, ffn_w_gate, ffn_w_up, ffn_w_down,
           w_in, w_out, attn_sink, pool_w, pool_scale, diff_lambda, diff_subln):
    srcs = (x.reshape(N_LAT, D_MODEL), ctx.reshape(N_CTX, D_MODEL))
    cc = jnp.zeros((MOD_ROWS, D_MODEL), F32).at[:BATCH].set(c).at[BATCH].set(c_ctx)
    mods_shape = (MOD_ROWS, 3, 3, 1, D_MODEL)
    mods = _ada_call(cc, w_ada, b_ada, 0).reshape(mods_shape)
    npre = norm_pre.reshape(DEPTH, 3, 1, D_MODEL)
    npost = norm_post.reshape(DEPTH, 3, 1, D_MODEL)
    pool_w16 = pool_w.astype(BF16)
    cos, sina, sinb = _rope_tables()

    ffn_f32 = (ffn_w_gate, ffn_w_up, ffn_w_down)
    ffn_order = [(layer, half) for layer in range(DEPTH) for half in range(2)]
    w_out16 = w_out[0].astype(BF16)

    def next_ffn(layer, half):
        i = ffn_order.index((layer, half)) + 1
        return [(w, ffn_order[i]) for w in ffn_f32] if i < len(ffn_order) else []

    for layer in range(DEPTH):
        update_ctx = layer < DEPTH - 1
        n_tiles = ALL_TILES if update_ctx else LAT_TILES
        jobs = next_ffn(layer, 0)
        if layer == 0:
            jobs, down_job = jobs[:2] + [(w_in, (0,))], jobs[2:]
            first, ffn16 = _ffn_call(srcs[:1], mods, npre, npost, ffn_f32, 0, 0, 1, f32_weights=True)
            s, done = _ffn_call(srcs, mods, npre, npost, ffn16, 0, 0, ALL_TILES - 1, jobs, tile0=1, head=first)
            ffn16, w_in16 = done[:2], done[2]
        else:
            s, ffn16 = _ffn_call(srcs, mods, npre, npost, ffn16, layer, 0, ALL_TILES, jobs)
            down_job = []
        proj_next = ([(w_in, (layer + 1,)), (w_out, (layer + 1,))] if update_ctx else []) + down_job
        p, u, proj16 = _inproj_call(s, mods, npre, w_in16, cos, sina, sinb, layer, proj_next)
        if down_job:
            ffn16, proj16 = ffn16 + proj16[-1:], proj16[:-1]
        o_a = _win_call(p, attn_sink, layer, update_ctx)
        ada_next = (cc, w_ada, b_ada, layer + 1) if update_ctx else None
        o_c, mods_next = _diff_call(p, diff_lambda, diff_subln, layer, update_ctx, ada_next)
        o_b =[_pool_call(u, pool_w16, pool_scale, layer, SEQ, 0)]
        if update_ctx:
            o_b.append(_pool_call(u, pool_w16, pool_scale, layer, CTX_LEN, N_LAT // CTX_LEN))
        s = _outproj_call(s, mods, npost, o_a, o_b, o_c, w_out16, layer, n_tiles * TM)
        s, ffn16 = _ffn_call((s,), mods, npre, npost, ffn16, layer, 1, n_tiles, next_ffn(layer, 1))
        srcs = (s,)
        if update_ctx:
            w_in16, w_out16 = proj16
            mods = mods_next.reshape(mods_shape)
    return s.reshape(BATCH, SEQ, D_MODEL)
```

```python
import functools
import math

import jax
import jax.numpy as jnp
from jax import lax
from jax.experimental import pallas as pl
from jax.experimental.pallas import tpu as pltpu

D_MODEL = 4096
BATCH = 4
SEQ = 2048
DEPTH = 2
GRID_W = 64
CTX_LEN = 256
HEAD_DIM = 128
ROPE_THETA = 10000.0
EPS = 1e-6
N_MOD = 9
FFN_DIM = 2 * D_MODEL

WIN_Q_HEADS = 12
WIN_KV_HEADS = 4
WIN_GROUP = WIN_Q_HEADS // WIN_KV_HEADS
WIN_WIDTH = WIN_Q_HEADS * HEAD_DIM
WINDOW = 128
BLOCK = 128

POOL_WINDOWS = (2, 4, 8, 16)
POOL_WIDTH = 1024
POOL_GROUP = POOL_WIDTH // len(POOL_WINDOWS)
POOL_HALO = 8

DIFF_HEADS = 6
DIFF_V_DIM = 2 * HEAD_DIM
DIFF_WIDTH = DIFF_HEADS * DIFF_V_DIM
IN_WIDTH = 8192

COL_QA = 0
COL_KA = COL_QA + WIN_WIDTH
COL_VA = COL_KA + WIN_KV_HEADS * HEAD_DIM
COL_U = COL_VA + WIN_KV_HEADS * HEAD_DIM
COL_QD = COL_U + POOL_WIDTH
COL_KD = COL_QD + 2 * DIFF_HEADS * HEAD_DIM
COL_VD = COL_KD + 2 * DIFF_HEADS * HEAD_DIM

F32 = jnp.float32
BF16 = jnp.bfloat16
LANES = 128
SUBLANES = 8
LOG2E = math.log2(math.e)
Q_TO_LOG2 = HEAD_DIM ** -0.5 * LOG2E
ADA_WIDTH = N_MOD * D_MODEL

N_LAT = BATCH * SEQ
N_CTX = BATCH * CTX_LEN
N_ALL = N_LAT + N_CTX

TM = 512
TM_OUT = 256
TF = 512
TF_F32 = 256
TN_IN = 1024
RH = 512
MXU_COLS = 256
TN_ADA = 512
TQ_DIFF = 1024
TQ_SUB = 256
ROW_CHUNK = 16
PASS_CHUNKS = 2
MOD_ROWS = 16
VMEM_LIMIT = 63 * 1024 * 1024
RING_DMA_THREAD = 1

LAT_TILES = N_LAT // TM
ALL_TILES = N_ALL // TM


def _mod_row(t, tm):
    return jnp.where(t < N_LAT // tm, t // (SEQ // tm), BATCH)


def _dot(a, b):
    return jnp.dot(a, b, preferred_element_type=F32)


def _dot_nt(a, b):
    return lax.dot_general(a, b, (((1,), (1,)), ((), ())), preferred_element_type=F32)


def _for_rows(n_rows, chunk, fn, unroll=1):
    def body(i, carry):
        fn(pl.ds(pl.multiple_of(i * chunk, chunk), chunk))
        return carry
    lax.fori_loop(0, n_rows // chunk, body, 0, unroll=unroll)


def _for_row_starts(n_rows, chunk, fn):
    def body(i, carry):
        fn(pl.multiple_of(i * chunk, chunk))
        return carry
    lax.fori_loop(0, n_rows // chunk, body, 0)


def _rstd_pass(load_rows, stat_ref, n_rows):
    def rows(r):
        v = load_rows(r)
        sq = v * v
        parts = [sq[:, k * LANES:(k + 1) * LANES] for k in range(D_MODEL // LANES)]
        while len(parts) > 1:
            parts = [parts[k] + parts[k + 1] for k in range(0, len(parts), 2)]
        total = jnp.sum(parts[0], axis=-1, keepdims=True)
        rstd = lax.rsqrt(total * (1.0 / D_MODEL) + EPS)
        stat_ref[r, :] = jnp.broadcast_to(rstd, (SUBLANES, LANES))
    _for_rows(n_rows, SUBLANES, rows, unroll=min(64, n_rows // SUBLANES))


def _rms(v, g):
    return v * lax.rsqrt(jnp.mean(v * v, axis=-1, keepdims=True) + EPS) * g


def _params(*sem):
    return pltpu.CompilerParams(dimension_semantics=sem, vmem_limit_bytes=VMEM_LIMIT)


def _ada_kernel(c_ref, w_ref, b_ref, o_ref):
    c = c_ref[...]
    s = (c / (1.0 + jnp.exp(-c))).astype(BF16)
    o_ref[...] = _dot(s, w_ref[...].astype(BF16)) + b_ref[...]


def _ada_call(cc, w_ada, b_ada, layer):
    return pl.pallas_call(
        _ada_kernel,
        grid=(ADA_WIDTH // TN_ADA,),
        in_specs=[
            pl.BlockSpec((MOD_ROWS, D_MODEL), lambda j: (0, 0)),
            pl.BlockSpec((None, D_MODEL, TN_ADA), lambda j: (layer, 0, j)),
            pl.BlockSpec((None, 1, TN_ADA), lambda j: (layer, 0, j)),
        ],
        out_specs=pl.BlockSpec((MOD_ROWS, TN_ADA), lambda j: (0, j)),
        out_shape=jax.ShapeDtypeStruct((MOD_ROWS, ADA_WIDTH), F32),
        compiler_params=_params("arbitrary"),
        name=f"adaln_l{layer}",
    )(cc, w_ada, b_ada.reshape(DEPTH, 1, ADA_WIDTH))


def _mod_spec(group, tm, tile0=0):
    return pl.BlockSpec((None, None, 3, 1, D_MODEL), lambda t, *_: (_mod_row(t + tile0, tm), group, 0, 0, 0))


def _norm_spec(layer, k):
    return pl.BlockSpec((None, None, 1, D_MODEL), lambda t, *_: (layer, k, 0, 0))


def _modulated_norm_to(h_ref, s_ref, stat_ref, vec_ref, m_ref, gpre_ref, n_rows, also=None):
    _rstd_pass(lambda r: s_ref[r, :], stat_ref, n_rows)
    full = (ROW_CHUNK, D_MODEL)
    vec_ref[0:ROW_CHUNK, :] = jnp.broadcast_to(gpre_ref[...] * (1.0 + m_ref[1]), full)
    vec_ref[ROW_CHUNK:, :] = jnp.broadcast_to(m_ref[0], full)

    def rows(r0):
        rs = [pl.ds(r0 + c * ROW_CHUNK, ROW_CHUNK) for c in range(PASS_CHUNKS)]
        rstd = [stat_ref[r, :] for r in rs]
        for k in range(D_MODEL // LANES):
            cs = slice(k * LANES, (k + 1) * LANES)
            gain, shift = vec_ref[0:ROW_CHUNK, cs], vec_ref[ROW_CHUNK:, cs]
            for r, rstd_r in zip(rs, rstd):
                h_ref[r, cs] = (s_ref[r, cs] * rstd_r * gain + shift).astype(BF16)
        if also is not None:
            for r in rs:
                also(r)
    _for_row_starts(n_rows, PASS_CHUNKS * ROW_CHUNK, rows)


def _gated_residual_to(o_ref, s_ref, y_ref, stat_ref, vec_ref, gate_gain, n_rows):
    _rstd_pass(lambda r: y_ref[r, :], stat_ref, n_rows)
    vec_ref[0:ROW_CHUNK, :] = jnp.broadcast_to(gate_gain, (ROW_CHUNK, D_MODEL))

    def rows(r):
        rstd = stat_ref[r, :]
        for k in range(D_MODEL // LANES):
            cs = slice(k * LANES, (k + 1) * LANES)
            o_ref[r, cs] = s_ref[r, cs] + y_ref[r, cs] * rstd * vec_ref[0:ROW_CHUNK, cs]
    _for_rows(n_rows, ROW_CHUNK, rows, unroll=2)


def _norm_scratch(n_rows):
    return [pltpu.VMEM((n_rows, LANES), F32), pltpu.VMEM((2 * ROW_CHUNK, D_MODEL), F32)]


def _hosted_cast(arr, lead, n_inner):
    rows, cols = arr.shape[-2:]
    block = (rows // LAT_TILES, cols // n_inner)

    def index(t, i):
        return jnp.minimum(t, LAT_TILES - 1), jnp.where(t < LAT_TILES, i, n_inner - 1)

    in_spec = pl.BlockSpec((None,) * len(lead) + block, lambda t, i: lead + index(t, i))
    return in_spec, pl.BlockSpec(block, index), jax.ShapeDtypeStruct((rows, cols), BF16)


def _ffn_kernel(*refs, n_src, n_cast, n_tiles, tile0, has_head, convert_weights):
    head_hbm, refs = (refs[0], refs[1:]) if has_head else (None, refs)
    srcs, refs = refs[:n_src], refs[n_src:]
    m_ref, gpre_ref, gpost_ref, wg_ref, wu_ref, wd_ref = refs[:6]
    n_conv = 3 if convert_weights else 0
    cast_src, o_hbm = refs[6:6 + n_cast], refs[6 + n_cast]
    cast_dst = refs[7 + n_cast:7 + 2 * n_cast]
    conv_dst = refs[7 + 2 * n_cast:7 + 2 * n_cast + n_conv]
    slots, acc_ref, h_ref, stat_ref, vec_ref, fetch_sem, store_sem, head_sem = refs[7 + 2 * n_cast + n_conv:]
    n_slots = slots.shape[0]

    def head_copy():
        return pltpu.make_async_copy(head_hbm, o_hbm.at[pl.ds(0, tile0 * TM), :], head_sem.at[0])
    t = pl.program_id(0)
    f = pl.program_id(1)
    cur = slots.at[lax.rem(t, n_slots)]

    def tile_rows(ref, tile):
        return ref.at[pl.ds(pl.multiple_of(tile * TM, TM), TM), :]

    def fetch(local, start):
        slot = lax.rem(local, n_slots)
        tile = local + tile0

        def run(src, src_tile):
            copy = pltpu.make_async_copy(tile_rows(src, src_tile), slots.at[slot], fetch_sem.at[slot])
            copy.start(priority=RING_DMA_THREAD) if start else copy.wait()
        if n_src == 1:
            run(srcs[0], tile)
        else:
            pl.when(tile < LAT_TILES)(lambda: run(srcs[0], tile))
            pl.when(tile >= LAT_TILES)(lambda: run(srcs[1], tile - LAT_TILES))

    def store(local):
        slot = lax.rem(local, n_slots)
        return pltpu.make_async_copy(slots.at[slot], tile_rows(o_hbm, local + tile0), store_sem.at[slot])

    @pl.when((t == 0) & (f == 0))
    def _first_fetch():
        fetch(t, start=True)
        if has_head:
            head_copy().start()

    @pl.when(f == 0)
    def _prologue():
        fetch(t, start=False)

        def zero_acc(r):
            acc_ref[r, :] = jnp.zeros((ROW_CHUNK, D_MODEL), F32)
        _modulated_norm_to(h_ref, cur, stat_ref, vec_ref, m_ref, gpre_ref, TM, also=zero_acc)

    @pl.when(f == 1)
    def _refill_other_slot():
        @pl.when(t >= 1)
        def _():
            store(t - 1).wait()

        @pl.when(t + 1 < n_tiles)
        def _():
            fetch(t + 1, start=True)

    if convert_weights:
        for dst, w_ref in zip(conv_dst, (wg_ref, wu_ref, wd_ref)):
            dst[...] = w_ref[...].astype(BF16)
        wg_ref, wu_ref, wd_ref = conv_dst
    h = h_ref[...]
    g = _dot(h, wg_ref[...])
    u = _dot(h, wu_ref[...])
    a = (g / (1.0 + jnp.exp(-g)) * u).astype(BF16)
    for c in range(D_MODEL // 1024):
        cs = slice(c * 1024, (c + 1) * 1024)
        acc_ref[:, cs] += _dot(a, wd_ref[:, cs])
    for src, dst in zip(cast_src, cast_dst):
        dst[...] = src[...].astype(BF16)

    @pl.when(f == pl.num_programs(1) - 1)
    def _epilogue():
        _gated_residual_to(cur, cur, acc_ref, stat_ref, vec_ref, 0.5 * m_ref[2] * gpost_ref[...], TM)
        store(t).start(priority=RING_DMA_THREAD)

        @pl.when(t == n_tiles - 1)
        def _():
            store(t).wait()
            if has_head:
                head_copy().wait()


def _ffn_call(srcs, mods, norm_pre, norm_post, weights, layer, half, n_tiles, cast_next=(), *,
              tile0=0, head=None, f32_weights=False):
    k = 2 * half
    tf = TF_F32 if f32_weights else TF
    n_f = FFN_DIM // tf
    assert n_f >= 2 and (head is None) == (tile0 == 0)
    out_rows = (tile0 + n_tiles) * TM
    hosted = [_hosted_cast(arr, lead, n_f) for arr, lead in cast_next]
    hbm = pl.BlockSpec(memory_space=pl.ANY)
    lead = (layer, half) if f32_weights else ()
    none = (None,) * len(lead)
    w_specs = [
        pl.BlockSpec(none + (D_MODEL, tf), lambda t, f: lead + (0, f)),
        pl.BlockSpec(none + (D_MODEL, tf), lambda t, f: lead + (0, f)),
        pl.BlockSpec(none + (tf, D_MODEL), lambda t, f: lead + (f, 0)),
    ]
    conv_specs, conv_shapes = [], []
    if f32_weights:
        conv_specs = [pl.BlockSpec((D_MODEL, tf), lambda t, f: (0, f)),
                      pl.BlockSpec((D_MODEL, tf), lambda t, f: (0, f)),
                      pl.BlockSpec((tf, D_MODEL), lambda t, f: (f, 0))]
        conv_shapes = [jax.ShapeDtypeStruct(w.shape[-2:], BF16) for w in weights]
    heads = [] if head is None else [head]
    out = pl.pallas_call(
        functools.partial(_ffn_kernel, n_src=len(srcs), n_cast=len(hosted), n_tiles=n_tiles, tile0=tile0,
                          has_head=head is not None, convert_weights=f32_weights),
        grid=(n_tiles, n_f),
        in_specs=[hbm] * (len(heads) + len(srcs)) + [
            _mod_spec(k, TM, tile0),
            _norm_spec(layer, k),
            _norm_spec(layer, k),
        ] + w_specs + [h[0] for h in hosted],
        out_specs=[hbm] + [h[1] for h in hosted] + conv_specs,
        out_shape=[jax.ShapeDtypeStruct((out_rows, D_MODEL), F32)] + [h[2] for h in hosted] + conv_shapes,
        scratch_shapes=[
            pltpu.VMEM((min(2, n_tiles), TM, D_MODEL), F32),
            pltpu.VMEM((TM, D_MODEL), F32),
            pltpu.VMEM((TM, D_MODEL), BF16),
        ] + _norm_scratch(TM) + [pltpu.SemaphoreType.DMA((2,)), pltpu.SemaphoreType.DMA((2,)),
                                 pltpu.SemaphoreType.DMA((1,))],
        compiler_params=_params("arbitrary", "arbitrary"),
        name=f"ffn_l{layer}h{half}" + ("_first" if f32_weights else ""),
    )(*heads, *srcs, mods, norm_pre, norm_post, *weights, *[arr for arr, _ in cast_next])
    return out[0], tuple(out[1:])


def _is_rope_block(jj):
    blk = lambda col: col // RH
    return (jj < blk(COL_VA)) | ((jj >= blk(COL_QD)) & (jj < blk(COL_VD)))


def _is_query_block(jj):
    blk = lambda col: col // RH
    return (jj < blk(COL_KA)) | ((jj >= blk(COL_QD)) & (jj < blk(COL_KD)))


def _inproj_kernel(s_ref, m_ref, gpre_ref, w_ref, cos_ref, sina_ref, sinb_ref, *rest):
    n_cast = (len(rest) - 5) // 2
    cast_src, (p_ref, u_ref) = rest[:n_cast], rest[n_cast:n_cast + 2]
    cast_dst = rest[n_cast + 2:2 * n_cast + 2]
    h_ref, stat_ref, vec_ref = rest[2 * n_cast + 2:]
    t = pl.program_id(0)
    j = pl.program_id(1)

    @pl.when(j == 0)
    def _prologue():
        _modulated_norm_to(h_ref, s_ref, stat_ref, vec_ref, m_ref, gpre_ref, TM)

    for src, dst in zip(cast_src, cast_dst):
        dst[...] = src[...].astype(BF16)

    h = h_ref[...]
    chunks = []
    for c in range(TN_IN // MXU_COLS):
        if c % (RH // MXU_COLS) == 0:
            blk = j * (TN_IN // RH) + c // (RH // MXU_COLS)
            rope = _is_rope_block(blk) & (t < LAT_TILES)
            gain = jnp.where(_is_query_block(blk), Q_TO_LOG2, 1.0)
            cos = jnp.where(rope, cos_ref[...], 1.0) * gain
            sina = jnp.where(rope, sina_ref[...], 0.0) * gain
            sinb = jnp.where(rope, sinb_ref[...], 0.0) * gain
        y = _dot(h, w_ref[:, c * MXU_COLS:(c + 1) * MXU_COLS])
        chunks.append(y)
        for hd in range(MXU_COLS // HEAD_DIM):
            v = y[:, hd * HEAD_DIM:(hd + 1) * HEAD_DIM]
            fwd = pltpu.roll(v, HEAD_DIM // 4, 1)
            bwd = pltpu.roll(v, HEAD_DIM - HEAD_DIM // 4, 1)
            col = c * MXU_COLS + hd * HEAD_DIM
            p_ref[:, col:col + HEAD_DIM] = (v * cos + bwd * sina + fwd * sinb).astype(BF16)

    for step in range(IN_WIDTH // TN_IN):
        for c in range(TN_IN // MXU_COLS):
            col = step * TN_IN + c * MXU_COLS
            if COL_U <= col < COL_QD:
                @pl.when(j == step)
                def _pool_channels(c=c, col=col):
                    u_ref[:, col - COL_U:col - COL_U + MXU_COLS] = chunks[c]


def _inproj_call(s, mods, norm_pre, w_in, cos, sina, sinb, layer, cast_next=()):
    n_j = IN_WIDTH // TN_IN
    hosted = [_hosted_cast(arr, lead, n_j) for arr, lead in cast_next]
    rope_spec = pl.BlockSpec((TM, HEAD_DIM), lambda t, j: (t % (SEQ // TM), 0))
    out = pl.pallas_call(
        _inproj_kernel,
        grid=(ALL_TILES, n_j),
        in_specs=[
            pl.BlockSpec((TM, D_MODEL), lambda t, j: (t, 0)),
            _mod_spec(1, TM),
            _norm_spec(layer, 1),
            pl.BlockSpec((D_MODEL, TN_IN), lambda t, j: (0, j)),
            rope_spec, rope_spec, rope_spec,
        ] + [h[0] for h in hosted],
        out_specs=[
            pl.BlockSpec((TM, TN_IN), lambda t, j: (t, j)),
            pl.BlockSpec((TM, POOL_WIDTH), lambda t, j: (t, 0)),
        ] + [h[1] for h in hosted],
        out_shape=[
            jax.ShapeDtypeStruct((N_ALL, IN_WIDTH), BF16),
            jax.ShapeDtypeStruct((N_ALL, POOL_WIDTH), F32),
        ] + [h[2] for h in hosted],
        scratch_shapes=[pltpu.VMEM((TM, D_MODEL), BF16)] + _norm_scratch(TM),
        compiler_params=_params("arbitrary", "arbitrary"),
        name=f"inproj_l{layer}",
    )(s, mods, norm_pre, w_in, cos, sina, sinb, *[arr for arr, _ in cast_next])
    return out[0], out[1], tuple(out[2:])


LAT_QBLOCKS = SEQ // BLOCK
CTX_QBLOCKS = CTX_LEN // BLOCK
BAND = 3 * BLOCK


def _win_kernel(sink_ref, q_ref, k_ref, v_ref, kc_ref, vc_ref, o_ref, s_ref):
    n = pl.program_id(1)
    row = lax.broadcasted_iota(jnp.int32, (WIN_GROUP * BLOCK, 1), 0)

    def head_slice(i):
        return slice(i * HEAD_DIM, (i + 1) * HEAD_DIM)

    def group_q(h):
        return jnp.concatenate([q_ref[:, head_slice(WIN_GROUP * h + g)] for g in range(WIN_GROUP)], axis=0)

    def group_sink(h):
        s0 = sink_ref[0, WIN_GROUP * h]
        s1 = sink_ref[0, WIN_GROUP * h + 1]
        s2 = sink_ref[0, WIN_GROUP * h + 2]
        return jnp.where(row < BLOCK, s0, jnp.where(row < 2 * BLOCK, s1, s2)) * LOG2E

    def store(h, o):
        for g in range(WIN_GROUP):
            o_ref[:, head_slice(WIN_GROUP * h + g)] = o[g * BLOCK:(g + 1) * BLOCK].astype(BF16)

    @pl.when(n < LAT_QBLOCKS)
    def _latent_queries():
        start = pl.multiple_of(jnp.clip((n - 1) * BLOCK, 0, SEQ - BAND), BLOCK)
        qpos = n * BLOCK + (row & (BLOCK - 1))
        kpos = start + lax.broadcasted_iota(jnp.int32, (1, BAND), 1)
        valid = jnp.abs(kpos - qpos) <= WINDOW

        def logits_to_slot(h, slot):
            q = group_q(h)
            s_ref[slot, :, 0:BAND] = jnp.where(valid, _dot_nt(q, k_ref[pl.ds(start, BAND), head_slice(h)]),
                                               -jnp.inf)
            s_ref[slot, :, BAND:] = _dot_nt(q, kc_ref[:, head_slice(h)])

        logits_to_slot(0, 0)
        for h in range(WIN_KV_HEADS):
            hs = head_slice(h)
            slot = h % 2
            if h + 1 < WIN_KV_HEADS:
                logits_to_slot(h + 1, 1 - slot)
            s = s_ref[slot]
            sink = group_sink(h)
            m = jnp.maximum(jnp.max(s, axis=-1, keepdims=True), sink)
            e = jnp.exp2(s - m)
            den = jnp.sum(e, axis=-1, keepdims=True) + jnp.exp2(sink - m)
            e = e.astype(BF16)
            o = _dot(e[:, 0:BAND], v_ref[pl.ds(start, BAND), hs]) + _dot(e[:, BAND:], vc_ref[:, hs])
            store(h, o * (1.0 / den))

    @pl.when(n >= LAT_QBLOCKS)
    def _context_queries():
        for h in range(WIN_KV_HEADS):
            hs = head_slice(h)
            s_ctx = _dot_nt(group_q(h), kc_ref[:, hs])
            sink = group_sink(h)
            m = jnp.maximum(jnp.max(s_ctx, axis=-1, keepdims=True), sink)
            e_ctx = jnp.exp2(s_ctx - m)
            den = jnp.sum(e_ctx, axis=-1, keepdims=True) + jnp.exp2(sink - m)
            store(h, _dot(e_ctx.astype(BF16), vc_ref[:, hs]) * (1.0 / den))


def _win_call(p, sink, layer, with_ctx_queries):
    nq = LAT_QBLOCKS + (CTX_QBLOCKS if with_ctx_queries else 0)
    kv_cols = WIN_KV_HEADS * HEAD_DIM

    def q_row(b, n):
        return jnp.where(n < LAT_QBLOCKS, b * LAT_QBLOCKS + n,
                         N_LAT // BLOCK + b * CTX_QBLOCKS + n - LAT_QBLOCKS)

    ctx_row = lambda b: N_LAT // CTX_LEN + b
    return pl.pallas_call(
        _win_kernel,
        grid=(BATCH, nq),
        in_specs=[
            pl.BlockSpec(memory_space=pltpu.SMEM),
            pl.BlockSpec((BLOCK, WIN_WIDTH), lambda b, n: (q_row(b, n), 0)),
            pl.BlockSpec((SEQ, kv_cols), lambda b, n: (b, COL_KA // kv_cols)),
            pl.BlockSpec((SEQ, kv_cols), lambda b, n: (b, COL_VA // kv_cols)),
            pl.BlockSpec((CTX_LEN, kv_cols), lambda b, n: (ctx_row(b), COL_KA // kv_cols)),
            pl.BlockSpec((CTX_LEN, kv_cols), lambda b, n: (ctx_row(b), COL_VA // kv_cols)),
        ],
        out_specs=pl.BlockSpec((BLOCK, WIN_WIDTH), lambda b, n: (q_row(b, n), 0)),
        out_shape=jax.ShapeDtypeStruct((N_ALL if with_ctx_queries else N_LAT, WIN_WIDTH), BF16),
        scratch_shapes=[pltpu.VMEM((2, WIN_GROUP * BLOCK, BAND + CTX_LEN), F32)],
        compiler_params=_params("arbitrary", "arbitrary"),
        name=f"win_attn_l{layer}",
    )(sink[layer].reshape(1, WIN_Q_HEADS), p, p, p, p, p)


LAT_QTILES = SEQ // TQ_DIFF


def _diff_kernel(*refs, lam_init, with_ctx_queries, with_ada):
    lam_ref, subln_ref, q_ref, k_ref, v_ref, kc_ref, vc_ref = refs[:7]
    rest = list(refs[7:])
    s_ref = rest.pop(-1)
    qc_ref = rest.pop(0) if with_ctx_queries else None
    ada_in = [rest.pop(0) for _ in range(3)] if with_ada else None
    o_ref = rest.pop(0)
    oc_ref = rest.pop(0) if with_ctx_queries else None
    i = pl.program_id(2)
    if with_ada:
        _ada_kernel(*ada_in, rest.pop(0))
    dl = lam_ref[...]
    lam = (jnp.exp(jnp.sum(dl[0:1] * dl[1:2], axis=-1, keepdims=True))
           - jnp.exp(jnp.sum(dl[2:3] * dl[3:4], axis=-1, keepdims=True)) + lam_init)
    out_gain = subln_ref[...] * (1.0 - lam_init)

    def head_cols(j):
        return slice(j * HEAD_DIM, (j + 1) * HEAD_DIM)

    def mix(logits):
        maps = []
        for s in logits:
            e = jnp.exp2(s - jnp.max(s, axis=-1, keepdims=True))
            maps.append((e, 1.0 / jnp.sum(e, axis=-1, keepdims=True)))
        (e1, inv1), (e2, inv2) = maps
        return (e1 * inv1 - e2 * (lam * inv2)).astype(BF16)

    def sub_norm(o):
        return _rms(o, out_gain).astype(BF16)

    n_sub = TQ_DIFF // TQ_SUB

    def logits_to_slot(k, slot):
        rows = slice(k * TQ_SUB, (k + 1) * TQ_SUB)
        for j in range(2):
            q = q_ref[rows, head_cols(j)]
            s_ref[slot, j, :, 0:SEQ] = _dot_nt(q, k_ref[:, head_cols(j)])
            s_ref[slot, j, :, SEQ:] = _dot_nt(q, kc_ref[:, head_cols(j)])

    logits_to_slot(0, 0)
    for k in range(n_sub):
        slot = k % 2
        if k + 1 < n_sub:
            logits_to_slot(k + 1, 1 - slot)
        a = mix([s_ref[slot, 0], s_ref[slot, 1]])
        o = _dot(a[:, 0:SEQ], v_ref[...]) + _dot(a[:, SEQ:], vc_ref[...])
        o_ref[k * TQ_SUB:(k + 1) * TQ_SUB, :] = sub_norm(o)

    if with_ctx_queries:
        @pl.when(i == LAT_QTILES - 1)
        def _context_queries():
            q = qc_ref[...]
            a = mix([_dot_nt(q[:, head_cols(j)], kc_ref[:, head_cols(j)]) for j in range(2)])
            oc_ref[...] = sub_norm(_dot(a, vc_ref[...]))


def _diff_call(p, diff_lambda, diff_subln, layer, with_ctx_queries, ada_next=None):
    lam_init = 0.8 - 0.6 * math.exp(-0.3 * layer)
    w = DIFF_V_DIM
    ctx_row = lambda b: N_LAT // CTX_LEN + b
    n_steps = BATCH * DIFF_HEADS * LAT_QTILES
    step = lambda b, h, i: (b * DIFF_HEADS + h) * LAT_QTILES + i
    in_specs = [
        pl.BlockSpec((None, 4, HEAD_DIM), lambda b, h, i: (layer, 0, 0)),
        pl.BlockSpec((None, 1, DIFF_V_DIM), lambda b, h, i: (layer, 0, 0)),
        pl.BlockSpec((TQ_DIFF, w), lambda b, h, i: (b * LAT_QTILES + i, COL_QD // w + h)),
        pl.BlockSpec((SEQ, w), lambda b, h, i: (b, COL_KD // w + h)),
        pl.BlockSpec((SEQ, w), lambda b, h, i: (b, COL_VD // w + h)),
        pl.BlockSpec((CTX_LEN, w), lambda b, h, i: (ctx_row(b), COL_KD // w + h)),
        pl.BlockSpec((CTX_LEN, w), lambda b, h, i: (ctx_row(b), COL_VD // w + h)),
    ]
    out_specs = [pl.BlockSpec((TQ_DIFF, w), lambda b, h, i: (b * LAT_QTILES + i, h))]
    out_shape = [jax.ShapeDtypeStruct((N_LAT, DIFF_WIDTH), BF16)]
    operands = [diff_lambda, diff_subln.reshape(DEPTH, 1, DIFF_V_DIM), p, p, p, p, p]
    if with_ctx_queries:
        in_specs.append(pl.BlockSpec((CTX_LEN, w), lambda b, h, i: (ctx_row(b), COL_QD // w + h)))
        out_specs.append(pl.BlockSpec((CTX_LEN, w), lambda b, h, i: (b, h)))
        out_shape.append(jax.ShapeDtypeStruct((N_CTX, DIFF_WIDTH), BF16))
        operands.append(p)
    if ada_next is not None:
        cc, w_ada, b_ada, ada_layer = ada_next
        tn = ADA_WIDTH // n_steps
        in_specs += [
            pl.BlockSpec((MOD_ROWS, D_MODEL), lambda b, h, i: (0, 0)),
            pl.BlockSpec((None, D_MODEL, tn), lambda b, h, i: (ada_layer, 0, step(b, h, i))),
            pl.BlockSpec((None, 1, tn), lambda b, h, i: (ada_layer, 0, step(b, h, i))),
        ]
        out_specs.append(pl.BlockSpec((MOD_ROWS, tn), lambda b, h, i: (0, step(b, h, i))))
        out_shape.append(jax.ShapeDtypeStruct((MOD_ROWS, ADA_WIDTH), F32))
        operands += [cc, w_ada, b_ada.reshape(DEPTH, 1, ADA_WIDTH)]
    out = pl.pallas_call(
        functools.partial(_diff_kernel, lam_init=lam_init, with_ctx_queries=with_ctx_queries,
                          with_ada=ada_next is not None),
        grid=(BATCH, DIFF_HEADS, LAT_QTILES),
        in_specs=in_specs,
        out_specs=out_specs,
        out_shape=out_shape,
        scratch_shapes=[pltpu.VMEM((2, 2, TQ_SUB, SEQ + CTX_LEN), F32)],
        compiler_params=_params("arbitrary", "arbitrary", "arbitrary"),
        name=f"diff_attn_l{layer}",
    )(*operands)
    return (out[:-1], out[-1]) if ada_next is not None else (out, None)


def _pool_kernel(u_ref, w_ref, scale_ref, o_ref, pad_ref, *, n_tok):
    t = lax.broadcasted_iota(jnp.int32, (n_tok, 1), 0)
    zeros = jnp.zeros((POOL_HALO, POOL_GROUP), F32)
    pad_ref[0:POOL_HALO, :] = zeros
    pad_ref[POOL_HALO + n_tok:, :] = zeros
    for g, w in enumerate(POOL_WINDOWS):
        cs = slice(g * POOL_GROUP, (g + 1) * POOL_GROUP)
        u = u_ref[:, cs]
        pad_ref[POOL_HALO:POOL_HALO + n_tok, :] = u
        total = pad_ref[POOL_HALO - w // 2:POOL_HALO - w // 2 + n_tok, :]
        for d in range(1 - w // 2, w // 2):
            total = total + pad_ref[POOL_HALO + d:POOL_HALO + d + n_tok, :]
        count = (jnp.clip(t + w // 2, 0, n_tok) - jnp.clip(t - w // 2, 0, n_tok)).astype(F32)
        centred = (total / count - u).astype(BF16)
        o_ref[:, cs] = (_dot(centred, w_ref[g]) * scale_ref[:, cs]).astype(BF16)


def _pool_call(u, pool_w, pool_scale, layer, n_tok, row_block0):
    return pl.pallas_call(
        functools.partial(_pool_kernel, n_tok=n_tok),
        grid=(BATCH,),
        in_specs=[
            pl.BlockSpec((n_tok, POOL_WIDTH), lambda b: (row_block0 + b, 0)),
            pl.BlockSpec((None, len(POOL_WINDOWS), POOL_GROUP, POOL_GROUP), lambda b: (layer, 0, 0, 0)),
            pl.BlockSpec((None, 1, POOL_WIDTH), lambda b: (layer, 0, 0)),
        ],
        out_specs=pl.BlockSpec((n_tok, POOL_WIDTH), lambda b: (b, 0)),
        out_shape=jax.ShapeDtypeStruct((BATCH * n_tok, POOL_WIDTH), BF16),
        scratch_shapes=[pltpu.VMEM((n_tok + 2 * POOL_HALO, POOL_GROUP), F32)],
        compiler_params=_params("arbitrary"),
        name=f"pool_l{layer}_n{n_tok}",
    )(u, pool_w, pool_scale.reshape(DEPTH, 1, POOL_WIDTH))


def _outproj_kernel(s_ref, m_ref, gpost_ref, oa_ref, *rest):
    w_ref, o_ref, stat_ref, vec_ref = rest[-4:]
    oa, ob, oc = oa_ref[...], rest[0][...], rest[1][...]
    if len(rest) == 8:
        latent = pl.program_id(0) < N_LAT // TM_OUT
        ob = jnp.where(latent, ob, rest[2][...])
        oc = jnp.where(latent, oc, rest[3][...])
    b0, c0 = WIN_WIDTH, WIN_WIDTH + POOL_WIDTH
    for n in range(D_MODEL // 1024):
        cs = slice(n * 1024, (n + 1) * 1024)
        o_ref[:, cs] = _dot(oa, w_ref[0:b0, cs]) + _dot(ob, w_ref[b0:c0, cs]) + _dot(oc, w_ref[c0:, cs])
    _gated_residual_to(o_ref, s_ref, o_ref, stat_ref, vec_ref, m_ref[2] * gpost_ref[...], TM_OUT)


def _outproj_call(s, mods, norm_post, o_a, o_b, o_c, w_out, layer, n_rows):
    lat_tiles = N_LAT // TM_OUT
    row_spec = lambda width: pl.BlockSpec((TM_OUT, width), lambda t: (t, 0))
    lat_spec = lambda width: pl.BlockSpec((TM_OUT, width), lambda t: (jnp.minimum(t, lat_tiles - 1), 0))
    ctx_spec = lambda width: pl.BlockSpec((TM_OUT, width), lambda t: (jnp.maximum(t - lat_tiles, 0), 0))
    mixer_specs = [lat_spec(POOL_WIDTH), lat_spec(DIFF_WIDTH)]
    mixers = [o_b[0], o_c[0]]
    if len(o_b) == 2:
        mixer_specs += [ctx_spec(POOL_WIDTH), ctx_spec(DIFF_WIDTH)]
        mixers += [o_b[1], o_c[1]]
    return pl.pallas_call(
        _outproj_kernel,
        grid=(n_rows // TM_OUT,),
        in_specs=[
            row_spec(D_MODEL),
            _mod_spec(1, TM_OUT),
            _norm_spec(layer, 1),
            row_spec(WIN_WIDTH),
        ] + mixer_specs + [
            pl.BlockSpec((D_MODEL, D_MODEL), lambda t: (0, 0), pipeline_mode=pl.Buffered(1)),
        ],
        out_specs=row_spec(D_MODEL),
        out_shape=jax.ShapeDtypeStruct((n_rows, D_MODEL), F32),
        scratch_shapes=_norm_scratch(TM_OUT),
        compiler_params=_params("arbitrary"),
        name=f"outproj_l{layer}",
    )(s, mods, norm_post, o_a, *mixers, w_out)


def _rope_tables():
    row = jnp.repeat(jnp.arange(SEQ // GRID_W, dtype=F32), GRID_W)
    col = jnp.tile(jnp.arange(GRID_W, dtype=F32), SEQ // GRID_W)
    half = HEAD_DIM // 2
    inv = ROPE_THETA ** (-jnp.arange(0, half, 2, dtype=F32) / half)
    ar = row[:, None] * inv
    ac = col[:, None] * inv
    ang = jnp.concatenate([ar, ar, ac, ac], axis=-1)
    cos, sin = jnp.cos(ang), jnp.sin(ang)
    low = (jnp.arange(HEAD_DIM) % half) < half // 2
    return cos, jnp.where(low, -sin, 0.0), jnp.where(low, 0.0, sin)


def kernel(x, c, ctx, c_ctx, w_ada, b_ada, norm_pre, norm_post, ffn_w_gate, ffn_w_up, ffn_w_down,
           w_in, w_out, attn_sink, pool_w, pool_scale, diff_lambda, diff_subln):
    srcs = (x.reshape(N_LAT, D_MODEL), ctx.reshape(N_CTX, D_MODEL))
    cc = jnp.zeros((MOD_ROWS, D_MODEL), F32).at[:BATCH].set(c).at[BATCH].set(c_ctx)
    mods_shape = (MOD_ROWS, 3, 3, 1, D_MODEL)
    mods = _ada_call(cc, w_ada, b_ada, 0).reshape(mods_shape)
    npre = norm_pre.reshape(DEPTH, 3, 1, D_MODEL)
    npost = norm_post.reshape(DEPTH, 3, 1, D_MODEL)
    pool_w16 = pool_w.astype(BF16)
    cos, sina, sinb = _rope_tables()

    ffn_f32 = (ffn_w_gate, ffn_w_up, ffn_w_down)
    ffn_order = [(layer, half) for layer in range(DEPTH) for half in range(2)]
    w_out16 = w_out[0].astype(BF16)

    def next_ffn(layer, half):
        i = ffn_order.index((layer, half)) + 1
        return [(w, ffn_order[i]) for w in ffn_f32] if i < len(ffn_order) else []

    for layer in range(DEPTH):
        update_ctx = layer < DEPTH - 1
        n_tiles = ALL_TILES if update_ctx else LAT_TILES
        jobs = next_ffn(layer, 0)
        if layer == 0:
            jobs, down_job = jobs[:2] + [(w_in, (0,))], jobs[2:]
            first, ffn16 = _ffn_call(srcs[:1], mods, npre, npost, ffn_f32, 0, 0, 1, f32_weights=True)
            s, done = _ffn_call(srcs, mods, npre, npost, ffn16, 0, 0, ALL_TILES - 1, jobs, tile0=1, head=first)
            ffn16, w_in16 = done[:2], done[2]
        else:
            s, ffn16 = _ffn_call(srcs, mods, npre, npost, ffn16, layer, 0, ALL_TILES, jobs)
            down_job = []
        proj_next = ([(w_in, (layer + 1,)), (w_out, (layer + 1,))] if update_ctx else []) + down_job
        p, u, proj16 = _inproj_call(s, mods, npre, w_in16, cos, sina, sinb, layer, proj_next)
        if down_job:
            ffn16, proj16 = ffn16 + proj16[-1:], proj16[:-1]
        o_a = _win_call(p, attn_sink, layer, update_ctx)
        ada_next = (cc, w_ada, b_ada, layer + 1) if update_ctx else None
        o_c, mods_next = _diff_call(p, diff_lambda, diff_subln, layer, update_ctx, ada_next)
        o_b =[_pool_call(u, pool_w16, pool_scale, layer, SEQ, 0)]
        if update_ctx:
            o_b.append(_pool_call(u, pool_w16, pool_scale, layer, CTX_LEN, N_LAT // CTX_LEN))
        s = _outproj_call(s, mods, npost, o_a, o_b, o_c, w_out16, layer, n_tiles * TM)
        s, ffn16 = _ffn_call((s,), mods, npre, npost, ffn16, layer, 1, n_tiles, next_ffn(layer, 1))
        srcs = (s,)
        if update_ctx:
            w_in16, w_out16 = proj16
            mods = mods_next.reshape(mods_shape)
    return s.reshape(BATCH, SEQ, D_MODEL)
```

```python
import functools
import math

import jax
import jax.numpy as jnp
from jax import lax
from jax.experimental import pallas as pl
from jax.experimental.pallas import tpu as pltpu

D_MODEL = 4096
BATCH = 4
SEQ = 2048
DEPTH = 2
GRID_W = 64
CTX_LEN = 256
HEAD_DIM = 128
ROPE_THETA = 10000.0
EPS = 1e-6
N_MOD = 9
FFN_DIM = 2 * D_MODEL

WIN_Q_HEADS = 12
WIN_KV_HEADS = 4
WIN_GROUP = WIN_Q_HEADS // WIN_KV_HEADS
WIN_WIDTH = WIN_Q_HEADS * HEAD_DIM
WINDOW = 128
BLOCK = 128

POOL_WINDOWS = (2, 4, 8, 16)
POOL_WIDTH = 1024
POOL_GROUP = POOL_WIDTH // len(POOL_WINDOWS)
POOL_HALO = 8

DIFF_HEADS = 6
DIFF_V_DIM = 2 * HEAD_DIM
DIFF_WIDTH = DIFF_HEADS * DIFF_V_DIM
IN_WIDTH = 8192

COL_QA = 0
COL_KA = COL_QA + WIN_WIDTH
COL_VA = COL_KA + WIN_KV_HEADS * HEAD_DIM
COL_U = COL_VA + WIN_KV_HEADS * HEAD_DIM
COL_QD = COL_U + POOL_WIDTH
COL_KD = COL_QD + 2 * DIFF_HEADS * HEAD_DIM
COL_VD = COL_KD + 2 * DIFF_HEADS * HEAD_DIM

F32 = jnp.float32
BF16 = jnp.bfloat16
LANES = 128
SUBLANES = 8
LOG2E = math.log2(math.e)
Q_TO_LOG2 = HEAD_DIM ** -0.5 * LOG2E
ADA_WIDTH = N_MOD * D_MODEL

N_LAT = BATCH * SEQ
N_CTX = BATCH * CTX_LEN
N_ALL = N_LAT + N_CTX

TM = 512
TM_OUT = 256
TF = 512
TF_F32 = 256
TN_IN = 1024
RH = 512
MXU_COLS = 256
TN_ADA = 512
TQ_DIFF = 1024
TQ_SUB = 256
ROW_CHUNK = 16
PASS_CHUNKS = 2
MOD_ROWS = 16
VMEM_LIMIT = 63 * 1024 * 1024
RING_DMA_THREAD = 1

LAT_TILES = N_LAT // TM
ALL_TILES = N_ALL // TM


def _mod_row(t, tm):
    return jnp.where(t < N_LAT // tm, t // (SEQ // tm), BATCH)


def _dot(a, b):
    return jnp.dot(a, b, preferred_element_type=F32)


def _dot_nt(a, b):
    return lax.dot_general(a, b, (((1,), (1,)), ((), ())), preferred_element_type=F32)


def _for_rows(n_rows, chunk, fn, unroll=1):
    def body(i, carry):
        fn(pl.ds(pl.multiple_of(i * chunk, chunk), chunk))
        return carry
    lax.fori_loop(0, n_rows // chunk, body, 0, unroll=unroll)


def _for_row_starts(n_rows, chunk, fn):
    def body(i, carry):
        fn(pl.multiple_of(i * chunk, chunk))
        return carry
    lax.fori_loop(0, n_rows // chunk, body, 0)


def _rstd_pass(load_rows, stat_ref, n_rows):
    def rows(r):
        v = load_rows(r)
        sq = v * v
        parts = [sq[:, k * LANES:(k + 1) * LANES] for k in range(D_MODEL // LANES)]
        while len(parts) > 1:
            parts = [parts[k] + parts[k + 1] for k in range(0, len(parts), 2)]
        total = jnp.sum(parts[0], axis=-1, keepdims=True)
        rstd = lax.rsqrt(total * (1.0 / D_MODEL) + EPS)
        stat_ref[r, :] = jnp.broadcast_to(rstd, (SUBLANES, LANES))
    _for_rows(n_rows, SUBLANES, rows, unroll=min(64, n_rows // SUBLANES))


def _rms(v, g):
    return v * lax.rsqrt(jnp.mean(v * v, axis=-1, keepdims=True) + EPS) * g


def _params(*sem):
    return pltpu.CompilerParams(dimension_semantics=sem, vmem_limit_bytes=VMEM_LIMIT)


def _ada_kernel(c_ref, w_ref, b_ref, o_ref):
    c = c_ref[...]
    s = (c / (1.0 + jnp.exp(-c))).astype(BF16)
    o_ref[...] = _dot(s, w_ref[...].astype(BF16)) + b_ref[...]


def _ada_call(cc, w_ada, b_ada, layer, width):
    return pl.pallas_call(
        _ada_kernel,
        grid=(width // TN_ADA,),
        in_specs=[
            pl.BlockSpec((MOD_ROWS, D_MODEL), lambda j: (0, 0)),
            pl.BlockSpec((None, D_MODEL, TN_ADA), lambda j: (layer, 0, j)),
            pl.BlockSpec((None, 1, TN_ADA), lambda j: (layer, 0, j)),
        ],
        out_specs=pl.BlockSpec((MOD_ROWS, TN_ADA), lambda j: (0, j)),
        out_shape=jax.ShapeDtypeStruct((MOD_ROWS, width), F32),
        compiler_params=_params("arbitrary"),
        name=f"adaln_l{layer}",
    )(cc, w_ada, b_ada.reshape(DEPTH, 1, ADA_WIDTH))


def _mod_spec(group, tm, tile0=0):
    return pl.BlockSpec((None, None, 3, 1, D_MODEL), lambda t, *_: (_mod_row(t + tile0, tm), group, 0, 0, 0))


def _norm_spec(layer, k):
    return pl.BlockSpec((None, None, 1, D_MODEL), lambda t, *_: (layer, k, 0, 0))


def _modulated_norm_to(h_ref, s_ref, stat_ref, vec_ref, m_ref, gpre_ref, n_rows, also=None):
    _rstd_pass(lambda r: s_ref[r, :], stat_ref, n_rows)
    full = (ROW_CHUNK, D_MODEL)
    vec_ref[0:ROW_CHUNK, :] = jnp.broadcast_to(gpre_ref[...] * (1.0 + m_ref[1]), full)
    vec_ref[ROW_CHUNK:, :] = jnp.broadcast_to(m_ref[0], full)

    def rows(r0):
        rs = [pl.ds(r0 + c * ROW_CHUNK, ROW_CHUNK) for c in range(PASS_CHUNKS)]
        rstd = [stat_ref[r, :] for r in rs]
        for k in range(D_MODEL // LANES):
            cs = slice(k * LANES, (k + 1) * LANES)
            gain, shift = vec_ref[0:ROW_CHUNK, cs], vec_ref[ROW_CHUNK:, cs]
            for r, rstd_r in zip(rs, rstd):
                h_ref[r, cs] = (s_ref[r, cs] * rstd_r * gain + shift).astype(BF16)
        if also is not None:
            for r in rs:
                also(r)
    _for_row_starts(n_rows, PASS_CHUNKS * ROW_CHUNK, rows)


def _gated_residual_to(o_ref, s_ref, y_ref, stat_ref, vec_ref, gate_gain, n_rows):
    _rstd_pass(lambda r: y_ref[r, :], stat_ref, n_rows)
    vec_ref[0:ROW_CHUNK, :] = jnp.broadcast_to(gate_gain, (ROW_CHUNK, D_MODEL))

    def rows(r):
        rstd = stat_ref[r, :]
        for k in range(D_MODEL // LANES):
            cs = slice(k * LANES, (k + 1) * LANES)
            o_ref[r, cs] = s_ref[r, cs] + y_ref[r, cs] * rstd * vec_ref[0:ROW_CHUNK, cs]
    _for_rows(n_rows, ROW_CHUNK, rows, unroll=2)


def _norm_scratch(n_rows):
    return [pltpu.VMEM((n_rows, LANES), F32), pltpu.VMEM((2 * ROW_CHUNK, D_MODEL), F32)]


def _hosted_cast(arr, lead, n_inner):
    rows, cols = arr.shape[-2:]
    block = (rows // LAT_TILES, cols // n_inner)

    def index(t, i):
        return jnp.minimum(t, LAT_TILES - 1), jnp.where(t < LAT_TILES, i, n_inner - 1)

    in_spec = pl.BlockSpec((None,) * len(lead) + block, lambda t, i: lead + index(t, i))
    return in_spec, pl.BlockSpec(block, index), jax.ShapeDtypeStruct((rows, cols), BF16)


def _ffn_kernel(*refs, n_src, n_cast, n_tiles, tile0, has_head, convert_weights):
    head_hbm, refs = (refs[0], refs[1:]) if has_head else (None, refs)
    srcs, refs = refs[:n_src], refs[n_src:]
    m_ref, gpre_ref, gpost_ref, wg_ref, wu_ref, wd_ref = refs[:6]
    n_conv = 3 if convert_weights else 0
    cast_src, o_hbm = refs[6:6 + n_cast], refs[6 + n_cast]
    cast_dst = refs[7 + n_cast:7 + 2 * n_cast]
    conv_dst = refs[7 + 2 * n_cast:7 + 2 * n_cast + n_conv]
    slots, acc_ref, h_ref, stat_ref, vec_ref, fetch_sem, store_sem, head_sem = refs[7 + 2 * n_cast + n_conv:]
    n_slots = slots.shape[0]

    def head_copy():
        return pltpu.make_async_copy(head_hbm, o_hbm.at[pl.ds(0, tile0 * TM), :], head_sem.at[0])
    t = pl.program_id(0)
    f = pl.program_id(1)
    cur = slots.at[lax.rem(t, n_slots)]

    def tile_rows(ref, tile):
        return ref.at[pl.ds(pl.multiple_of(tile * TM, TM), TM), :]

    def fetch(local, start):
        slot = lax.rem(local, n_slots)
        tile = local + tile0

        def run(src, src_tile):
            copy = pltpu.make_async_copy(tile_rows(src, src_tile), slots.at[slot], fetch_sem.at[slot])
            copy.start(priority=RING_DMA_THREAD) if start else copy.wait()
        if n_src == 1:
            run(srcs[0], tile)
        else:
            pl.when(tile < LAT_TILES)(lambda: run(srcs[0], tile))
            pl.when(tile >= LAT_TILES)(lambda: run(srcs[1], tile - LAT_TILES))

    def store(local):
        slot = lax.rem(local, n_slots)
        return pltpu.make_async_copy(slots.at[slot], tile_rows(o_hbm, local + tile0), store_sem.at[slot])

    @pl.when((t == 0) & (f == 0))
    def _first_fetch():
        fetch(t, start=True)
        if has_head:
            head_copy().start()

    @pl.when(f == 0)
    def _prologue():
        fetch(t, start=False)

        def zero_acc(r):
            acc_ref[r, :] = jnp.zeros((ROW_CHUNK, D_MODEL), F32)
        _modulated_norm_to(h_ref, cur, stat_ref, vec_ref, m_ref, gpre_ref, TM, also=zero_acc)

    @pl.when(f == 1)
    def _refill_other_slot():
        @pl.when(t >= 1)
        def _():
            store(t - 1).wait()

        @pl.when(t + 1 < n_tiles)
        def _():
            fetch(t + 1, start=True)

    if convert_weights:
        for dst, w_ref in zip(conv_dst, (wg_ref, wu_ref, wd_ref)):
            dst[...] = w_ref[...].astype(BF16)
        wg_ref, wu_ref, wd_ref = conv_dst
    h = h_ref[...]
    g = _dot(h, wg_ref[...])
    u = _dot(h, wu_ref[...])
    a = (g / (1.0 + jnp.exp(-g)) * u).astype(BF16)
    for c in range(D_MODEL // 1024):
        cs = slice(c * 1024, (c + 1) * 1024)
        acc_ref[:, cs] += _dot(a, wd_ref[:, cs])
    for src, dst in zip(cast_src, cast_dst):
        dst[...] = src[...].astype(BF16)

    @pl.when(f == pl.num_programs(1) - 1)
    def _epilogue():
        _gated_residual_to(cur, cur, acc_ref, stat_ref, vec_ref, 0.5 * m_ref[2] * gpost_ref[...], TM)
        store(t).start(priority=RING_DMA_THREAD)

        @pl.when(t == n_tiles - 1)
        def _():
            store(t).wait()
            if has_head:
                head_copy().wait()


def _ffn_call(srcs, mods, norm_pre, norm_post, weights, layer, half, n_tiles, cast_next=(), *,
              tile0=0, head=None, f32_weights=False):
    k = 2 * half
    tf = TF_F32 if f32_weights else TF
    n_f = FFN_DIM // tf
    assert n_f >= 2 and (head is None) == (tile0 == 0)
    out_rows = (tile0 + n_tiles) * TM
    hosted = [_hosted_cast(arr, lead, n_f) for arr, lead in cast_next]
    hbm = pl.BlockSpec(memory_space=pl.ANY)
    lead = (layer, half) if f32_weights else ()
    none = (None,) * len(lead)
    w_specs = [
        pl.BlockSpec(none + (D_MODEL, tf), lambda t, f: lead + (0, f)),
        pl.BlockSpec(none + (D_MODEL, tf), lambda t, f: lead + (0, f)),
        pl.BlockSpec(none + (tf, D_MODEL), lambda t, f: lead + (f, 0)),
    ]
    conv_specs, conv_shapes = [], []
    if f32_weights:
        conv_specs = [pl.BlockSpec((D_MODEL, tf), lambda t, f: (0, f)),
                      pl.BlockSpec((D_MODEL, tf), lambda t, f: (0, f)),
                      pl.BlockSpec((tf, D_MODEL), lambda t, f: (f, 0))]
        conv_shapes = [jax.ShapeDtypeStruct(w.shape[-2:], BF16) for w in weights]
    heads = [] if head is None else [head]
    out = pl.pallas_call(
        functools.partial(_ffn_kernel, n_src=len(srcs), n_cast=len(hosted), n_tiles=n_tiles, tile0=tile0,
                          has_head=head is not None, convert_weights=f32_weights),
        grid=(n_tiles, n_f),
        in_specs=[hbm] * (len(heads) + len(srcs)) + [
            _mod_spec(k, TM, tile0),
            _norm_spec(layer, k),
            _norm_spec(layer, k),
        ] + w_specs + [h[0] for h in hosted],
        out_specs=[hbm] + [h[1] for h in hosted] + conv_specs,
        out_shape=[jax.ShapeDtypeStruct((out_rows, D_MODEL), F32)] + [h[2] for h in hosted] + conv_shapes,
        scratch_shapes=[
            pltpu.VMEM((min(2, n_tiles), TM, D_MODEL), F32),
            pltpu.VMEM((TM, D_MODEL), F32),
            pltpu.VMEM((TM, D_MODEL), BF16),
        ] + _norm_scratch(TM) + [pltpu.SemaphoreType.DMA((2,)), pltpu.SemaphoreType.DMA((2,)),
                                 pltpu.SemaphoreType.DMA((1,))],
        compiler_params=_params("arbitrary", "arbitrary"),
        name=f"ffn_l{layer}h{half}" + ("_first" if f32_weights else ""),
    )(*heads, *srcs, mods, norm_pre, norm_post, *weights, *[arr for arr, _ in cast_next])
    return out[0], tuple(out[1:])


def _is_rope_block(jj):
    blk = lambda col: col // RH
    return (jj < blk(COL_VA)) | ((jj >= blk(COL_QD)) & (jj < blk(COL_VD)))


def _is_query_block(jj):
    blk = lambda col: col // RH
    return (jj < blk(COL_KA)) | ((jj >= blk(COL_QD)) & (jj < blk(COL_KD)))


def _inproj_kernel(s_ref, m_ref, gpre_ref, w_ref, cos_ref, sina_ref, sinb_ref, *rest):
    n_cast = (len(rest) - 5) // 2
    cast_src, (p_ref, u_ref) = rest[:n_cast], rest[n_cast:n_cast + 2]
    cast_dst = rest[n_cast + 2:2 * n_cast + 2]
    h_ref, stat_ref, vec_ref = rest[2 * n_cast + 2:]
    t = pl.program_id(0)
    j = pl.program_id(1)

    @pl.when(j == 0)
    def _prologue():
        _modulated_norm_to(h_ref, s_ref, stat_ref, vec_ref, m_ref, gpre_ref, TM)

    for src, dst in zip(cast_src, cast_dst):
        dst[...] = src[...].astype(BF16)

    h = h_ref[...]
    chunks = []
    for c in range(TN_IN // MXU_COLS):
        if c % (RH // MXU_COLS) == 0:
            blk = j * (TN_IN // RH) + c // (RH // MXU_COLS)
            rope = _is_rope_block(blk) & (t < LAT_TILES)
            gain = jnp.where(_is_query_block(blk), Q_TO_LOG2, 1.0)
            cos = jnp.where(rope, cos_ref[...], 1.0) * gain
            sina = jnp.where(rope, sina_ref[...], 0.0) * gain
            sinb = jnp.where(rope, sinb_ref[...], 0.0) * gain
        y = _dot(h, w_ref[:, c * MXU_COLS:(c + 1) * MXU_COLS])
        chunks.append(y)
        for hd in range(MXU_COLS // HEAD_DIM):
            v = y[:, hd * HEAD_DIM:(hd + 1) * HEAD_DIM]
            fwd = pltpu.roll(v, HEAD_DIM // 4, 1)
            bwd = pltpu.roll(v, HEAD_DIM - HEAD_DIM // 4, 1)
            col = c * MXU_COLS + hd * HEAD_DIM
            p_ref[:, col:col + HEAD_DIM] = (v * cos + bwd * sina + fwd * sinb).astype(BF16)

    for step in range(IN_WIDTH // TN_IN):
        for c in range(TN_IN // MXU_COLS):
            col = step * TN_IN + c * MXU_COLS
            if COL_U <= col < COL_QD:
                @pl.when(j == step)
                def _pool_channels(c=c, col=col):
                    u_ref[:, col - COL_U:col - COL_U + MXU_COLS] = chunks[c]


def _inproj_call(s, mods, norm_pre, w_in, cos, sina, sinb, layer, cast_next=()):
    n_j = IN_WIDTH // TN_IN
    hosted = [_hosted_cast(arr, lead, n_j) for arr, lead in cast_next]
    rope_spec = pl.BlockSpec((TM, HEAD_DIM), lambda t, j: (t % (SEQ // TM), 0))
    out = pl.pallas_call(
        _inproj_kernel,
        grid=(ALL_TILES, n_j),
        in_specs=[
            pl.BlockSpec((TM, D_MODEL), lambda t, j: (t, 0)),
            _mod_spec(1, TM),
            _norm_spec(layer, 1),
            pl.BlockSpec((D_MODEL, TN_IN), lambda t, j: (0, j)),
            rope_spec, rope_spec, rope_spec,
        ] + [h[0] for h in hosted],
        out_specs=[
            pl.BlockSpec((TM, TN_IN), lambda t, j: (t, j)),
            pl.BlockSpec((TM, POOL_WIDTH), lambda t, j: (t, 0)),
        ] + [h[1] for h in hosted],
        out_shape=[
            jax.ShapeDtypeStruct((N_ALL, IN_WIDTH), BF16),
            jax.ShapeDtypeStruct((N_ALL, POOL_WIDTH), F32),
        ] + [h[2] for h in hosted],
        scratch_shapes=[pltpu.VMEM((TM, D_MODEL), BF16)] + _norm_scratch(TM),
        compiler_params=_params("arbitrary", "arbitrary"),
        name=f"inproj_l{layer}",
    )(s, mods, norm_pre, w_in, cos, sina, sinb, *[arr for arr, _ in cast_next])
    return out[0], out[1], tuple(out[2:])


LAT_QBLOCKS = SEQ // BLOCK
CTX_QBLOCKS = CTX_LEN // BLOCK
BAND = 3 * BLOCK


def _win_kernel(sink_ref, q_ref, k_ref, v_ref, kc_ref, vc_ref, o_ref, s_ref):
    n = pl.program_id(1)
    row = lax.broadcasted_iota(jnp.int32, (WIN_GROUP * BLOCK, 1), 0)

    def head_slice(i):
        return slice(i * HEAD_DIM, (i + 1) * HEAD_DIM)

    def group_q(h):
        return jnp.concatenate([q_ref[:, head_slice(WIN_GROUP * h + g)] for g in range(WIN_GROUP)], axis=0)

    def group_sink(h):
        s0 = sink_ref[0, WIN_GROUP * h]
        s1 = sink_ref[0, WIN_GROUP * h + 1]
        s2 = sink_ref[0, WIN_GROUP * h + 2]
        return jnp.where(row < BLOCK, s0, jnp.where(row < 2 * BLOCK, s1, s2)) * LOG2E

    def store(h, o):
        for g in range(WIN_GROUP):
            o_ref[:, head_slice(WIN_GROUP * h + g)] = o[g * BLOCK:(g + 1) * BLOCK].astype(BF16)

    @pl.when(n < LAT_QBLOCKS)
    def _latent_queries():
        start = pl.multiple_of(jnp.clip((n - 1) * BLOCK, 0, SEQ - BAND), BLOCK)
        qpos = n * BLOCK + (row & (BLOCK - 1))
        kpos = start + lax.broadcasted_iota(jnp.int32, (1, BAND), 1)
        valid = jnp.abs(kpos - qpos) <= WINDOW

        def logits_to_slot(h, slot):
            q = group_q(h)
            s_ref[slot, :, 0:BAND] = jnp.where(valid, _dot_nt(q, k_ref[pl.ds(start, BAND), head_slice(h)]),
                                               -jnp.inf)
            s_ref[slot, :, BAND:] = _dot_nt(q, kc_ref[:, head_slice(h)])

        logits_to_slot(0, 0)
        for h in range(WIN_KV_HEADS):
            hs = head_slice(h)
            slot = h % 2
            if h + 1 < WIN_KV_HEADS:
                logits_to_slot(h + 1, 1 - slot)
            s = s_ref[slot]
            sink = group_sink(h)
            m = jnp.maximum(jnp.max(s, axis=-1, keepdims=True), sink)
            e = jnp.exp2(s - m)
            den = jnp.sum(e, axis=-1, keepdims=True) + jnp.exp2(sink - m)
            e = e.astype(BF16)
            o = _dot(e[:, 0:BAND], v_ref[pl.ds(start, BAND), hs]) + _dot(e[:, BAND:], vc_ref[:, hs])
            store(h, o * (1.0 / den))

    @pl.when(n >= LAT_QBLOCKS)
    def _context_queries():
        for h in range(WIN_KV_HEADS):
            hs = head_slice(h)
            s_ctx = _dot_nt(group_q(h), kc_ref[:, hs])
            sink = group_sink(h)
            m = jnp.maximum(jnp.max(s_ctx, axis=-1, keepdims=True), sink)
            e_ctx = jnp.exp2(s_ctx - m)
            den = jnp.sum(e_ctx, axis=-1, keepdims=True) + jnp.exp2(sink - m)
            store(h, _dot(e_ctx.astype(BF16), vc_ref[:, hs]) * (1.0 / den))


def _win_call(p, sink, layer, with_ctx_queries):
    nq = LAT_QBLOCKS + (CTX_QBLOCKS if with_ctx_queries else 0)
    kv_cols = WIN_KV_HEADS * HEAD_DIM

    def q_row(b, n):
        return jnp.where(n < LAT_QBLOCKS, b * LAT_QBLOCKS + n,
                         N_LAT // BLOCK + b * CTX_QBLOCKS + n - LAT_QBLOCKS)

    ctx_row = lambda b: N_LAT // CTX_LEN + b
    return pl.pallas_call(
        _win_kernel,
        grid=(BATCH, nq),
        in_specs=[
            pl.BlockSpec(memory_space=pltpu.SMEM),
            pl.BlockSpec((BLOCK, WIN_WIDTH), lambda b, n: (q_row(b, n), 0)),
            pl.BlockSpec((SEQ, kv_cols), lambda b, n: (b, COL_KA // kv_cols)),
            pl.BlockSpec((SEQ, kv_cols), lambda b, n: (b, COL_VA // kv_cols)),
            pl.BlockSpec((CTX_LEN, kv_cols), lambda b, n: (ctx_row(b), COL_KA // kv_cols)),
            pl.BlockSpec((CTX_LEN, kv_cols), lambda b, n: (ctx_row(b), COL_VA // kv_cols)),
        ],
        out_specs=pl.BlockSpec((BLOCK, WIN_WIDTH), lambda b, n: (q_row(b, n), 0)),
        out_shape=jax.ShapeDtypeStruct((N_ALL if with_ctx_queries else N_LAT, WIN_WIDTH), BF16),
        scratch_shapes=[pltpu.VMEM((2, WIN_GROUP * BLOCK, BAND + CTX_LEN), F32)],
        compiler_params=_params("arbitrary", "arbitrary"),
        name=f"win_attn_l{layer}",
    )(sink[layer].reshape(1, WIN_Q_HEADS), p, p, p, p, p)


LAT_QTILES = SEQ // TQ_DIFF


def _diff_kernel(*refs, lam_init, with_ctx_queries, n_ada):
    lam_ref, subln_ref, q_ref, k_ref, v_ref, kc_ref, vc_ref = refs[:7]
    rest = list(refs[7:])
    s_ref = rest.pop(-1)
    qc_ref = rest.pop(0) if with_ctx_queries else None
    ada_in = [[rest.pop(0) for _ in range(3)] for _ in range(n_ada)]
    o_ref = rest.pop(0)
    oc_ref = rest.pop(0) if with_ctx_queries else None
    i = pl.program_id(2)
    for job in ada_in:
        _ada_kernel(*job, rest.pop(0))
    dl = lam_ref[...]
    lam = (jnp.exp(jnp.sum(dl[0:1] * dl[1:2], axis=-1, keepdims=True))
           - jnp.exp(jnp.sum(dl[2:3] * dl[3:4], axis=-1, keepdims=True)) + lam_init)
    out_gain = subln_ref[...] * (1.0 - lam_init)

    def head_cols(j):
        return slice(j * HEAD_DIM, (j + 1) * HEAD_DIM)

    def mix(logits):
        maps = []
        for s in logits:
            e = jnp.exp2(s - jnp.max(s, axis=-1, keepdims=True))
            maps.append((e, 1.0 / jnp.sum(e, axis=-1, keepdims=True)))
        (e1, inv1), (e2, inv2) = maps
        return (e1 * inv1 - e2 * (lam * inv2)).astype(BF16)

    def sub_norm(o):
        return _rms(o, out_gain).astype(BF16)

    n_sub = TQ_DIFF // TQ_SUB

    def logits_to_slot(k, slot):
        rows = slice(k * TQ_SUB, (k + 1) * TQ_SUB)
        for j in range(2):
            q = q_ref[rows, head_cols(j)]
            s_ref[slot, j, :, 0:SEQ] = _dot_nt(q, k_ref[:, head_cols(j)])
            s_ref[slot, j, :, SEQ:] = _dot_nt(q, kc_ref[:, head_cols(j)])

    logits_to_slot(0, 0)
    for k in range(n_sub):
        slot = k % 2
        if k + 1 < n_sub:
            logits_to_slot(k + 1, 1 - slot)
        a = mix([s_ref[slot, 0], s_ref[slot, 1]])
        o = _dot(a[:, 0:SEQ], v_ref[...]) + _dot(a[:, SEQ:], vc_ref[...])
        o_ref[k * TQ_SUB:(k + 1) * TQ_SUB, :] = sub_norm(o)

    if with_ctx_queries:
        @pl.when(i == LAT_QTILES - 1)
        def _context_queries():
            q = qc_ref[...]
            a = mix([_dot_nt(q[:, head_cols(j)], kc_ref[:, head_cols(j)]) for j in range(2)])
            oc_ref[...] = sub_norm(_dot(a, vc_ref[...]))


def _diff_call(p, diff_lambda, diff_subln, layer, with_ctx_queries, ada_next=None):
    lam_init = 0.8 - 0.6 * math.exp(-0.3 * layer)
    w = DIFF_V_DIM
    ctx_row = lambda b: N_LAT // CTX_LEN + b
    n_steps = BATCH * DIFF_HEADS * LAT_QTILES
    step = lambda b, h, i: (b * DIFF_HEADS + h) * LAT_QTILES + i
    in_specs = [
        pl.BlockSpec((None, 4, HEAD_DIM), lambda b, h, i: (layer, 0, 0)),
        pl.BlockSpec((None, 1, DIFF_V_DIM), lambda b, h, i: (layer, 0, 0)),
        pl.BlockSpec((TQ_DIFF, w), lambda b, h, i: (b * LAT_QTILES + i, COL_QD // w + h)),
        pl.BlockSpec((SEQ, w), lambda b, h, i: (b, COL_KD // w + h)),
        pl.BlockSpec((SEQ, w), lambda b, h, i: (b, COL_VD // w + h)),
        pl.BlockSpec((CTX_LEN, w), lambda b, h, i: (ctx_row(b), COL_KD // w + h)),
        pl.BlockSpec((CTX_LEN, w), lambda b, h, i: (ctx_row(b), COL_VD // w + h)),
    ]
    out_specs = [pl.BlockSpec((TQ_DIFF, w), lambda b, h, i: (b * LAT_QTILES + i, h))]
    out_shape = [jax.ShapeDtypeStruct((N_LAT, DIFF_WIDTH), BF16)]
    operands = [diff_lambda, diff_subln.reshape(DEPTH, 1, DIFF_V_DIM), p, p, p, p, p]
    if with_ctx_queries:
        in_specs.append(pl.BlockSpec((CTX_LEN, w), lambda b, h, i: (ctx_row(b), COL_QD // w + h)))
        out_specs.append(pl.BlockSpec((CTX_LEN, w), lambda b, h, i: (b, h)))
        out_shape.append(jax.ShapeDtypeStruct((N_CTX, DIFF_WIDTH), BF16))
        operands.append(p)
    n_ada = 0
    if ada_next is not None:
        cc, w_ada, b_ada, jobs = ada_next
        n_ada = len(jobs)
        for ada_layer, col0, width in jobs:
            tn = width // n_steps
            blk0 = col0 // tn
            assert tn % LANES == 0 and col0 % tn == 0
            in_specs += [
                pl.BlockSpec((MOD_ROWS, D_MODEL), lambda b, h, i: (0, 0)),
                pl.BlockSpec((None, D_MODEL, tn),
                             lambda b, h, i, l=ada_layer, o=blk0: (l, 0, o + step(b, h, i))),
                pl.BlockSpec((None, 1, tn), lambda b, h, i, l=ada_layer, o=blk0: (l, 0, o + step(b, h, i))),
            ]
            operands += [cc, w_ada, b_ada.reshape(DEPTH, 1, ADA_WIDTH)]
        for _, _, width in jobs:
            tn = width // n_steps
            out_specs.append(pl.BlockSpec((MOD_ROWS, tn), lambda b, h, i: (0, step(b, h, i))))
            out_shape.append(jax.ShapeDtypeStruct((MOD_ROWS, width), F32))
    out = pl.pallas_call(
        functools.partial(_diff_kernel, lam_init=lam_init, with_ctx_queries=with_ctx_queries, n_ada=n_ada),
        grid=(BATCH, DIFF_HEADS, LAT_QTILES),
        in_specs=in_specs,
        out_specs=out_specs,
        out_shape=out_shape,
        scratch_shapes=[pltpu.VMEM((2, 2, TQ_SUB, SEQ + CTX_LEN), F32)],
        compiler_params=_params("arbitrary", "arbitrary", "arbitrary"),
        name=f"diff_attn_l{layer}",
    )(*operands)
    return (out[:len(out) - n_ada], tuple(out[len(out) - n_ada:]))


def _pool_kernel(u_ref, w_ref, scale_ref, o_ref, pad_ref, *, n_tok):
    t = lax.broadcasted_iota(jnp.int32, (n_tok, 1), 0)
    zeros = jnp.zeros((POOL_HALO, POOL_GROUP), F32)
    pad_ref[0:POOL_HALO, :] = zeros
    pad_ref[POOL_HALO + n_tok:, :] = zeros
    for g, w in enumerate(POOL_WINDOWS):
        cs = slice(g * POOL_GROUP, (g + 1) * POOL_GROUP)
        u = u_ref[:, cs]
        pad_ref[POOL_HALO:POOL_HALO + n_tok, :] = u
        total = pad_ref[POOL_HALO - w // 2:POOL_HALO - w // 2 + n_tok, :]
        for d in range(1 - w // 2, w // 2):
            total = total + pad_ref[POOL_HALO + d:POOL_HALO + d + n_tok, :]
        count = (jnp.clip(t + w // 2, 0, n_tok) - jnp.clip(t - w // 2, 0, n_tok)).astype(F32)
        centred = (total / count - u).astype(BF16)
        o_ref[:, cs] = (_dot(centred, w_ref[g]) * scale_ref[:, cs]).astype(BF16)


def _pool_call(u, pool_w, pool_scale, layer, n_tok, row_block0):
    return pl.pallas_call(
        functools.partial(_pool_kernel, n_tok=n_tok),
        grid=(BATCH,),
        in_specs=[
            pl.BlockSpec((n_tok, POOL_WIDTH), lambda b: (row_block0 + b, 0)),
            pl.BlockSpec((None, len(POOL_WINDOWS), POOL_GROUP, POOL_GROUP), lambda b: (layer, 0, 0, 0)),
            pl.BlockSpec((None, 1, POOL_WIDTH), lambda b: (layer, 0, 0)),
        ],
        out_specs=pl.BlockSpec((n_tok, POOL_WIDTH), lambda b: (b, 0)),
        out_shape=jax.ShapeDtypeStruct((BATCH * n_tok, POOL_WIDTH), BF16),
        scratch_shapes=[pltpu.VMEM((n_tok + 2 * POOL_HALO, POOL_GROUP), F32)],
        compiler_params=_params("arbitrary"),
        name=f"pool_l{layer}_n{n_tok}",
    )(u, pool_w, pool_scale.reshape(DEPTH, 1, POOL_WIDTH))


def _outproj_kernel(s_ref, m_ref, gpost_ref, oa_ref, *rest):
    w_ref, o_ref, stat_ref, vec_ref = rest[-4:]
    oa, ob, oc = oa_ref[...], rest[0][...], rest[1][...]
    if len(rest) == 8:
        latent = pl.program_id(0) < N_LAT // TM_OUT
        ob = jnp.where(latent, ob, rest[2][...])
        oc = jnp.where(latent, oc, rest[3][...])
    b0, c0 = WIN_WIDTH, WIN_WIDTH + POOL_WIDTH
    for n in range(D_MODEL // 1024):
        cs = slice(n * 1024, (n + 1) * 1024)
        o_ref[:, cs] = _dot(oa, w_ref[0:b0, cs]) + _dot(ob, w_ref[b0:c0, cs]) + _dot(oc, w_ref[c0:, cs])
    _gated_residual_to(o_ref, s_ref, o_ref, stat_ref, vec_ref, m_ref[2] * gpost_ref[...], TM_OUT)


def _outproj_call(s, mods, norm_post, o_a, o_b, o_c, w_out, layer, n_rows):
    lat_tiles = N_LAT // TM_OUT
    row_spec = lambda width: pl.BlockSpec((TM_OUT, width), lambda t: (t, 0))
    lat_spec = lambda width: pl.BlockSpec((TM_OUT, width), lambda t: (jnp.minimum(t, lat_tiles - 1), 0))
    ctx_spec = lambda width: pl.BlockSpec((TM_OUT, width), lambda t: (jnp.maximum(t - lat_tiles, 0), 0))
    mixer_specs = [lat_spec(POOL_WIDTH), lat_spec(DIFF_WIDTH)]
    mixers = [o_b[0], o_c[0]]
    if len(o_b) == 2:
        mixer_specs += [ctx_spec(POOL_WIDTH), ctx_spec(DIFF_WIDTH)]
        mixers += [o_b[1], o_c[1]]
    return pl.pallas_call(
        _outproj_kernel,
        grid=(n_rows // TM_OUT,),
        in_specs=[
            row_spec(D_MODEL),
            _mod_spec(1, TM_OUT),
            _norm_spec(layer, 1),
            row_spec(WIN_WIDTH),
        ] + mixer_specs + [
            pl.BlockSpec((D_MODEL, D_MODEL), lambda t: (0, 0), pipeline_mode=pl.Buffered(1)),
        ],
        out_specs=row_spec(D_MODEL),
        out_shape=jax.ShapeDtypeStruct((n_rows, D_MODEL), F32),
        scratch_shapes=_norm_scratch(TM_OUT),
        compiler_params=_params("arbitrary"),
        name=f"outproj_l{layer}",
    )(s, mods, norm_post, o_a, *mixers, w_out)


def _rope_tables():
    row = jnp.repeat(jnp.arange(SEQ // GRID_W, dtype=F32), GRID_W)
    col = jnp.tile(jnp.arange(GRID_W, dtype=F32), SEQ // GRID_W)
    half = HEAD_DIM // 2
    inv = ROPE_THETA ** (-jnp.arange(0, half, 2, dtype=F32) / half)
    ar = row[:, None] * inv
    ac = col[:, None] * inv
    ang = jnp.concatenate([ar, ar, ac, ac], axis=-1)
    cos, sin = jnp.cos(ang), jnp.sin(ang)
    low = (jnp.arange(HEAD_DIM) % half) < half // 2
    return cos, jnp.where(low, -sin, 0.0), jnp.where(low, 0.0, sin)


def kernel(x, c, ctx, c_ctx, w_ada, b_ada, norm_pre, norm_post, ffn_w_gate, ffn_w_up, ffn_w_down,
           w_in, w_out, attn_sink, pool_w, pool_scale, diff_lambda, diff_subln):
    srcs = (x.reshape(N_LAT, D_MODEL), ctx.reshape(N_CTX, D_MODEL))
    cc = jnp.zeros((MOD_ROWS, D_MODEL), F32).at[:BATCH].set(c).at[BATCH].set(c_ctx)
    group_w = 3 * D_MODEL
    mods = _ada_call(cc, w_ada, b_ada, 0, 2 * group_w).reshape(MOD_ROWS, 2, 3, 1, D_MODEL)
    npre = norm_pre.reshape(DEPTH, 3, 1, D_MODEL)
    npost = norm_post.reshape(DEPTH, 3, 1, D_MODEL)
    pool_w16 = pool_w.astype(BF16)
    cos, sina, sinb = _rope_tables()

    ffn_f32 = (ffn_w_gate, ffn_w_up, ffn_w_down)
    ffn_order = [(layer, half) for layer in range(DEPTH) for half in range(2)]
    w_out16 = w_out[0].astype(BF16)

    def next_ffn(layer, half):
        i = ffn_order.index((layer, half)) + 1
        return [(w, ffn_order[i]) for w in ffn_f32] if i < len(ffn_order) else []

    for layer in range(DEPTH):
        update_ctx = layer < DEPTH - 1
        n_tiles = ALL_TILES if update_ctx else LAT_TILES
        jobs = next_ffn(layer, 0)
        if layer == 0:
            jobs, down_job = jobs[:2] + [(w_in, (0,))], jobs[2:]
            first, ffn16 = _ffn_call(srcs[:1], mods, npre, npost, ffn_f32, 0, 0, 1, f32_weights=True)
            s, done = _ffn_call(srcs, mods, npre, npost, ffn16, 0, 0, ALL_TILES - 1, jobs, tile0=1, head=first)
            ffn16, w_in16 = done[:2], done[2]
        else:
            s, ffn16 = _ffn_call(srcs, mods, npre, npost, ffn16, layer, 0, ALL_TILES, jobs)
            down_job = []
        proj_next = ([(w_in, (layer + 1,)), (w_out, (layer + 1,))] if update_ctx else []) + down_job
        p, u, proj16 = _inproj_call(s, mods, npre, w_in16, cos, sina, sinb, layer, proj_next)
        if down_job:
            ffn16, proj16 = ffn16 + proj16[-1:], proj16[:-1]
        o_a = _win_call(p, attn_sink, layer, update_ctx)
        ada_jobs = [(layer, 2 * group_w, group_w)] if layer == 0 else []
        if update_ctx:
            ada_jobs.append((layer + 1, 0, ADA_WIDTH))
        o_c, tables = _diff_call(p, diff_lambda, diff_subln, layer, update_ctx,
                                 (cc, w_ada, b_ada, ada_jobs) if ada_jobs else None)
        if layer == 0:
            last_group = tables[0].reshape(MOD_ROWS, 1, 3, 1, D_MODEL)
            mods = jnp.concatenate([mods, last_group], axis=1)
        o_b =[_pool_call(u, pool_w16, pool_scale, layer, SEQ, 0)]
        if update_ctx:
            o_b.append(_pool_call(u, pool_w16, pool_scale, layer, CTX_LEN, N_LAT // CTX_LEN))
        s = _outproj_call(s, mods, npost, o_a, o_b, o_c, w_out16, layer, n_tiles * TM)
        s, ffn16 = _ffn_call((s,), mods, npre, npost, ffn16, layer, 1, n_tiles, next_ffn(layer, 1))
        srcs = (s,)
        if update_ctx:
            w_in16, w_out16 = proj16
            mods = tables[-1].reshape(MOD_ROWS, 3, 3, 1, D_MODEL)
    return s.reshape(BATCH, SEQ, D_MODEL)
```

```python
import functools
import math

import jax
import jax.numpy as jnp
from jax import lax
from jax.experimental import pallas as pl
from jax.experimental.pallas import tpu as pltpu

D_MODEL = 4096
BATCH = 4
SEQ = 2048
DEPTH = 2
GRID_W = 64
CTX_LEN = 256
HEAD_DIM = 128
ROPE_THETA = 10000.0
EPS = 1e-6
N_MOD = 9
FFN_DIM = 2 * D_MODEL

WIN_Q_HEADS = 12
WIN_KV_HEADS = 4
WIN_GROUP = WIN_Q_HEADS // WIN_KV_HEADS
WIN_WIDTH = WIN_Q_HEADS * HEAD_DIM
WINDOW = 128
BLOCK = 128

POOL_WINDOWS = (2, 4, 8, 16)
POOL_WIDTH = 1024
POOL_GROUP = POOL_WIDTH // len(POOL_WINDOWS)
POOL_HALO = 8

DIFF_HEADS = 6
DIFF_V_DIM = 2 * HEAD_DIM
DIFF_WIDTH = DIFF_HEADS * DIFF_V_DIM
IN_WIDTH = 8192

COL_QA = 0
COL_KA = COL_QA + WIN_WIDTH
COL_VA = COL_KA + WIN_KV_HEADS * HEAD_DIM
COL_U = COL_VA + WIN_KV_HEADS * HEAD_DIM
COL_QD = COL_U + POOL_WIDTH
COL_KD = COL_QD + 2 * DIFF_HEADS * HEAD_DIM
COL_VD = COL_KD + 2 * DIFF_HEADS * HEAD_DIM

F32 = jnp.float32
BF16 = jnp.bfloat16
LANES = 128
SUBLANES = 8
LOG2E = math.log2(math.e)
Q_TO_LOG2 = HEAD_DIM ** -0.5 * LOG2E
ADA_WIDTH = N_MOD * D_MODEL

N_LAT = BATCH * SEQ
N_CTX = BATCH * CTX_LEN
N_ALL = N_LAT + N_CTX

TM = 512
TM_OUT = 256
TF = 512
TF_F32 = 256
TN_IN = 1024
RH = 512
MXU_COLS = 256
TN_ADA = 512
TQ_DIFF = 1024
TQ_SUB = 256
ROW_CHUNK = 16
PASS_CHUNKS = 2
MOD_ROWS = 16
VMEM_LIMIT = 63 * 1024 * 1024
RING_DMA_THREAD = 1

LAT_TILES = N_LAT // TM
ALL_TILES = N_ALL // TM


def _mod_row(t, tm):
    return jnp.where(t < N_LAT // tm, t // (SEQ // tm), BATCH)


def _dot(a, b):
    return jnp.dot(a, b, preferred_element_type=F32)


def _dot_nt(a, b):
    return lax.dot_general(a, b, (((1,), (1,)), ((), ())), preferred_element_type=F32)


def _for_rows(n_rows, chunk, fn, unroll=1):
    def body(i, carry):
        fn(pl.ds(pl.multiple_of(i * chunk, chunk), chunk))
        return carry
    lax.fori_loop(0, n_rows // chunk, body, 0, unroll=unroll)


def _for_row_starts(n_rows, chunk, fn):
    def body(i, carry):
        fn(pl.multiple_of(i * chunk, chunk))
        return carry
    lax.fori_loop(0, n_rows // chunk, body, 0)


def _rstd_pass(load_rows, stat_ref, n_rows):
    def rows(r):
        v = load_rows(r)
        sq = v * v
        parts = [sq[:, k * LANES:(k + 1) * LANES] for k in range(D_MODEL // LANES)]
        while len(parts) > 1:
            parts = [parts[k] + parts[k + 1] for k in range(0, len(parts), 2)]
        total = jnp.sum(parts[0], axis=-1, keepdims=True)
        rstd = lax.rsqrt(total * (1.0 / D_MODEL) + EPS)
        stat_ref[r, :] = jnp.broadcast_to(rstd, (SUBLANES, LANES))
    _for_rows(n_rows, SUBLANES, rows, unroll=min(64, n_rows // SUBLANES))


def _rms(v, g):
    return v * lax.rsqrt(jnp.mean(v * v, axis=-1, keepdims=True) + EPS) * g


def _params(*sem):
    return pltpu.CompilerParams(dimension_semantics=sem, vmem_limit_bytes=VMEM_LIMIT)


def _ada_kernel(c_ref, w_ref, b_ref, o_ref):
    c = c_ref[...]
    s = (c / (1.0 + jnp.exp(-c))).astype(BF16)
    o_ref[...] = _dot(s, w_ref[...].astype(BF16)) + b_ref[...]


def _ada_call(cc, w_ada, b_ada, layer, width):
    return pl.pallas_call(
        _ada_kernel,
        grid=(width // TN_ADA,),
        in_specs=[
            pl.BlockSpec((MOD_ROWS, D_MODEL), lambda j: (0, 0)),
            pl.BlockSpec((None, D_MODEL, TN_ADA), lambda j: (layer, 0, j)),
            pl.BlockSpec((None, 1, TN_ADA), lambda j: (layer, 0, j)),
        ],
        out_specs=pl.BlockSpec((MOD_ROWS, TN_ADA), lambda j: (0, j)),
        out_shape=jax.ShapeDtypeStruct((MOD_ROWS, width), F32),
        compiler_params=_params("arbitrary"),
        name=f"adaln_l{layer}",
    )(cc, w_ada, b_ada.reshape(DEPTH, 1, ADA_WIDTH))


def _mod_spec(group, tm, tile0=0):
    return pl.BlockSpec((None, None, 3, 1, D_MODEL), lambda t, *_: (_mod_row(t + tile0, tm), group, 0, 0, 0))


def _norm_spec(layer, k):
    return pl.BlockSpec((None, None, 1, D_MODEL), lambda t, *_: (layer, k, 0, 0))


def _modulated_norm_to(h_ref, s_ref, stat_ref, vec_ref, m_ref, gpre_ref, n_rows, also=None):
    _rstd_pass(lambda r: s_ref[r, :], stat_ref, n_rows)
    full = (ROW_CHUNK, D_MODEL)
    vec_ref[0:ROW_CHUNK, :] = jnp.broadcast_to(gpre_ref[...] * (1.0 + m_ref[1]), full)
    vec_ref[ROW_CHUNK:, :] = jnp.broadcast_to(m_ref[0], full)

    def rows(r0):
        rs = [pl.ds(r0 + c * ROW_CHUNK, ROW_CHUNK) for c in range(PASS_CHUNKS)]
        rstd = [stat_ref[r, :] for r in rs]
        for k in range(D_MODEL // LANES):
            cs = slice(k * LANES, (k + 1) * LANES)
            gain, shift = vec_ref[0:ROW_CHUNK, cs], vec_ref[ROW_CHUNK:, cs]
            for r, rstd_r in zip(rs, rstd):
                h_ref[r, cs] = (s_ref[r, cs] * rstd_r * gain + shift).astype(BF16)
        if also is not None:
            for r in rs:
                also(r)
    _for_row_starts(n_rows, PASS_CHUNKS * ROW_CHUNK, rows)


def _gated_residual_to(o_ref, s_ref, y_ref, stat_ref, vec_ref, gate_gain, n_rows):
    _rstd_pass(lambda r: y_ref[r, :], stat_ref, n_rows)
    vec_ref[0:ROW_CHUNK, :] = jnp.broadcast_to(gate_gain, (ROW_CHUNK, D_MODEL))

    def rows(r):
        rstd = stat_ref[r, :]
        for k in range(D_MODEL // LANES):
            cs = slice(k * LANES, (k + 1) * LANES)
            o_ref[r, cs] = s_ref[r, cs] + y_ref[r, cs] * rstd * vec_ref[0:ROW_CHUNK, cs]
    _for_rows(n_rows, ROW_CHUNK, rows, unroll=2)


def _norm_scratch(n_rows):
    return [pltpu.VMEM((n_rows, LANES), F32), pltpu.VMEM((2 * ROW_CHUNK, D_MODEL), F32)]


def _hosted_cast(arr, lead, n_inner):
    rows, cols = arr.shape[-2:]
    block = (rows // LAT_TILES, cols // n_inner)

    def index(t, i):
        return jnp.minimum(t, LAT_TILES - 1), jnp.where(t < LAT_TILES, i, n_inner - 1)

    in_spec = pl.BlockSpec((None,) * len(lead) + block, lambda t, i: lead + index(t, i))
    return in_spec, pl.BlockSpec(block, index), jax.ShapeDtypeStruct((rows, cols), BF16)


def _ffn_kernel(*refs, n_src, n_cast, n_tiles, tile0, has_head, convert_weights):
    head_hbm, refs = (refs[0], refs[1:]) if has_head else (None, refs)
    srcs, refs = refs[:n_src], refs[n_src:]
    m_ref, gpre_ref, gpost_ref, wg_ref, wu_ref, wd_ref = refs[:6]
    n_conv = 3 if convert_weights else 0
    cast_src, o_hbm = refs[6:6 + n_cast], refs[6 + n_cast]
    cast_dst = refs[7 + n_cast:7 + 2 * n_cast]
    conv_dst = refs[7 + 2 * n_cast:7 + 2 * n_cast + n_conv]
    slots, acc_ref, h_ref, stat_ref, vec_ref, fetch_sem, store_sem, head_sem = refs[7 + 2 * n_cast + n_conv:]
    n_slots = slots.shape[0]

    def head_copy():
        return pltpu.make_async_copy(head_hbm, o_hbm.at[pl.ds(0, tile0 * TM), :], head_sem.at[0])
    t = pl.program_id(0)
    f = pl.program_id(1)
    cur = slots.at[lax.rem(t, n_slots)]

    def tile_rows(ref, tile):
        return ref.at[pl.ds(pl.multiple_of(tile * TM, TM), TM), :]

    def fetch(local, start):
        slot = lax.rem(local, n_slots)
        tile = local + tile0

        def run(src, src_tile):
            copy = pltpu.make_async_copy(tile_rows(src, src_tile), slots.at[slot], fetch_sem.at[slot])
            copy.start(priority=RING_DMA_THREAD) if start else copy.wait()
        if n_src == 1:
            run(srcs[0], tile)
        else:
            pl.when(tile < LAT_TILES)(lambda: run(srcs[0], tile))
            pl.when(tile >= LAT_TILES)(lambda: run(srcs[1], tile - LAT_TILES))

    def store(local):
        slot = lax.rem(local, n_slots)
        return pltpu.make_async_copy(slots.at[slot], tile_rows(o_hbm, local + tile0), store_sem.at[slot])

    @pl.when((t == 0) & (f == 0))
    def _first_fetch():
        fetch(t, start=True)
        if has_head:
            head_copy().start()

    @pl.when(f == 0)
    def _prologue():
        fetch(t, start=False)

        def zero_acc(r):
            acc_ref[r, :] = jnp.zeros((ROW_CHUNK, D_MODEL), F32)
        _modulated_norm_to(h_ref, cur, stat_ref, vec_ref, m_ref, gpre_ref, TM, also=zero_acc)

    @pl.when(f == 1)
    def _refill_other_slot():
        @pl.when(t >= 1)
        def _():
            store(t - 1).wait()

        @pl.when(t + 1 < n_tiles)
        def _():
            fetch(t + 1, start=True)

    if convert_weights:
        for dst, w_ref in zip(conv_dst, (wg_ref, wu_ref, wd_ref)):
            dst[...] = w_ref[...].astype(BF16)
        wg_ref, wu_ref, wd_ref = conv_dst
    h = h_ref[...]
    g = _dot(h, wg_ref[...])
    u = _dot(h, wu_ref[...])
    a = (g / (1.0 + jnp.exp(-g)) * u).astype(BF16)
    for c in range(D_MODEL // 1024):
        cs = slice(c * 1024, (c + 1) * 1024)
        acc_ref[:, cs] += _dot(a, wd_ref[:, cs])
    for src, dst in zip(cast_src, cast_dst):
        dst[...] = src[...].astype(BF16)

    @pl.when(f == pl.num_programs(1) - 1)
    def _epilogue():
        _gated_residual_to(cur, cur, acc_ref, stat_ref, vec_ref, 0.5 * m_ref[2] * gpost_ref[...], TM)
        store(t).start(priority=RING_DMA_THREAD)

        @pl.when(t == n_tiles - 1)
        def _():
            store(t).wait()
            if has_head:
                head_copy().wait()


def _ffn_call(srcs, mods, norm_pre, norm_post, weights, layer, half, n_tiles, cast_next=(), *,
              tile0=0, head=None, f32_weights=False):
    k = 2 * half
    tf = TF_F32 if f32_weights else TF
    n_f = FFN_DIM // tf
    assert n_f >= 2 and (head is None) == (tile0 == 0)
    out_rows = (tile0 + n_tiles) * TM
    hosted = [_hosted_cast(arr, lead, n_f) for arr, lead in cast_next]
    hbm = pl.BlockSpec(memory_space=pl.ANY)
    lead = (layer, half) if f32_weights else ()
    none = (None,) * len(lead)
    w_specs = [
        pl.BlockSpec(none + (D_MODEL, tf), lambda t, f: lead + (0, f)),
        pl.BlockSpec(none + (D_MODEL, tf), lambda t, f: lead + (0, f)),
        pl.BlockSpec(none + (tf, D_MODEL), lambda t, f: lead + (f, 0)),
    ]
    conv_specs, conv_shapes = [], []
    if f32_weights:
        conv_specs = [pl.BlockSpec((D_MODEL, tf), lambda t, f: (0, f)),
                      pl.BlockSpec((D_MODEL, tf), lambda t, f: (0, f)),
                      pl.BlockSpec((tf, D_MODEL), lambda t, f: (f, 0))]
        conv_shapes = [jax.ShapeDtypeStruct(w.shape[-2:], BF16) for w in weights]
    heads = [] if head is None else [head]
    out = pl.pallas_call(
        functools.partial(_ffn_kernel, n_src=len(srcs), n_cast=len(hosted), n_tiles=n_tiles, tile0=tile0,
                          has_head=head is not None, convert_weights=f32_weights),
        grid=(n_tiles, n_f),
        in_specs=[hbm] * (len(heads) + len(srcs)) + [
            _mod_spec(k, TM, tile0),
            _norm_spec(layer, k),
            _norm_spec(layer, k),
        ] + w_specs + [h[0] for h in hosted],
        out_specs=[hbm] + [h[1] for h in hosted] + conv_specs,
        out_shape=[jax.ShapeDtypeStruct((out_rows, D_MODEL), F32)] + [h[2] for h in hosted] + conv_shapes,
        scratch_shapes=[
            pltpu.VMEM((min(2, n_tiles), TM, D_MODEL), F32),
            pltpu.VMEM((TM, D_MODEL), F32),
            pltpu.VMEM((TM, D_MODEL), BF16),
        ] + _norm_scratch(TM) + [pltpu.SemaphoreType.DMA((2,)), pltpu.SemaphoreType.DMA((2,)),
                                 pltpu.SemaphoreType.DMA((1,))],
        compiler_params=_params("arbitrary", "arbitrary"),
        name=f"ffn_l{layer}h{half}" + ("_first" if f32_weights else ""),
    )(*heads, *srcs, mods, norm_pre, norm_post, *weights, *[arr for arr, _ in cast_next])
    return out[0], tuple(out[1:])


def _is_rope_block(jj):
    blk = lambda col: col // RH
    return (jj < blk(COL_VA)) | ((jj >= blk(COL_QD)) & (jj < blk(COL_VD)))


def _is_query_block(jj):
    blk = lambda col: col // RH
    return (jj < blk(COL_KA)) | ((jj >= blk(COL_QD)) & (jj < blk(COL_KD)))


def _inproj_kernel(s_ref, m_ref, gpre_ref, w_ref, cos_ref, sina_ref, sinb_ref, *rest,
                   keys_values_only_for_ctx):
    n_cast = (len(rest) - 5) // 2
    cast_src, (p_ref, u_ref) = rest[:n_cast], rest[n_cast:n_cast + 2]
    cast_dst = rest[n_cast + 2:2 * n_cast + 2]
    h_ref, stat_ref, vec_ref = rest[2 * n_cast + 2:]
    t = pl.program_id(0)
    j = pl.program_id(1)

    @pl.when(j == 0)
    def _prologue():
        _modulated_norm_to(h_ref, s_ref, stat_ref, vec_ref, m_ref, gpre_ref, TM)

    def pooled_columns(step):
        cols = [(c, step * TN_IN + c * MXU_COLS) for c in range(TN_IN // MXU_COLS)]
        return [(c, col - COL_U) for c, col in cols if COL_U <= col < COL_QD]

    def project_columns():
        for src, dst in zip(cast_src, cast_dst):
            dst[...] = src[...].astype(BF16)

        h = h_ref[...]
        chunks = []
        for c in range(TN_IN // MXU_COLS):
            if c % (RH // MXU_COLS) == 0:
                blk = j * (TN_IN // RH) + c // (RH // MXU_COLS)
                rope = _is_rope_block(blk) & (t < LAT_TILES)
                gain = jnp.where(_is_query_block(blk), Q_TO_LOG2, 1.0)
                cos = jnp.where(rope, cos_ref[...], 1.0) * gain
                sina = jnp.where(rope, sina_ref[...], 0.0) * gain
                sinb = jnp.where(rope, sinb_ref[...], 0.0) * gain
            y = _dot(h, w_ref[:, c * MXU_COLS:(c + 1) * MXU_COLS])
            chunks.append(y)
            for hd in range(MXU_COLS // HEAD_DIM):
                v = y[:, hd * HEAD_DIM:(hd + 1) * HEAD_DIM]
                fwd = pltpu.roll(v, HEAD_DIM // 4, 1)
                bwd = pltpu.roll(v, HEAD_DIM - HEAD_DIM // 4, 1)
                col = c * MXU_COLS + hd * HEAD_DIM
                p_ref[:, col:col + HEAD_DIM] = (v * cos + bwd * sina + fwd * sinb).astype(BF16)

        for step in range(IN_WIDTH // TN_IN):
            for c, dst0 in pooled_columns(step):
                @pl.when(j == step)
                def _pool_channels(c=c, dst0=dst0):
                    u_ref[:, dst0:dst0 + MXU_COLS] = chunks[c]

    if not keys_values_only_for_ctx:
        project_columns()
    else:
        lo, hi = j * TN_IN, (j + 1) * TN_IN
        needed = ((lo < COL_U) & (hi > COL_KA)) | (hi > COL_KD)
        active = (t < LAT_TILES) | needed
        pl.when(active)(project_columns)

        @pl.when(jnp.logical_not(active))
        def _unused_context_columns():
            p_ref[...] = jnp.zeros((TM, TN_IN), BF16)
            for step in range(IN_WIDTH // TN_IN):
                for _, dst0 in pooled_columns(step):
                    @pl.when(j == step)
                    def _zero_pool(dst0=dst0):
                        u_ref[:, dst0:dst0 + MXU_COLS] = jnp.zeros((TM, MXU_COLS), F32)


def _inproj_call(s, mods, norm_pre, w_in, cos, sina, sinb, layer, cast_next=()):
    n_j = IN_WIDTH // TN_IN
    hosted = [_hosted_cast(arr, lead, n_j) for arr, lead in cast_next]
    rope_spec = pl.BlockSpec((TM, HEAD_DIM), lambda t, j: (t % (SEQ // TM), 0))
    out = pl.pallas_call(
        functools.partial(_inproj_kernel, keys_values_only_for_ctx=layer == DEPTH - 1),
        grid=(ALL_TILES, n_j),
        in_specs=[
            pl.BlockSpec((TM, D_MODEL), lambda t, j: (t, 0)),
            _mod_spec(1, TM),
            _norm_spec(layer, 1),
            pl.BlockSpec((D_MODEL, TN_IN), lambda t, j: (0, j)),
            rope_spec, rope_spec, rope_spec,
        ] + [h[0] for h in hosted],
        out_specs=[
            pl.BlockSpec((TM, TN_IN), lambda t, j: (t, j)),
            pl.BlockSpec((TM, POOL_WIDTH), lambda t, j: (t, 0)),
        ] + [h[1] for h in hosted],
        out_shape=[
            jax.ShapeDtypeStruct((N_ALL, IN_WIDTH), BF16),
            jax.ShapeDtypeStruct((N_ALL, POOL_WIDTH), F32),
        ] + [h[2] for h in hosted],
        scratch_shapes=[pltpu.VMEM((TM, D_MODEL), BF16)] + _norm_scratch(TM),
        compiler_params=_params("arbitrary", "arbitrary"),
        name=f"inproj_l{layer}",
    )(s, mods, norm_pre, w_in, cos, sina, sinb, *[arr for arr, _ in cast_next])
    return out[0], out[1], tuple(out[2:])


LAT_QBLOCKS = SEQ // BLOCK
CTX_QBLOCKS = CTX_LEN // BLOCK
BAND = 3 * BLOCK


def _win_kernel(sink_ref, q_ref, k_ref, v_ref, kc_ref, vc_ref, o_ref, s_ref):
    n = pl.program_id(1)
    row = lax.broadcasted_iota(jnp.int32, (WIN_GROUP * BLOCK, 1), 0)

    def head_slice(i):
        return slice(i * HEAD_DIM, (i + 1) * HEAD_DIM)

    def group_q(h):
        return jnp.concatenate([q_ref[:, head_slice(WIN_GROUP * h + g)] for g in range(WIN_GROUP)], axis=0)

    def group_sink(h):
        s0 = sink_ref[0, WIN_GROUP * h]
        s1 = sink_ref[0, WIN_GROUP * h + 1]
        s2 = sink_ref[0, WIN_GROUP * h + 2]
        return jnp.where(row < BLOCK, s0, jnp.where(row < 2 * BLOCK, s1, s2)) * LOG2E

    def store(h, o):
        for g in range(WIN_GROUP):
            o_ref[:, head_slice(WIN_GROUP * h + g)] = o[g * BLOCK:(g + 1) * BLOCK].astype(BF16)

    @pl.when(n < LAT_QBLOCKS)
    def _latent_queries():
        start = pl.multiple_of(jnp.clip((n - 1) * BLOCK, 0, SEQ - BAND), BLOCK)
        qpos = n * BLOCK + (row & (BLOCK - 1))
        kpos = start + lax.broadcasted_iota(jnp.int32, (1, BAND), 1)
        valid = jnp.abs(kpos - qpos) <= WINDOW

        def logits_to_slot(h, slot):
            q = group_q(h)
            s_ref[slot, :, 0:BAND] = jnp.where(valid, _dot_nt(q, k_ref[pl.ds(start, BAND), head_slice(h)]),
                                               -jnp.inf)
            s_ref[slot, :, BAND:] = _dot_nt(q, kc_ref[:, head_slice(h)])

        logits_to_slot(0, 0)
        for h in range(WIN_KV_HEADS):
            hs = head_slice(h)
            slot = h % 2
            if h + 1 < WIN_KV_HEADS:
                logits_to_slot(h + 1, 1 - slot)
            s = s_ref[slot]
            sink = group_sink(h)
            m = jnp.maximum(jnp.max(s, axis=-1, keepdims=True), sink)
            e = jnp.exp2(s - m)
            den = jnp.sum(e, axis=-1, keepdims=True) + jnp.exp2(sink - m)
            e = e.astype(BF16)
            o = _dot(e[:, 0:BAND], v_ref[pl.ds(start, BAND), hs]) + _dot(e[:, BAND:], vc_ref[:, hs])
            store(h, o * (1.0 / den))

    @pl.when(n >= LAT_QBLOCKS)
    def _context_queries():
        for h in range(WIN_KV_HEADS):
            hs = head_slice(h)
            s_ctx = _dot_nt(group_q(h), kc_ref[:, hs])
            sink = group_sink(h)
            m = jnp.maximum(jnp.max(s_ctx, axis=-1, keepdims=True), sink)
            e_ctx = jnp.exp2(s_ctx - m)
            den = jnp.sum(e_ctx, axis=-1, keepdims=True) + jnp.exp2(sink - m)
            store(h, _dot(e_ctx.astype(BF16), vc_ref[:, hs]) * (1.0 / den))


def _win_call(p, sink, layer, with_ctx_queries):
    nq = LAT_QBLOCKS + (CTX_QBLOCKS if with_ctx_queries else 0)
    kv_cols = WIN_KV_HEADS * HEAD_DIM

    def q_row(b, n):
        return jnp.where(n < LAT_QBLOCKS, b * LAT_QBLOCKS + n,
                         N_LAT // BLOCK + b * CTX_QBLOCKS + n - LAT_QBLOCKS)

    ctx_row = lambda b: N_LAT // CTX_LEN + b
    return pl.pallas_call(
        _win_kernel,
        grid=(BATCH, nq),
        in_specs=[
            pl.BlockSpec(memory_space=pltpu.SMEM),
            pl.BlockSpec((BLOCK, WIN_WIDTH), lambda b, n: (q_row(b, n), 0)),
            pl.BlockSpec((SEQ, kv_cols), lambda b, n: (b, COL_KA // kv_cols)),
            pl.BlockSpec((SEQ, kv_cols), lambda b, n: (b, COL_VA // kv_cols)),
            pl.BlockSpec((CTX_LEN, kv_cols), lambda b, n: (ctx_row(b), COL_KA // kv_cols)),
            pl.BlockSpec((CTX_LEN, kv_cols), lambda b, n: (ctx_row(b), COL_VA // kv_cols)),
        ],
        out_specs=pl.BlockSpec((BLOCK, WIN_WIDTH), lambda b, n: (q_row(b, n), 0)),
        out_shape=jax.ShapeDtypeStruct((N_ALL if with_ctx_queries else N_LAT, WIN_WIDTH), BF16),
        scratch_shapes=[pltpu.VMEM((2, WIN_GROUP * BLOCK, BAND + CTX_LEN), F32)],
        compiler_params=_params("arbitrary", "arbitrary"),
        name=f"win_attn_l{layer}",
    )(sink[layer].reshape(1, WIN_Q_HEADS), p, p, p, p, p)


LAT_QTILES = SEQ // TQ_DIFF


def _diff_kernel(*refs, lam_init, with_ctx_queries, n_ada):
    lam_ref, subln_ref, q_ref, k_ref, v_ref, kc_ref, vc_ref = refs[:7]
    rest = list(refs[7:])
    s_ref = rest.pop(-1)
    qc_ref = rest.pop(0) if with_ctx_queries else None
    ada_in = [[rest.pop(0) for _ in range(3)] for _ in range(n_ada)]
    o_ref = rest.pop(0)
    oc_ref = rest.pop(0) if with_ctx_queries else None
    i = pl.program_id(2)
    for job in ada_in:
        _ada_kernel(*job, rest.pop(0))
    dl = lam_ref[...]
    lam = (jnp.exp(jnp.sum(dl[0:1] * dl[1:2], axis=-1, keepdims=True))
           - jnp.exp(jnp.sum(dl[2:3] * dl[3:4], axis=-1, keepdims=True)) + lam_init)
    out_gain = subln_ref[...] * (1.0 - lam_init)

    def head_cols(j):
        return slice(j * HEAD_DIM, (j + 1) * HEAD_DIM)

    def mix(logits):
        maps = []
        for s in logits:
            e = jnp.exp2(s - jnp.max(s, axis=-1, keepdims=True))
            maps.append((e, 1.0 / jnp.sum(e, axis=-1, keepdims=True)))
        (e1, inv1), (e2, inv2) = maps
        return (e1 * inv1 - e2 * (lam * inv2)).astype(BF16)

    def sub_norm(o):
        return _rms(o, out_gain).astype(BF16)

    n_sub = TQ_DIFF // TQ_SUB

    def logits_to_slot(k, slot):
        rows = slice(k * TQ_SUB, (k + 1) * TQ_SUB)
        for j in range(2):
            q = q_ref[rows, head_cols(j)]
            s_ref[slot, j, :, 0:SEQ] = _dot_nt(q, k_ref[:, head_cols(j)])
            s_ref[slot, j, :, SEQ:] = _dot_nt(q, kc_ref[:, head_cols(j)])

    logits_to_slot(0, 0)
    for k in range(n_sub):
        slot = k % 2
        if k + 1 < n_sub:
            logits_to_slot(k + 1, 1 - slot)
        a = mix([s_ref[slot, 0], s_ref[slot, 1]])
        o = _dot(a[:, 0:SEQ], v_ref[...]) + _dot(a[:, SEQ:], vc_ref[...])
        o_ref[k * TQ_SUB:(k + 1) * TQ_SUB, :] = sub_norm(o)

    if with_ctx_queries:
        @pl.when(i == LAT_QTILES - 1)
        def _context_queries():
            q = qc_ref[...]
            a = mix([_dot_nt(q[:, head_cols(j)], kc_ref[:, head_cols(j)]) for j in range(2)])
            oc_ref[...] = sub_norm(_dot(a, vc_ref[...]))


def _diff_call(p, diff_lambda, diff_subln, layer, with_ctx_queries, ada_next=None):
    lam_init = 0.8 - 0.6 * math.exp(-0.3 * layer)
    w = DIFF_V_DIM
    ctx_row = lambda b: N_LAT // CTX_LEN + b
    n_steps = BATCH * DIFF_HEADS * LAT_QTILES
    step = lambda b, h, i: (b * DIFF_HEADS + h) * LAT_QTILES + i
    in_specs = [
        pl.BlockSpec((None, 4, HEAD_DIM), lambda b, h, i: (layer, 0, 0)),
        pl.BlockSpec((None, 1, DIFF_V_DIM), lambda b, h, i: (layer, 0, 0)),
        pl.BlockSpec((TQ_DIFF, w), lambda b, h, i: (b * LAT_QTILES + i, COL_QD // w + h)),
        pl.BlockSpec((SEQ, w), lambda b, h, i: (b, COL_KD // w + h)),
        pl.BlockSpec((SEQ, w), lambda b, h, i: (b, COL_VD // w + h)),
        pl.BlockSpec((CTX_LEN, w), lambda b, h, i: (ctx_row(b), COL_KD // w + h)),
        pl.BlockSpec((CTX_LEN, w), lambda b, h, i: (ctx_row(b), COL_VD // w + h)),
    ]
    out_specs = [pl.BlockSpec((TQ_DIFF, w), lambda b, h, i: (b * LAT_QTILES + i, h))]
    out_shape = [jax.ShapeDtypeStruct((N_LAT, DIFF_WIDTH), BF16)]
    operands = [diff_lambda, diff_subln.reshape(DEPTH, 1, DIFF_V_DIM), p, p, p, p, p]
    if with_ctx_queries:
        in_specs.append(pl.BlockSpec((CTX_LEN, w), lambda b, h, i: (ctx_row(b), COL_QD // w + h)))
        out_specs.append(pl.BlockSpec((CTX_LEN, w), lambda b, h, i: (b, h)))
        out_shape.append(jax.ShapeDtypeStruct((N_CTX, DIFF_WIDTH), BF16))
        operands.append(p)
    n_ada = 0
    if ada_next is not None:
        cc, w_ada, b_ada, jobs = ada_next
        n_ada = len(jobs)
        for ada_layer, col0, width in jobs:
            tn = width // n_steps
            blk0 = col0 // tn
            assert tn % LANES == 0 and col0 % tn == 0
            in_specs += [
                pl.BlockSpec((MOD_ROWS, D_MODEL), lambda b, h, i: (0, 0)),
                pl.BlockSpec((None, D_MODEL, tn),
                             lambda b, h, i, l=ada_layer, o=blk0: (l, 0, o + step(b, h, i))),
                pl.BlockSpec((None, 1, tn), lambda b, h, i, l=ada_layer, o=blk0: (l, 0, o + step(b, h, i))),
            ]
            operands += [cc, w_ada, b_ada.reshape(DEPTH, 1, ADA_WIDTH)]
        for _, _, width in jobs:
            tn = width // n_steps
            out_specs.append(pl.BlockSpec((MOD_ROWS, tn), lambda b, h, i: (0, step(b, h, i))))
            out_shape.append(jax.ShapeDtypeStruct((MOD_ROWS, width), F32))
    out = pl.pallas_call(
        functools.partial(_diff_kernel, lam_init=lam_init, with_ctx_queries=with_ctx_queries, n_ada=n_ada),
        grid=(BATCH, DIFF_HEADS, LAT_QTILES),
        in_specs=in_specs,
        out_specs=out_specs,
        out_shape=out_shape,
        scratch_shapes=[pltpu.VMEM((2, 2, TQ_SUB, SEQ + CTX_LEN), F32)],
        compiler_params=_params("arbitrary", "arbitrary", "arbitrary"),
        name=f"diff_attn_l{layer}",
    )(*operands)
    return (out[:len(out) - n_ada], tuple(out[len(out) - n_ada:]))


def _pool_kernel(u_ref, w_ref, scale_ref, o_ref, pad_ref, *, n_tok):
    t = lax.broadcasted_iota(jnp.int32, (n_tok, 1), 0)
    zeros = jnp.zeros((POOL_HALO, POOL_GROUP), F32)
    pad_ref[0:POOL_HALO, :] = zeros
    pad_ref[POOL_HALO + n_tok:, :] = zeros
    for g, w in enumerate(POOL_WINDOWS):
        cs = slice(g * POOL_GROUP, (g + 1) * POOL_GROUP)
        u = u_ref[:, cs]
        pad_ref[POOL_HALO:POOL_HALO + n_tok, :] = u
        total = pad_ref[POOL_HALO - w // 2:POOL_HALO - w // 2 + n_tok, :]
        for d in range(1 - w // 2, w // 2):
            total = total + pad_ref[POOL_HALO + d:POOL_HALO + d + n_tok, :]
        count = (jnp.clip(t + w // 2, 0, n_tok) - jnp.clip(t - w // 2, 0, n_tok)).astype(F32)
        centred = (total / count - u).astype(BF16)
        o_ref[:, cs] = (_dot(centred, w_ref[g]) * scale_ref[:, cs]).astype(BF16)


def _pool_call(u, pool_w, pool_scale, layer, n_tok, row_block0):
    return pl.pallas_call(
        functools.partial(_pool_kernel, n_tok=n_tok),
        grid=(BATCH,),
        in_specs=[
            pl.BlockSpec((n_tok, POOL_WIDTH), lambda b: (row_block0 + b, 0)),
            pl.BlockSpec((None, len(POOL_WINDOWS), POOL_GROUP, POOL_GROUP), lambda b: (layer, 0, 0, 0)),
            pl.BlockSpec((None, 1, POOL_WIDTH), lambda b: (layer, 0, 0)),
        ],
        out_specs=pl.BlockSpec((n_tok, POOL_WIDTH), lambda b: (b, 0)),
        out_shape=jax.ShapeDtypeStruct((BATCH * n_tok, POOL_WIDTH), BF16),
        scratch_shapes=[pltpu.VMEM((n_tok + 2 * POOL_HALO, POOL_GROUP), F32)],
        compiler_params=_params("arbitrary"),
        name=f"pool_l{layer}_n{n_tok}",
    )(u, pool_w, pool_scale.reshape(DEPTH, 1, POOL_WIDTH))


def _outproj_kernel(s_ref, m_ref, gpost_ref, oa_ref, *rest):
    w_ref, o_ref, stat_ref, vec_ref = rest[-4:]
    oa, ob, oc = oa_ref[...], rest[0][...], rest[1][...]
    if len(rest) == 8:
        latent = pl.program_id(0) < N_LAT // TM_OUT
        ob = jnp.where(latent, ob, rest[2][...])
        oc = jnp.where(latent, oc, rest[3][...])
    b0, c0 = WIN_WIDTH, WIN_WIDTH + POOL_WIDTH
    for n in range(D_MODEL // 1024):
        cs = slice(n * 1024, (n + 1) * 1024)
        o_ref[:, cs] = _dot(oa, w_ref[0:b0, cs]) + _dot(ob, w_ref[b0:c0, cs]) + _dot(oc, w_ref[c0:, cs])
    _gated_residual_to(o_ref, s_ref, o_ref, stat_ref, vec_ref, m_ref[2] * gpost_ref[...], TM_OUT)


def _outproj_call(s, mods, norm_post, o_a, o_b, o_c, w_out, layer, n_rows):
    lat_tiles = N_LAT // TM_OUT
    row_spec = lambda width: pl.BlockSpec((TM_OUT, width), lambda t: (t, 0))
    lat_spec = lambda width: pl.BlockSpec((TM_OUT, width), lambda t: (jnp.minimum(t, lat_tiles - 1), 0))
    ctx_spec = lambda width: pl.BlockSpec((TM_OUT, width), lambda t: (jnp.maximum(t - lat_tiles, 0), 0))
    mixer_specs = [lat_spec(POOL_WIDTH), lat_spec(DIFF_WIDTH)]
    mixers = [o_b[0], o_c[0]]
    if len(o_b) == 2:
        mixer_specs += [ctx_spec(POOL_WIDTH), ctx_spec(DIFF_WIDTH)]
        mixers += [o_b[1], o_c[1]]
    return pl.pallas_call(
        _outproj_kernel,
        grid=(n_rows // TM_OUT,),
        in_specs=[
            row_spec(D_MODEL),
            _mod_spec(1, TM_OUT),
            _norm_spec(layer, 1),
            row_spec(WIN_WIDTH),
        ] + mixer_specs + [
            pl.BlockSpec((D_MODEL, D_MODEL), lambda t: (0, 0), pipeline_mode=pl.Buffered(1)),
        ],
        out_specs=row_spec(D_MODEL),
        out_shape=jax.ShapeDtypeStruct((n_rows, D_MODEL), F32),
        scratch_shapes=_norm_scratch(TM_OUT),
        compiler_params=_params("arbitrary"),
        name=f"outproj_l{layer}",
    )(s, mods, norm_post, o_a, *mixers, w_out)


def _rope_tables():
    row = jnp.repeat(jnp.arange(SEQ // GRID_W, dtype=F32), GRID_W)
    col = jnp.tile(jnp.arange(GRID_W, dtype=F32), SEQ // GRID_W)
    half = HEAD_DIM // 2
    inv = ROPE_THETA ** (-jnp.arange(0, half, 2, dtype=F32) / half)
    ar = row[:, None] * inv
    ac = col[:, None] * inv
    ang = jnp.concatenate([ar, ar, ac, ac], axis=-1)
    cos, sin = jnp.cos(ang), jnp.sin(ang)
    low = (jnp.arange(HEAD_DIM) % half) < half // 2
    return cos, jnp.where(low, -sin, 0.0), jnp.where(low, 0.0, sin)


def kernel(x, c, ctx, c_ctx, w_ada, b_ada, norm_pre, norm_post, ffn_w_gate, ffn_w_up, ffn_w_down,
           w_in, w_out, attn_sink, pool_w, pool_scale, diff_lambda, diff_subln):
    srcs = (x.reshape(N_LAT, D_MODEL), ctx.reshape(N_CTX, D_MODEL))
    cc = jnp.zeros((MOD_ROWS, D_MODEL), F32).at[:BATCH].set(c).at[BATCH].set(c_ctx)
    group_w = 3 * D_MODEL
    mods = _ada_call(cc, w_ada, b_ada, 0, 2 * group_w).reshape(MOD_ROWS, 2, 3, 1, D_MODEL)
    npre = norm_pre.reshape(DEPTH, 3, 1, D_MODEL)
    npost = norm_post.reshape(DEPTH, 3, 1, D_MODEL)
    pool_w16 = pool_w.astype(BF16)
    cos, sina, sinb = _rope_tables()

    ffn_f32 = (ffn_w_gate, ffn_w_up, ffn_w_down)
    ffn_order = [(layer, half) for layer in range(DEPTH) for half in range(2)]
    w_out16 = w_out[0].astype(BF16)

    def next_ffn(layer, half):
        i = ffn_order.index((layer, half)) + 1
        return [(w, ffn_order[i]) for w in ffn_f32] if i < len(ffn_order) else []

    for layer in range(DEPTH):
        update_ctx = layer < DEPTH - 1
        n_tiles = ALL_TILES if update_ctx else LAT_TILES
        jobs = next_ffn(layer, 0)
        if layer == 0:
            jobs, down_job = jobs[:2] + [(w_in, (0,))], jobs[2:]
            first, ffn16 = _ffn_call(srcs[:1], mods, npre, npost, ffn_f32, 0, 0, 1, f32_weights=True)
            s, done = _ffn_call(srcs, mods, npre, npost, ffn16, 0, 0, ALL_TILES - 1, jobs, tile0=1, head=first)
            ffn16, w_in16 = done[:2], done[2]
        else:
            s, ffn16 = _ffn_call(srcs, mods, npre, npost, ffn16, layer, 0, ALL_TILES, jobs)
            down_job = []
        proj_next = ([(w_in, (layer + 1,)), (w_out, (layer + 1,))] if update_ctx else []) + down_job
        p, u, proj16 = _inproj_call(s, mods, npre, w_in16, cos, sina, sinb, layer, proj_next)
        if down_job:
            ffn16, proj16 = ffn16 + proj16[-1:], proj16[:-1]
        o_a = _win_call(p, attn_sink, layer, update_ctx)
        ada_jobs = [(layer, 2 * group_w, group_w)] if layer == 0 else []
        if update_ctx:
            ada_jobs.append((layer + 1, 0, ADA_WIDTH))
        o_c, tables = _diff_call(p, diff_lambda, diff_subln, layer, update_ctx,
                                 (cc, w_ada, b_ada, ada_jobs) if ada_jobs else None)
        if layer == 0:
            last_group = tables[0].reshape(MOD_ROWS, 1, 3, 1, D_MODEL)
            mods = jnp.concatenate([mods, last_group], axis=1)
        o_b =[_pool_call(u, pool_w16, pool_scale, layer, SEQ, 0)]
        if update_ctx:
            o_b.append(_pool_call(u, pool_w16, pool_scale, layer, CTX_LEN, N_LAT // CTX_LEN))
        s = _outproj_call(s, mods, npost, o_a, o_b, o_c, w_out16, layer, n_tiles * TM)
        s, ffn16 = _ffn_call((s,), mods, npre, npost, ffn16, layer, 1, n_tiles, next_ffn(layer, 1))
        srcs = (s,)
        if update_ctx:
            w_in16, w_out16 = proj16
            mods = tables[-1].reshape(MOD_ROWS, 3, 3, 1, D_MODEL)
    return s.reshape(BATCH, SEQ, D_MODEL)
```
